```python
import jax, jax.numpy as jnp
from jax import lax
import numpy as np

D_MODEL = 2048
BATCH = 2
SEQ = 4096
DEPTH = 1
DEC_BATCH = 32
DEC_SEQ = 1
PAST_LEN = 8192
PAGE_SIZE = 128

HEAD_DIM = 128
ATTN_W = D_MODEL // 2
ATTN_HEADS = ATTN_W // HEAD_DIM
CONV_W = D_MODEL - ATTN_W
CONV_GROUPS = CONV_W // HEAD_DIM
MIX_W = ATTN_W + CONV_W
CONV_K = 31
MOBA_BLOCK = 256
MOBA_TOPK = 3
Q_CHUNK = 32
PLE_DIM = 256
IN_W = 4 * ATTN_W + 3 * CONV_W
DN_ALPHA = (2 * DEPTH) ** 0.25
DN_BETA = (8 * DEPTH) ** -0.25
LN_EPS = 1e-5
NEG = -1e30

kernel_name = 'hymba_conformer_moba_decoder_step'


def layer_norm(x, g, b):
    xf = x.astype(jnp.float32)
    mu = jnp.mean(xf, axis=-1, keepdims=True)
    var = jnp.mean(jnp.square(xf - mu), axis=-1, keepdims=True)
    y = (xf - mu) * lax.rsqrt(var + LN_EPS) * g.astype(jnp.float32) + b.astype(jnp.float32)
    return y.astype(x.dtype)


def moba_attention(q, k, v, q_pos0):
    B, Sq, H, Dh = q.shape
    T = k.shape[1]
    nb = -(-T // MOBA_BLOCK)
    pad_t = nb * MOBA_BLOCK - T
    kb = jnp.pad(k, ((0, 0), (0, pad_t), (0, 0), (0, 0))).reshape(B, nb, MOBA_BLOCK, H, Dh)
    vb = jnp.pad(v, ((0, 0), (0, pad_t), (0, 0), (0, 0))).reshape(B, nb, MOBA_BLOCK, H, Dh)
    k_mean = jnp.mean(kb.astype(jnp.float32), axis=2)
    qc = min(Q_CHUNK, Sq)
    n_chunk = -(-Sq // qc)
    sq_pad = n_chunk * qc
    q_p = jnp.pad(q, ((0, 0), (0, sq_pad - Sq), (0, 0), (0, 0)))
    pos = q_pos0 + jnp.arange(sq_pad)
    own = pos // MOBA_BLOCK
    gate = jnp.einsum('bshd,bnhd->bhsn', q_p.astype(jnp.float32), k_mean)
    cand = jnp.arange(nb)[None, :] < own[:, None]
    gate = jnp.where(cand[None, None], gate, -jnp.inf)
    n_sel = min(MOBA_TOPK, nb)
    _, sel = lax.top_k(gate, n_sel)
    sel_valid = jnp.arange(n_sel)[None, :] < own[:, None]
    own_c = jnp.minimum(own, nb - 1)
    blocks = jnp.concatenate(
        [sel, jnp.broadcast_to(own_c[None, None, :, None], (B, H, sq_pad, 1))], axis=-1)
    valid = jnp.concatenate([sel_valid, jnp.ones((sq_pad, 1), dtype=bool)], axis=-1)
    n_blk = n_sel + 1
    scale = HEAD_DIM ** -0.5
    bi = jnp.arange(B)[:, None, None, None]
    hi = jnp.arange(H)[None, :, None, None]

    def attend(args):
        q_c, blk, val, p_c = args
        ks = kb[bi, blk, :, hi]
        vs = vb[bi, blk, :, hi]
        s = jnp.einsum('bqhd,bhqnld->bhqnl', q_c, ks, preferred_element_type=jnp.float32) * scale
        key_pos = blk[..., None] * MOBA_BLOCK + jnp.arange(MOBA_BLOCK)
        mask = val[None, None, :, :, None] & (key_pos <= p_c[None, None, :, None, None])
        s = jnp.where(mask, s, NEG)
        pr = jax.nn.softmax(s.reshape(B, H, qc, n_blk * MOBA_BLOCK), axis=-1).reshape(s.shape)
        return jnp.einsum('bhqnl,bhqnld->bqhd', pr.astype(vs.dtype), vs)

    q_chunks = q_p.reshape(B, n_chunk, qc, H, Dh).transpose(1, 0, 2, 3, 4)
    blk_chunks = blocks.reshape(B, H, n_chunk, qc, n_blk).transpose(2, 0, 1, 3, 4)
    val_chunks = valid.reshape(n_chunk, qc, n_blk)
    pos_chunks = pos.reshape(n_chunk, qc)
    out = lax.map(attend, (q_chunks, blk_chunks, val_chunks, pos_chunks))
    out = out.transpose(1, 0, 2, 3, 4).reshape(B, sq_pad, H, Dh)
    return out[:, :Sq]


def causal_depthwise_conv(u_ext, w, b):
    y = lax.conv_general_dilated(u_ext, w[:, None, :].astype(u_ext.dtype), window_strides=(1,), padding='VALID',
                                 dimension_numbers=('NWC', 'WIO', 'NWC'), feature_group_count=u_ext.shape[-1])
    return y + b


def mixer_layer(x, p, conv_hist, k_past, v_past, q_pos0,
                w_in, b_in, w_dw, b_dw, g_cn, b_cn, w_pw, b_pw, w_out, b_out, g_ln, b_ln, w_pe, w_pg, b_pg):
    B, S, _ = x.shape
    z = x @ w_in + b_in
    q, k, v, g_a, a, bg, g_c = jnp.split(
        z, [ATTN_W, 2 * ATTN_W, 3 * ATTN_W, 4 * ATTN_W, 4 * ATTN_W + CONV_W, 4 * ATTN_W + 2 * CONV_W], axis=-1)
    q = q.reshape(B, S, ATTN_HEADS, HEAD_DIM)
    k = k.reshape(B, S, ATTN_HEADS, HEAD_DIM)
    v = v.reshape(B, S, ATTN_HEADS, HEAD_DIM)
    k_all = k if k_past is None else jnp.concatenate([k_past, k], axis=1)
    v_all = v if v_past is None else jnp.concatenate([v_past, v], axis=1)
    attn = moba_attention(q, k_all, v_all, q_pos0).reshape(B, S, ATTN_W) * jax.nn.silu(g_a)
    u = a * jax.nn.sigmoid(bg)
    u_ext = jnp.concatenate([conv_hist.astype(u.dtype), u], axis=1)
    c = causal_depthwise_conv(u_ext, w_dw, b_dw)
    c = jax.nn.silu(layer_norm(c, g_cn, b_cn))
    c = (c @ w_pw + b_pw) * jax.nn.silu(g_c)
    mix = jnp.concatenate([attn, c], axis=-1) @ w_out + b_out
    h = layer_norm(DN_ALPHA * x + mix, g_ln, b_ln)
    h = h + jax.nn.sigmoid(h @ w_pg + b_pg) * (p @ w_pe)
    new_conv = u_ext[:, -(CONV_K - 1):]
    return h, k, v, new_conv


def setup_inputs(seed: int = 0) -> dict:
    key = jax.random.key(seed)
    ks = jax.random.split(key, 26)
    f32 = jnp.float32
    n_pages = PAST_LEN // PAGE_SIZE
    n_phys = (5 * DEC_BATCH * n_pages) // 4
    nrm = lambda k, shape, s: jax.random.normal(k, shape, f32) * s
    page_table = jax.random.permutation(ks[0], n_phys)[:DEC_BATCH * n_pages].reshape(DEC_BATCH, n_pages).astype(jnp.int32)
    return {
        'x_prompt': nrm(ks[1], (BATCH, SEQ, D_MODEL), 1.0),
        'x_sample': nrm(ks[2], (DEC_BATCH, DEC_SEQ, D_MODEL), 1.0),
        'p_prompt': nrm(ks[3], (DEPTH, BATCH, SEQ, PLE_DIM), 1.0),
        'p_sample': nrm(ks[4], (DEPTH, DEC_BATCH, DEC_SEQ, PLE_DIM), 1.0),
        'cache_k': nrm(ks[5], (DEPTH, n_phys, PAGE_SIZE, ATTN_HEADS, HEAD_DIM), 1.0),
        'cache_v': nrm(ks[6], (DEPTH, n_phys, PAGE_SIZE, ATTN_HEADS, HEAD_DIM), 1.0),
        'state_conv': nrm(ks[7], (DEPTH, DEC_BATCH, CONV_K - 1, CONV_W), 0.5),
        'page_table': page_table,
        'w_in': nrm(ks[8], (DEPTH, D_MODEL, IN_W), D_MODEL ** -0.5),
        'b_in': nrm(ks[9], (DEPTH, IN_W), 0.01),
        'w_dw': nrm(ks[10], (DEPTH, CONV_K, CONV_W), CONV_K ** -0.5),
        'b_dw': nrm(ks[11], (DEPTH, CONV_W), 0.01),
        'g_cn': 1.0 + nrm(ks[12], (DEPTH, CONV_W), 0.01),
        'b_cn': nrm(ks[13], (DEPTH, CONV_W), 0.01),
        'w_pw': nrm(ks[14], (DEPTH, CONV_W, CONV_W), DN_BETA * CONV_W ** -0.5),
        'b_pw': nrm(ks[15], (DEPTH, CONV_W), 0.01),
        'w_out': nrm(ks[16], (DEPTH, MIX_W, D_MODEL), DN_BETA * MIX_W ** -0.5),
        'b_out': nrm(ks[17], (DEPTH, D_MODEL), 0.01),
        'g_ln': 1.0 + nrm(ks[18], (DEPTH, D_MODEL), 0.01),
        'b_ln': nrm(ks[19], (DEPTH, D_MODEL), 0.01),
        'w_pe': nrm(ks[20], (DEPTH, PLE_DIM, D_MODEL), PLE_DIM ** -0.5),
        'w_pg': nrm(ks[21], (DEPTH, D_MODEL, D_MODEL), D_MODEL ** -0.5),
        'b_pg': nrm(ks[22], (DEPTH, D_MODEL), 0.01),
    }


def reference(x_prompt, x_sample, p_prompt, p_sample, cache_k, cache_v, state_conv, page_table,
              w_in, b_in, w_dw, b_dw, g_cn, b_cn, w_pw, b_pw, w_out, b_out, g_ln, b_ln, w_pe, w_pg, b_pg):
    n_pages = page_table.shape[1]
    past = n_pages * PAGE_SIZE
    dec_b = x_sample.shape[0]
    hp, hs = x_prompt, x_sample
    kp_l, vp_l, cp_l, ks_l, vs_l, cs_l = [], [], [], [], [], []
    for i in range(DEPTH):
        wts = (w_in[i], b_in[i], w_dw[i], b_dw[i], g_cn[i], b_cn[i], w_pw[i], b_pw[i],
               w_out[i], b_out[i], g_ln[i], b_ln[i], w_pe[i], w_pg[i], b_pg[i])
        hist0 = jnp.zeros((hp.shape[0], CONV_K - 1, CONV_W), hp.dtype)
        hp, kp, vp, cp = mixer_layer(hp, p_prompt[i], hist0, None, None, 0, *wts)
        k_past = cache_k[i][page_table].reshape(dec_b, past, ATTN_HEADS, HEAD_DIM)
        v_past = cache_v[i][page_table].reshape(dec_b, past, ATTN_HEADS, HEAD_DIM)
        hs, ksn, vsn, csn = mixer_layer(hs, p_sample[i], state_conv[i], k_past, v_past, past, *wts)
        kp_l.append(kp); vp_l.append(vp); cp_l.append(cp)
        ks_l.append(ksn); vs_l.append(vsn); cs_l.append(csn)
    return (hp, hs, jnp.stack(kp_l), jnp.stack(vp_l), jnp.stack(cp_l),
            jnp.stack(ks_l), jnp.stack(vs_l), jnp.stack(cs_l))
```

```python
import functools

import jax
import jax.numpy as jnp
from jax import lax
from jax.experimental import pallas as pl
from jax.experimental.pallas import tpu as pltpu

F32 = jnp.float32
BF16 = jnp.bfloat16

HEAD_DIM = 128
CONV_K = 31
MOBA_BLOCK = 256
MOBA_TOPK = 3
PAGE_SIZE = 128
PAGES_PER_BLOCK = MOBA_BLOCK // PAGE_SIZE
LN_EPS = 1e-5
NEG = -1e30
LANES = 128
CONV_HALO = 32
VMEM_LIMIT = 56 * 1024 * 1024


def _silu(z):
    return z * jax.nn.sigmoid(z)


def _dot_nt(a, b, **kw):
    return lax.dot_general(a, b, (((1,), (1,)), ((), ())), preferred_element_type=F32, **kw)


def _in_proj_kernel(x_ref, w_ref, b_ref, q_ref, k_ref, v_ref, sga_ref, u_ref, sgc_ref, xb_ref, a_ref):
    j = pl.program_id(1)

    @pl.when(j == 0)
    def _():
        xb_ref[...] = x_ref[...].astype(BF16)

    z = jnp.dot(xb_ref[...], w_ref[...], preferred_element_type=F32) + b_ref[...]

    @pl.when(j == 0)
    def _():
        q_ref[...] = z

    @pl.when(j == 1)
    def _():
        k_ref[...] = z

    @pl.when(j == 2)
    def _():
        v_ref[...] = z

    @pl.when(j == 3)
    def _():
        sga_ref[...] = _silu(z)

    @pl.when(j == 4)
    def _():
        a_ref[...] = z

    @pl.when(j == 5)
    def _():
        u_ref[...] = a_ref[...] * jax.nn.sigmoid(z)

    @pl.when(j == 6)
    def _():
        sgc_ref[...] = _silu(z)


def _in_proj(x, w_b, b, tm):
    n, d = x.shape
    gw = w_b.shape[1] // 7
    row = pl.BlockSpec((tm, gw), lambda i, j: (i, 0))
    out = jax.ShapeDtypeStruct((n, gw), F32)
    return pl.pallas_call(
        _in_proj_kernel,
        grid=(n // tm, 7),
        in_specs=[
            pl.BlockSpec((tm, d), lambda i, j: (i, 0)),
            pl.BlockSpec((d, gw), lambda i, j: (0, j)),
            pl.BlockSpec((1, gw), lambda i, j: (0, j)),
        ],
        out_specs=[row] * 6,
        out_shape=[out] * 6,
        scratch_shapes=[pltpu.VMEM((tm, d), BF16), pltpu.VMEM((tm, gw), F32)],
        compiler_params=pltpu.CompilerParams(
            dimension_semantics=("arbitrary", "arbitrary"), vmem_limit_bytes=VMEM_LIMIT),
        name="in_proj",
    )(x, w_b, b)


def _topk_mask(g, idx, n_valid, axis):
    sel = jnp.zeros(g.shape, F32)
    big = g.shape[axis]
    for t in range(MOBA_TOPK):
        m = jnp.max(g, axis=axis, keepdims=True)
        first = jnp.min(jnp.where(g == m, idx, big), axis=axis, keepdims=True)
        hit = idx == first
        counts = jnp.where(t < n_valid, 1.0, 0.0)
        sel = jnp.maximum(sel, jnp.where(hit, counts, 0.0))
        g = jnp.where(hit, -jnp.inf, g)
    return sel


def _moba_prompt_kernel(q_ref, k_ref, v_ref, sga_ref, o_ref, kb_ref, vb_ref, kmean_ref, *, n_blocks):
    i = pl.program_id(2)
    blk = MOBA_BLOCK
    scale = HEAD_DIM ** -0.5

    @pl.when(i == 0)
    def _():
        kf = k_ref[...]
        kb_ref[...] = kf.astype(BF16)
        vb_ref[...] = v_ref[...].astype(BF16)
        kmean_ref[...] = jnp.zeros(kmean_ref.shape, F32)
        kmean_ref[0:n_blocks, :] = jnp.mean(kf.reshape(n_blocks, blk, HEAD_DIM), axis=1)

    qf = q_ref[...]
    qb = qf.astype(BF16)

    g = _dot_nt(qf, kmean_ref[...], precision=lax.Precision.HIGHEST)
    col = lax.broadcasted_iota(jnp.int32, g.shape, 1)
    g = jnp.where(col < i, g, -jnp.inf)
    sel = _topk_mask(g, col, i, axis=1)

    start = pl.multiple_of(i * blk, blk)
    s = _dot_nt(qb, kb_ref[pl.ds(start, blk), :]) * scale
    r_id = lax.broadcasted_iota(jnp.int32, s.shape, 0)
    c_id = lax.broadcasted_iota(jnp.int32, s.shape, 1)
    s = jnp.where(c_id <= r_id, s, NEG)
    m0 = jnp.max(s, axis=1, keepdims=True)
    p = jnp.exp(s - m0)
    l0 = jnp.sum(p, axis=1, keepdims=True)
    acc0 = jnp.dot(p.astype(BF16), vb_ref[pl.ds(start, blk), :], preferred_element_type=F32)

    def body(j, carry):
        m, l, acc = carry
        st = pl.multiple_of(j * blk, blk)
        sj = _dot_nt(qb, kb_ref[pl.ds(st, blk), :]) * scale
        picked = jnp.max(jnp.where(col == j, sel, 0.0), axis=1, keepdims=True) > 0.5
        sj = jnp.where(picked, sj, NEG)
        m_new = jnp.maximum(m, jnp.max(sj, axis=1, keepdims=True))
        alpha = jnp.exp(m - m_new)
        pj = jnp.exp(sj - m_new)
        l_new = alpha * l + jnp.sum(pj, axis=1, keepdims=True)
        acc_new = alpha * acc + jnp.dot(pj.astype(BF16), vb_ref[pl.ds(st, blk), :], preferred_element_type=F32)
        return m_new, l_new, acc_new

    _, l, acc = lax.fori_loop(0, i, body, (m0, l0, acc0))
    o_ref[...] = (acc / l * sga_ref[...]).astype(o_ref.dtype)


def _moba_prompt(q, k, v, sga, batch, seq, heads):
    n_blocks = seq // MOBA_BLOCK
    qspec = pl.BlockSpec((MOBA_BLOCK, HEAD_DIM), lambda b, h, i: (b * n_blocks + i, h))
    kvspec = pl.BlockSpec((seq, HEAD_DIM), lambda b, h, i: (b, h))
    return pl.pallas_call(
        functools.partial(_moba_prompt_kernel, n_blocks=n_blocks),
        grid=(batch, heads, n_blocks),
        in_specs=[qspec, kvspec, kvspec, qspec],
        out_specs=qspec,
        out_shape=jax.ShapeDtypeStruct(q.shape, BF16),
        scratch_shapes=[pltpu.VMEM((seq, HEAD_DIM), BF16), pltpu.VMEM((seq, HEAD_DIM), BF16),
                        pltpu.VMEM((LANES, HEAD_DIM), F32)],
        compiler_params=pltpu.CompilerParams(
            dimension_semantics=("arbitrary", "arbitrary", "arbitrary"), vmem_limit_bytes=VMEM_LIMIT),
        name="moba_prompt",
    )(q, k, v, sga)


def _conv_tail(c, sgc, gcn_ref, bcn_ref, wpw_ref, bpw_ref):
    mu = jnp.mean(c, axis=-1, keepdims=True)
    d = c - mu
    var = jnp.mean(d * d, axis=-1, keepdims=True)
    y = d * lax.rsqrt(var + LN_EPS) * gcn_ref[...] + bcn_ref[...]
    y = _silu(y)
    return (jnp.dot(y.astype(BF16), wpw_ref[...], preferred_element_type=F32) + bpw_ref[...]) * sgc


def _conv_prompt_kernel(u_ref, prev_ref, sgc_ref, wdw_ref, bdw_ref, gcn_ref, bcn_ref, wpw_ref, bpw_ref,
                        o_ref, ext_ref, c_ref, *, tiles_per_seq, rows):
    i = pl.program_id(0)
    t, w = u_ref.shape
    first = (i % tiles_per_seq) == 0
    ext_ref[0:CONV_HALO, :] = jnp.where(first, 0.0, prev_ref[...])
    ext_ref[CONV_HALO:, :] = u_ref[...]
    off = CONV_HALO - (CONV_K - 1)
    for c in range(w // LANES):
        cs = slice(c * LANES, (c + 1) * LANES)
        for r in range(t // rows):
            acc = jnp.broadcast_to(bdw_ref[:, cs], (rows, LANES))
            for j in range(CONV_K):
                lo = r * rows + off + j
                acc = acc + ext_ref[lo:lo + rows, cs] * wdw_ref[j:j + 1, cs]
            c_ref[r * rows:(r + 1) * rows, cs] = acc
    o_ref[...] = _conv_tail(c_ref[...], sgc_ref[...], gcn_ref, bcn_ref, wpw_ref, bpw_ref).astype(o_ref.dtype)


def _conv_prompt(u, sgc, w_dw, b_dw, g_cn, b_cn, w_pw_b, b_pw, seq, tile):
    n, w = u.shape
    halo_per_tile = tile // CONV_HALO
    full = lambda shape: pl.BlockSpec(shape, lambda i: (0, 0))
    rowspec = pl.BlockSpec((tile, w), lambda i: (i, 0))
    return pl.pallas_call(
        functools.partial(_conv_prompt_kernel, tiles_per_seq=seq // tile, rows=64),
        grid=(n // tile,),
        in_specs=[
            rowspec,
            pl.BlockSpec((CONV_HALO, w), lambda i: (jnp.maximum(i * halo_per_tile - 1, 0), 0)),
            rowspec,
            full((CONV_K, w)), full((1, w)), full((1, w)), full((1, w)), full((w, w)), full((1, w)),
        ],
        out_specs=rowspec,
        out_shape=jax.ShapeDtypeStruct((n, w), BF16),
        scratch_shapes=[pltpu.VMEM((tile + CONV_HALO, w), F32), pltpu.VMEM((tile, w), F32)],
        compiler_params=pltpu.CompilerParams(dimension_semantics=("arbitrary",), vmem_limit_bytes=VMEM_LIMIT),
        name="conv_prompt",
    )(u, u, sgc, w_dw, b_dw, g_cn, b_cn, w_pw_b, b_pw)


def _conv_sample_kernel(state_ref, u_ref, sgc_ref, wdw_ref, bdw_ref, gcn_ref, bcn_ref, wpw_ref, bpw_ref,
                        o_ref, new_ref, c_ref):
    nb = state_ref.shape[0]
    hist = CONV_K - 1
    for b in range(nb):
        st = state_ref[b]
        u_row = u_ref[b:b + 1, :]
        c_ref[b:b + 1, :] = (jnp.sum(st * wdw_ref[0:hist, :], axis=0, keepdims=True)
                             + u_row * wdw_ref[hist:hist + 1, :] + bdw_ref[...])
        new_ref[b, 0:hist - 1, :] = state_ref[b, 1:hist, :]
        new_ref[b, hist - 1:hist, :] = u_row
    o_ref[...] = _conv_tail(c_ref[...], sgc_ref[...], gcn_ref, bcn_ref, wpw_ref, bpw_ref).astype(o_ref.dtype)


def _conv_sample(state, u, sgc, w_dw, b_dw, g_cn, b_cn, w_pw_b, b_pw):
    nb, hist, w = state.shape
    full2 = lambda shape: pl.BlockSpec(shape, lambda i: (0, 0))
    full3 = lambda shape: pl.BlockSpec(shape, lambda i: (0, 0, 0))
    return pl.pallas_call(
        _conv_sample_kernel,
        grid=(1,),
        in_specs=[full3((nb, hist, w)), full2((nb, w)), full2((nb, w)),
                  full2((CONV_K, w)), full2((1, w)), full2((1, w)), full2((1, w)), full2((w, w)), full2((1, w))],
        out_specs=[full2((nb, w)), full3((nb, hist, w))],
        out_shape=[jax.ShapeDtypeStruct((nb, w), BF16), jax.ShapeDtypeStruct((nb, hist, w), F32)],
        scratch_shapes=[pltpu.VMEM((nb, w), F32)],
        compiler_params=pltpu.CompilerParams(dimension_semantics=("arbitrary",), vmem_limit_bytes=VMEM_LIMIT),
        name="conv_sample",
    )(state, u, sgc, w_dw, b_dw, g_cn, b_cn, w_pw_b, b_pw)


def _out_kernel(x_ref, a_ref, c_ref, p_ref, woa_ref, woc_ref, bo_ref, g_ref, b_ref, wpg_ref, bpg_ref, wpe_ref,
                o_ref, *, alpha):
    mix = (jnp.dot(a_ref[...].astype(BF16), woa_ref[...], preferred_element_type=F32)
           + jnp.dot(c_ref[...].astype(BF16), woc_ref[...], preferred_element_type=F32) + bo_ref[...])
    t = alpha * x_ref[...] + mix
    mu = jnp.mean(t, axis=-1, keepdims=True)
    d = t - mu
    var = jnp.mean(d * d, axis=-1, keepdims=True)
    h = d * lax.rsqrt(var + LN_EPS) * g_ref[...] + b_ref[...]
    gate = jax.nn.sigmoid(jnp.dot(h.astype(BF16), wpg_ref[...], preferred_element_type=F32) + bpg_ref[...])
    pe = jnp.dot(p_ref[...].astype(BF16), wpe_ref[...], preferred_element_type=F32)
    o_ref[...] = h + gate * pe


def _out_proj(x, attn, conv, p, w_out_b, b_out, g_ln, b_ln, w_pg_b, b_pg, w_pe_b, alpha, tm):
    n, d = x.shape
    half = attn.shape[1]
    pd = p.shape[1]
    const = lambda shape, r=0: pl.BlockSpec(shape, lambda i: (r, 0), pipeline_mode=pl.Buffered(1))
    row = lambda w: pl.BlockSpec((tm, w), lambda i: (i, 0))
    return pl.pallas_call(
        functools.partial(_out_kernel, alpha=alpha),
        grid=(n // tm,),
        in_specs=[row(d), row(half), row(half), row(pd),
                  const((half, d), 0), const((half, d), 1), const((1, d)), const((1, d)), const((1, d)),
                  const((d, d)), const((1, d)), const((pd, d))],
        out_specs=row(d),
        out_shape=jax.ShapeDtypeStruct((n, d), F32),
        compiler_params=pltpu.CompilerParams(dimension_semantics=("arbitrary",), vmem_limit_bytes=VMEM_LIMIT),
        name="out_proj",
    )(x, attn, conv, p, w_out_b, w_out_b, b_out, g_ln, b_ln, w_pg_b, b_pg, w_pe_b)


def _kmean_kernel(pt_ref, *refs, pages_per_step):
    del pt_ref
    page_refs, o_ref = refs[:pages_per_step], refs[pages_per_step]
    s = pl.program_id(1)
    blocks_per_step = pages_per_step // PAGES_PER_BLOCK
    for a in range(blocks_per_step):
        tot = jnp.sum(page_refs[PAGES_PER_BLOCK * a][0], axis=0, keepdims=True)
        for r in range(1, PAGES_PER_BLOCK):
            tot = tot + jnp.sum(page_refs[PAGES_PER_BLOCK * a + r][0], axis=0, keepdims=True)
        o_ref[0, pl.ds(s * blocks_per_step + a, 1), :] = tot * (1.0 / MOBA_BLOCK)


def _kmean_sample(page_table_flat, cache, n_seq, n_pages, pages_per_step):
    n_phys, page, w = cache.shape
    n_blocks = n_pages // PAGES_PER_BLOCK
    steps = n_pages // pages_per_step

    def page_spec(r):
        return pl.BlockSpec(
            (1, page, w), lambda b, s, pt: (pt[b * n_pages + s * pages_per_step + r], 0, 0))

    return pl.pallas_call(
        functools.partial(_kmean_kernel, pages_per_step=pages_per_step),
        grid_spec=pltpu.PrefetchScalarGridSpec(
            num_scalar_prefetch=1,
            grid=(n_seq, steps),
            in_specs=[page_spec(r) for r in range(pages_per_step)],
            out_specs=pl.BlockSpec((1, n_blocks, w), lambda b, s, pt: (b, 0, 0)),
        ),
        out_shape=jax.ShapeDtypeStruct((n_seq, n_blocks, w), F32),
        compiler_params=pltpu.CompilerParams(
            dimension_semantics=("arbitrary", "arbitrary"), vmem_limit_bytes=VMEM_LIMIT),
        name="kmean_sample",
    )(page_table_flat, *([cache] * pages_per_step))


def _select_kernel(q_ref, kmean_ref, o_ref):
    prod = kmean_ref[0] * q_ref[0]
    n_blocks, w = prod.shape
    row = lax.broadcasted_iota(jnp.int32, (w, LANES), 0)
    colh = lax.broadcasted_iota(jnp.int32, (w, LANES), 1)
    head_sum = jnp.where(row // HEAD_DIM == colh, 1.0, 0.0)
    g = jnp.dot(prod, head_sum, preferred_element_type=F32, precision=lax.Precision.HIGHEST)
    idx = lax.broadcasted_iota(jnp.int32, g.shape, 0)
    out = jnp.zeros((8, LANES), jnp.int32)
    rid = lax.broadcasted_iota(jnp.int32, (8, LANES), 0)
    for t in range(MOBA_TOPK):
        m = jnp.max(g, axis=0, keepdims=True)
        first = jnp.min(jnp.where(g == m, idx, n_blocks), axis=0, keepdims=True)
        out = jnp.where(rid == t, first, out)
        g = jnp.where(idx == first, -jnp.inf, g)
    o_ref[0] = out


def _select_sample(q3, kmean, heads):
    n_seq, n_blocks, w = kmean.shape
    return pl.pallas_call(
        _select_kernel,
        grid=(n_seq,),
        in_specs=[pl.BlockSpec((1, 1, w), lambda b: (b, 0, 0)),
                  pl.BlockSpec((1, n_blocks, w), lambda b: (b, 0, 0))],
        out_specs=pl.BlockSpec((1, 8, LANES), lambda b: (b, 0, 0)),
        out_shape=jax.ShapeDtypeStruct((n_seq, 8, LANES), jnp.int32),
        compiler_params=pltpu.CompilerParams(dimension_semantics=("arbitrary",), vmem_limit_bytes=VMEM_LIMIT),
        name="select_sample",
    )(q3, kmean)


def _moba_sample_kernel(sel_ref, pt_ref, q_ref, kn_ref, vn_ref, sga_ref, *refs, n_pages_sel):
    del sel_ref, pt_ref
    k_refs, v_refs, o_ref = refs[:n_pages_sel], refs[n_pages_sel:2 * n_pages_sel], refs[2 * n_pages_sel]
    scale = HEAD_DIM ** -0.5
    qb = q_ref[0].astype(BF16)
    s_new = jnp.sum(q_ref[0] * kn_ref[0], axis=1, keepdims=True) * scale
    s = [_dot_nt(qb, kr[0].astype(BF16)) * scale for kr in k_refs]
    m = s_new
    for sp in s:
        m = jnp.maximum(m, jnp.max(sp, axis=1, keepdims=True))
    p_new = jnp.exp(s_new - m)
    l = p_new
    acc = p_new * vn_ref[0]
    for sp, vr in zip(s, v_refs):
        pp = jnp.exp(sp - m)
        l = l + jnp.sum(pp, axis=1, keepdims=True)
        acc = acc + jnp.dot(pp.astype(BF16), vr[0].astype(BF16), preferred_element_type=F32)
    o_ref[0] = (acc / l * sga_ref[0]).astype(o_ref.dtype)


def _moba_sample(sel_flat, page_table_flat, q3, kn3, vn3, sga3, cache_k, cache_v, n_pages, heads):
    n_seq = q3.shape[0]
    n_pages_sel = MOBA_TOPK * PAGES_PER_BLOCK
    vec = pl.BlockSpec((1, 1, HEAD_DIM), lambda b, h, sel, pt: (b, 0, h))

    def page_spec(t, r):
        def index(b, h, sel, pt):
            blk = sel[(b * 8 + t) * LANES + h]
            return (pt[b * n_pages + blk * PAGES_PER_BLOCK + r], 0, h)
        return pl.BlockSpec((1, PAGE_SIZE, HEAD_DIM), index)

    pages = [page_spec(t, r) for t in range(MOBA_TOPK) for r in range(PAGES_PER_BLOCK)]
    return pl.pallas_call(
        functools.partial(_moba_sample_kernel, n_pages_sel=n_pages_sel),
        grid_spec=pltpu.PrefetchScalarGridSpec(
            num_scalar_prefetch=2,
            grid=(n_seq, heads),
            in_specs=[vec, vec, vec, vec] + pages + pages,
            out_specs=vec,
        ),
        out_shape=jax.ShapeDtypeStruct(q3.shape, F32),
        compiler_params=pltpu.CompilerParams(
            dimension_semantics=("arbitrary", "arbitrary"), vmem_limit_bytes=VMEM_LIMIT),
        name="moba_sample",
    )(sel_flat, page_table_flat, q3, kn3, vn3, sga3, *([cache_k] * n_pages_sel), *([cache_v] * n_pages_sel))


def kernel(x_prompt, x_sample, p_prompt, p_sample, cache_k, cache_v, state_conv, page_table, w_in, b_in, w_dw,
           b_dw, g_cn, b_cn, w_pw, b_pw, w_out, b_out, g_ln, b_ln, w_pe, w_pg, b_pg):
    depth = w_in.shape[0]
    assert depth == 1
    batch, seq, d_model = x_prompt.shape
    n_seq, dec_seq, _ = x_sample.shape
    assert dec_seq == 1
    n_phys, page, heads, head_dim = cache_k.shape[1:]
    assert head_dim == HEAD_DIM and page == PAGE_SIZE
    attn_w = heads * head_dim
    conv_w = w_pw.shape[1]
    assert conv_w == attn_w and w_in.shape[2] == 7 * attn_w
    n_pages = page_table.shape[1]
    hist = CONV_K - 1
    alpha = (2 * depth) ** 0.25
    n = batch * seq

    row2 = lambda a: a.reshape(1, -1)
    w_in_b = w_in[0].astype(BF16)
    w_pw_b = w_pw[0].astype(BF16)
    w_out_b = w_out[0].astype(BF16)
    w_pg_b = w_pg[0].astype(BF16)
    w_pe_b = w_pe[0].astype(BF16)
    b_in2, b_dw2, g_cn2, b_cn2, b_pw2 = row2(b_in[0]), row2(b_dw[0]), row2(g_cn[0]), row2(b_cn[0]), row2(b_pw[0])
    b_out2, g_ln2, b_ln2, b_pg2 = row2(b_out[0]), row2(g_ln[0]), row2(b_ln[0]), row2(b_pg[0])
    conv_w_args = (w_dw[0], b_dw2, g_cn2, b_cn2, w_pw_b, b_pw2)
    out_w_args = (w_out_b, b_out2, g_ln2, b_ln2, w_pg_b, b_pg2, w_pe_b, alpha)

    xp = x_prompt.reshape(n, d_model)
    q, k, v, sga, u, sgc = _in_proj(xp, w_in_b, b_in2, tm=512)
    attn = _moba_prompt(q, k, v, sga, batch, seq, heads)
    conv = _conv_prompt(u, sgc, *conv_w_args, seq=seq, tile=256)
    y_prompt = _out_proj(xp, attn, conv, p_prompt[0].reshape(n, -1), *out_w_args, tm=256)
    conv_prompt_new = u.reshape(batch, seq, conv_w)[:, seq - hist:, :]

    xs = x_sample.reshape(n_seq, d_model)
    qs, ks, vs, sgas, us, sgcs = _in_proj(xs, w_in_b, b_in2, tm=n_seq)
    conv_s, conv_sample_new = _conv_sample(state_conv[0], us, sgcs, *conv_w_args)
    pt_flat = page_table.reshape(-1)
    ck = cache_k[0].reshape(n_phys, page, attn_w)
    cv = cache_v[0].reshape(n_phys, page, attn_w)
    kmean = _kmean_sample(pt_flat, ck, n_seq, n_pages, pages_per_step=8)
    as3 = lambda a: a.reshape(n_seq, 1, attn_w)
    sel = _select_sample(as3(qs), kmean, heads)
    attn_s = _moba_sample(sel.reshape(-1), pt_flat, as3(qs), as3(ks), as3(vs), as3(sgas), ck, cv, n_pages, heads)
    y_sample = _out_proj(xs, attn_s.reshape(n_seq, attn_w), conv_s, p_sample[0].reshape(n_seq, -1),
                         *out_w_args, tm=n_seq)

    kv_shape = (depth, batch, seq, heads, head_dim)
    kvs_shape = (depth, n_seq, dec_seq, heads, head_dim)
    return (y_prompt.reshape(batch, seq, d_model), y_sample.reshape(n_seq, dec_seq, d_model),
            k.reshape(kv_shape), v.reshape(kv_shape), conv_prompt_new.reshape(depth, batch, hist, conv_w),
            ks.reshape(kvs_shape), vs.reshape(kvs_shape), conv_sample_new.reshape(depth, n_seq, hist, conv_w))
```

```python
import functools

import jax
import jax.numpy as jnp
from jax import lax
from jax.experimental import pallas as pl
from jax.experimental.pallas import tpu as pltpu

F32 = jnp.float32
BF16 = jnp.bfloat16

HEAD_DIM = 128
CONV_K = 31
MOBA_BLOCK = 256
MOBA_TOPK = 3
PAGE_SIZE = 128
PAGES_PER_BLOCK = MOBA_BLOCK // PAGE_SIZE
LN_EPS = 1e-5
NEG = -1e30
LOG2_E = 1.4426950408889634
LANES = 128
SUBLANES = 8
SUM_ROWS = 16
CONV_HALO = 32
VMEM_LIMIT = 56 * 1024 * 1024


def _silu(z):
    return z * jax.nn.sigmoid(z)


def _dot_nt(a, b, **kw):
    return lax.dot_general(a, b, (((1,), (1,)), ((), ())), preferred_element_type=F32, **kw)


def _in_proj_kernel(x_ref, w_ref, b_ref, q_ref, k_ref, v_ref, sga_ref, u_ref, sgc_ref, xb_ref, a_ref):
    j = pl.program_id(1)

    @pl.when(j == 0)
    def _():
        xb_ref[...] = x_ref[...].astype(BF16)

    z = jnp.dot(xb_ref[...], w_ref[...], preferred_element_type=F32) + b_ref[...]

    @pl.when(j == 0)
    def _():
        q_ref[...] = z

    @pl.when(j == 1)
    def _():
        k_ref[...] = z

    @pl.when(j == 2)
    def _():
        v_ref[...] = z

    @pl.when(j == 3)
    def _():
        sga_ref[...] = _silu(z)

    @pl.when(j == 4)
    def _():
        a_ref[...] = z

    @pl.when(j == 5)
    def _():
        u_ref[...] = a_ref[...] * jax.nn.sigmoid(z)

    @pl.when(j == 6)
    def _():
        sgc_ref[...] = _silu(z)


def _in_proj(x, w_b, b, tm):
    n, d = x.shape
    gw = w_b.shape[1] // 7
    row = pl.BlockSpec((tm, gw), lambda i, j: (i, 0))
    out = jax.ShapeDtypeStruct((n, gw), F32)
    return pl.pallas_call(
        _in_proj_kernel,
        grid=(n // tm, 7),
        in_specs=[
            pl.BlockSpec((tm, d), lambda i, j: (i, 0)),
            pl.BlockSpec((d, gw), lambda i, j: (0, j)),
            pl.BlockSpec((1, gw), lambda i, j: (0, j)),
        ],
        out_specs=[row] * 6,
        out_shape=[out] * 6,
        scratch_shapes=[pltpu.VMEM((tm, d), BF16), pltpu.VMEM((tm, gw), F32)],
        compiler_params=pltpu.CompilerParams(
            dimension_semantics=("arbitrary", "arbitrary"), vmem_limit_bytes=VMEM_LIMIT),
        name="in_proj",
    )(x, w_b, b)


def _topk_mask(g, idx, n_valid, axis):
    sel = jnp.zeros(g.shape, F32)
    big = g.shape[axis]
    for t in range(MOBA_TOPK):
        m = jnp.max(g, axis=axis, keepdims=True)
        first = jnp.min(jnp.where(g == m, idx, big), axis=axis, keepdims=True)
        hit = idx == first
        counts = jnp.where(t < n_valid, 1.0, 0.0)
        sel = jnp.maximum(sel, jnp.where(hit, counts, 0.0))
        g = jnp.where(hit, -jnp.inf, g)
    return sel


def _moba_prompt_kernel(q_ref, k_ref, v_ref, sga_ref, o_ref, kb_ref, vt_ref, kmean_ref, bias_ref, *, n_blocks, group):
    i = pl.program_id(2)
    blk = MOBA_BLOCK
    qscale = HEAD_DIM ** -0.5 * LOG2_E
    lanes = [slice(g * HEAD_DIM, (g + 1) * HEAD_DIM) for g in range(group)]

    @pl.when(i == 0)
    def _():
        for g in range(group):
            kmean_ref[g] = jnp.mean(k_ref[:, lanes[g]].reshape(n_blocks, blk, HEAD_DIM), axis=1)
            for jb in range(n_blocks):
                rows = slice(jb * blk, (jb + 1) * blk)
                kb_ref[g, jb] = k_ref[rows, lanes[g]].astype(BF16)
                vt_ref[g, jb, 0:HEAD_DIM, :] = v_ref[rows, lanes[g]].T.astype(BF16)
                vt_ref[g, jb, HEAD_DIM:, :] = jnp.ones((SUM_ROWS, blk), BF16)

    qs, init = [], []
    for g in range(group):
        qf = q_ref[:, lanes[g]]
        qs.append((qf * qscale).astype(BF16))

        gate = _dot_nt(kmean_ref[g], qf, precision=lax.Precision.HIGHEST)
        row = lax.broadcasted_iota(jnp.int32, gate.shape, 0)
        gate = jnp.where(row < i, gate, -jnp.inf)
        sel = _topk_mask(gate, row, i, axis=0)
        bias_ref[g] = jnp.where(sel > 0.5, 0.0, NEG)

        s = _dot_nt(kb_ref[g, i], qs[g])
        k_id = lax.broadcasted_iota(jnp.int32, s.shape, 0)
        q_id = lax.broadcasted_iota(jnp.int32, s.shape, 1)
        s = jnp.where(k_id <= q_id, s, NEG)
        m0 = jnp.max(s, axis=0, keepdims=True)
        p = jnp.exp2(s - m0)
        init.append((m0, jnp.dot(vt_ref[g, i], p.astype(BF16), preferred_element_type=F32)))

    def body(j, carry):
        out = []
        for g in range(group):
            m, acc = carry[g]
            sj = _dot_nt(kb_ref[g, j], qs[g])
            bj = bias_ref[g, pl.ds(j, 1), :]
            m_new = jnp.maximum(m, jnp.max(sj, axis=0, keepdims=True) + bj)
            alpha = jnp.exp2(m - m_new)
            pj = jnp.exp2(sj + (bj - m_new))
            acc_new = alpha * acc + jnp.dot(vt_ref[g, j], pj.astype(BF16), preferred_element_type=F32)
            out.append((m_new, acc_new))
        return tuple(out)

    final = lax.fori_loop(0, i, body, tuple(init))
    for g in range(group):
        acc = final[g][1]
        out_t = acc[0:HEAD_DIM, :] / acc[HEAD_DIM:HEAD_DIM + 1, :]
        o_ref[:, lanes[g]] = (out_t.T * sga_ref[:, lanes[g]]).astype(o_ref.dtype)


def _moba_prompt(q, k, v, sga, batch, seq, heads, group):
    n_blocks = seq // MOBA_BLOCK
    gw = group * HEAD_DIM
    qspec = pl.BlockSpec((MOBA_BLOCK, gw), lambda b, h, i: (b * n_blocks + i, h))
    kvspec = pl.BlockSpec((seq, gw), lambda b, h, i: (b, h))
    return pl.pallas_call(
        functools.partial(_moba_prompt_kernel, n_blocks=n_blocks, group=group),
        grid=(batch, heads // group, n_blocks),
        in_specs=[qspec, kvspec, kvspec, qspec],
        out_specs=qspec,
        out_shape=jax.ShapeDtypeStruct(q.shape, BF16),
        scratch_shapes=[pltpu.VMEM((group, n_blocks, MOBA_BLOCK, HEAD_DIM), BF16),
                        pltpu.VMEM((group, n_blocks, HEAD_DIM + SUM_ROWS, MOBA_BLOCK), BF16),
                        pltpu.VMEM((group, n_blocks, HEAD_DIM), F32),
                        pltpu.VMEM((group, n_blocks, MOBA_BLOCK), F32)],
        compiler_params=pltpu.CompilerParams(
            dimension_semantics=("arbitrary", "arbitrary", "arbitrary"), vmem_limit_bytes=VMEM_LIMIT),
        name="moba_prompt",
    )(q, k, v, sga)


def _conv_tail(c, sgc, gcn_ref, bcn_ref, wpw_ref, bpw_ref):
    mu = jnp.mean(c, axis=-1, keepdims=True)
    d = c - mu
    var = jnp.mean(d * d, axis=-1, keepdims=True)
    y = d * lax.rsqrt(var + LN_EPS) * gcn_ref[...] + bcn_ref[...]
    y = _silu(y)
    return (jnp.dot(y.astype(BF16), wpw_ref[...], preferred_element_type=F32) + bpw_ref[...]) * sgc


def _conv_prompt_kernel(u_ref, prev_ref, sgc_ref, wdw_ref, bdw_ref, gcn_ref, bcn_ref, wpw_ref, bpw_ref,
                        o_ref, ext_ref, c_ref, *, tiles_per_seq, rows):
    i = pl.program_id(0)
    t, w = u_ref.shape
    first = (i % tiles_per_seq) == 0
    ext_ref[0:CONV_HALO, :] = jnp.where(first, 0.0, prev_ref[...])
    ext_ref[CONV_HALO:, :] = u_ref[...]
    off = CONV_HALO - (CONV_K - 1)
    for c in range(w // LANES):
        cs = slice(c * LANES, (c + 1) * LANES)
        for r in range(t // rows):
            acc = jnp.broadcast_to(bdw_ref[:, cs], (rows, LANES))
            for j in range(CONV_K):
                lo = r * rows + off + j
                acc = acc + ext_ref[lo:lo + rows, cs] * wdw_ref[j:j + 1, cs]
            c_ref[r * rows:(r + 1) * rows, cs] = acc
    o_ref[...] = _conv_tail(c_ref[...], sgc_ref[...], gcn_ref, bcn_ref, wpw_ref, bpw_ref).astype(o_ref.dtype)


def _conv_prompt(u, sgc, w_dw, b_dw, g_cn, b_cn, w_pw_b, b_pw, seq, tile):
    n, w = u.shape
    halo_per_tile = tile // CONV_HALO
    full = lambda shape: pl.BlockSpec(shape, lambda i: (0, 0))
    rowspec = pl.BlockSpec((tile, w), lambda i: (i, 0))
    return pl.pallas_call(
        functools.partial(_conv_prompt_kernel, tiles_per_seq=seq // tile, rows=64),
        grid=(n // tile,),
        in_specs=[
            rowspec,
            pl.BlockSpec((CONV_HALO, w), lambda i: (jnp.maximum(i * halo_per_tile - 1, 0), 0)),
            rowspec,
            full((CONV_K, w)), full((1, w)), full((1, w)), full((1, w)), full((w, w)), full((1, w)),
        ],
        out_specs=rowspec,
        out_shape=jax.ShapeDtypeStruct((n, w), BF16),
        scratch_shapes=[pltpu.VMEM((tile + CONV_HALO, w), F32), pltpu.VMEM((tile, w), F32)],
        compiler_params=pltpu.CompilerParams(dimension_semantics=("arbitrary",), vmem_limit_bytes=VMEM_LIMIT),
        name="conv_prompt",
    )(u, u, sgc, w_dw, b_dw, g_cn, b_cn, w_pw_b, b_pw)


def _conv_sample_kernel(state_ref, u_ref, sgc_ref, wdw_ref, bdw_ref, gcn_ref, bcn_ref, wpw_ref, bpw_ref,
                        o_ref, new_ref, c_ref):
    nb = state_ref.shape[0]
    hist = CONV_K - 1
    for b in range(nb):
        st = state_ref[b]
        u_row = u_ref[b:b + 1, :]
        c_ref[b:b + 1, :] = (jnp.sum(st * wdw_ref[0:hist, :], axis=0, keepdims=True)
                             + u_row * wdw_ref[hist:hist + 1, :] + bdw_ref[...])
        new_ref[b, 0:hist - 1, :] = state_ref[b, 1:hist, :]
        new_ref[b, hist - 1:hist, :] = u_row
    o_ref[...] = _conv_tail(c_ref[...], sgc_ref[...], gcn_ref, bcn_ref, wpw_ref, bpw_ref).astype(o_ref.dtype)


def _conv_sample(state, u, sgc, w_dw, b_dw, g_cn, b_cn, w_pw_b, b_pw):
    nb, hist, w = state.shape
    full2 = lambda shape: pl.BlockSpec(shape, lambda i: (0, 0))
    full3 = lambda shape: pl.BlockSpec(shape, lambda i: (0, 0, 0))
    return pl.pallas_call(
        _conv_sample_kernel,
        grid=(1,),
        in_specs=[full3((nb, hist, w)), full2((nb, w)), full2((nb, w)),
                  full2((CONV_K, w)), full2((1, w)), full2((1, w)), full2((1, w)), full2((w, w)), full2((1, w))],
        out_specs=[full2((nb, w)), full3((nb, hist, w))],
        out_shape=[jax.ShapeDtypeStruct((nb, w), BF16), jax.ShapeDtypeStruct((nb, hist, w), F32)],
        scratch_shapes=[pltpu.VMEM((nb, w), F32)],
        compiler_params=pltpu.CompilerParams(dimension_semantics=("arbitrary",), vmem_limit_bytes=VMEM_LIMIT),
        name="conv_sample",
    )(state, u, sgc, w_dw, b_dw, g_cn, b_cn, w_pw_b, b_pw)


def _out_kernel(x_ref, a_ref, c_ref, p_ref, woa_ref, woc_ref, bo_ref, g_ref, b_ref, wpg_ref, bpg_ref, wpe_ref,
                o_ref, *, alpha):
    mix = (jnp.dot(a_ref[...].astype(BF16), woa_ref[...], preferred_element_type=F32)
           + jnp.dot(c_ref[...].astype(BF16), woc_ref[...], preferred_element_type=F32) + bo_ref[...])
    t = alpha * x_ref[...] + mix
    mu = jnp.mean(t, axis=-1, keepdims=True)
    d = t - mu
    var = jnp.mean(d * d, axis=-1, keepdims=True)
    h = d * lax.rsqrt(var + LN_EPS) * g_ref[...] + b_ref[...]
    gate = jax.nn.sigmoid(jnp.dot(h.astype(BF16), wpg_ref[...], preferred_element_type=F32) + bpg_ref[...])
    pe = jnp.dot(p_ref[...].astype(BF16), wpe_ref[...], preferred_element_type=F32)
    o_ref[...] = h + gate * pe


def _out_proj(x, attn, conv, p, w_out_b, b_out, g_ln, b_ln, w_pg_b, b_pg, w_pe_b, alpha, tm):
    n, d = x.shape
    half = attn.shape[1]
    pd = p.shape[1]
    const = lambda shape, r=0: pl.BlockSpec(shape, lambda i: (r, 0), pipeline_mode=pl.Buffered(1))
    row = lambda w: pl.BlockSpec((tm, w), lambda i: (i, 0))
    return pl.pallas_call(
        functools.partial(_out_kernel, alpha=alpha),
        grid=(n // tm,),
        in_specs=[row(d), row(half), row(half), row(pd),
                  const((half, d), 0), const((half, d), 1), const((1, d)), const((1, d)), const((1, d)),
                  const((d, d)), const((1, d)), const((pd, d))],
        out_specs=row(d),
        out_shape=jax.ShapeDtypeStruct((n, d), F32),
        compiler_params=pltpu.CompilerParams(dimension_semantics=("arbitrary",), vmem_limit_bytes=VMEM_LIMIT),
        name="out_proj",
    )(x, attn, conv, p, w_out_b, w_out_b, b_out, g_ln, b_ln, w_pg_b, b_pg, w_pe_b)


def _kmean_kernel(pt_ref, *refs, pages_per_step):
    del pt_ref
    page_refs, o_ref = refs[:pages_per_step], refs[pages_per_step]
    s = pl.program_id(1)
    blocks_per_step = pages_per_step // PAGES_PER_BLOCK
    for a in range(blocks_per_step):
        tot = jnp.sum(page_refs[PAGES_PER_BLOCK * a][0], axis=0)
        for r in range(1, PAGES_PER_BLOCK):
            tot = tot + jnp.sum(page_refs[PAGES_PER_BLOCK * a + r][0], axis=0)
        o_ref[0, s * blocks_per_step + a] = tot * (1.0 / MOBA_BLOCK)


def _kmean_sample(page_table_flat, cache, n_seq, n_pages, pages_per_step):
    n_phys, page, heads, hd = cache.shape
    n_blocks = n_pages // PAGES_PER_BLOCK
    steps = n_pages // pages_per_step

    def page_spec(r):
        return pl.BlockSpec(
            (1, page, heads, hd), lambda b, s, pt: (pt[b * n_pages + s * pages_per_step + r], 0, 0, 0))

    return pl.pallas_call(
        functools.partial(_kmean_kernel, pages_per_step=pages_per_step),
        grid_spec=pltpu.PrefetchScalarGridSpec(
            num_scalar_prefetch=1,
            grid=(n_seq, steps),
            in_specs=[page_spec(r) for r in range(pages_per_step)],
            out_specs=pl.BlockSpec((1, n_blocks, heads, hd), lambda b, s, pt: (b, 0, 0, 0)),
        ),
        out_shape=jax.ShapeDtypeStruct((n_seq, n_blocks, heads, hd), F32),
        compiler_params=pltpu.CompilerParams(
            dimension_semantics=("arbitrary", "arbitrary"), vmem_limit_bytes=VMEM_LIMIT),
        name="kmean_sample",
    )(page_table_flat, *([cache] * pages_per_step))


def _select_kernel(q_ref, kmean_ref, o_ref):
    g = jnp.sum(kmean_ref[0] * q_ref[...], axis=-1, keepdims=True)
    n_blocks = g.shape[0]
    idx = lax.broadcasted_iota(jnp.int32, g.shape, 0)
    for t in range(MOBA_TOPK):
        m = jnp.max(g, axis=0, keepdims=True)
        first = jnp.min(jnp.where(g == m, idx, n_blocks), axis=0, keepdims=True)
        o_ref[0, t] = jnp.broadcast_to(first[0], o_ref.shape[2:])
        g = jnp.where(idx == first, -jnp.inf, g)


def _select_sample(q4, kmean):
    n_seq, n_blocks, heads, hd = kmean.shape
    assert n_blocks >= MOBA_TOPK
    return pl.pallas_call(
        _select_kernel,
        grid=(n_seq,),
        in_specs=[pl.BlockSpec((1, heads, hd), lambda b: (b, 0, 0)),
                  pl.BlockSpec((1, n_blocks, heads, hd), lambda b: (b, 0, 0, 0))],
        out_specs=pl.BlockSpec((1, MOBA_TOPK, heads, LANES), lambda b: (b, 0, 0, 0)),
        out_shape=jax.ShapeDtypeStruct((n_seq, MOBA_TOPK, heads, LANES), jnp.int32),
        compiler_params=pltpu.CompilerParams(dimension_semantics=("arbitrary",), vmem_limit_bytes=VMEM_LIMIT),
        name="select_sample",
    )(q4, kmean)


def _moba_sample_kernel(sel_ref, pt_ref, q_ref, kn_ref, vn_ref, sga_ref, ck_ref, cv_ref, o_ref, kbuf, vbuf, sems,
                        *, n_seq, n_pages, heads):
    b = pl.program_id(0)
    n_sel = MOBA_TOPK * PAGES_PER_BLOCK
    scale = HEAD_DIM ** -0.5

    def slab_copies(seq, slot):
        copies = []
        for h in range(heads):
            for t in range(MOBA_TOPK):
                blk = sel_ref[(seq * MOBA_TOPK + t) * heads + h]
                for r in range(PAGES_PER_BLOCK):
                    page = pt_ref[seq * n_pages + blk * PAGES_PER_BLOCK + r]
                    dst = t * PAGES_PER_BLOCK + r
                    copies.append(pltpu.make_async_copy(
                        ck_ref.at[page, :, h, :], kbuf.at[slot, h, dst], sems.at[slot, 0, h]))
                    copies.append(pltpu.make_async_copy(
                        cv_ref.at[page, :, h, :], vbuf.at[slot, h, dst], sems.at[slot, 1, h]))
        return copies

    slot = b % 2

    @pl.when(b == 0)
    def _():
        for c in slab_copies(b, slot):
            c.start()

    @pl.when(b + 1 < n_seq)
    def _():
        for c in slab_copies(b + 1, 1 - slot):
            c.start()

    for c in slab_copies(b, slot):
        c.wait()

    for h in range(heads):
        hs = slice(h, h + 1)
        qh = q_ref[0, hs, :]
        kh = kbuf[slot, h].reshape(n_sel * PAGE_SIZE, HEAD_DIM).astype(BF16)
        vh = vbuf[slot, h].reshape(n_sel * PAGE_SIZE, HEAD_DIM).astype(BF16)
        s = _dot_nt(qh.astype(BF16), kh) * scale
        s_new = jnp.sum(qh * kn_ref[0, hs, :], axis=1, keepdims=True) * scale
        m = jnp.maximum(jnp.max(s, axis=1, keepdims=True), s_new)
        p = jnp.exp(s - m)
        p_new = jnp.exp(s_new - m)
        l = jnp.sum(p, axis=1, keepdims=True) + p_new
        acc = jnp.dot(p.astype(BF16), vh, preferred_element_type=F32) + p_new * vn_ref[0, hs, :]
        o_ref[0, hs, :] = acc / l * sga_ref[0, hs, :]


def _moba_sample(sel_flat, page_table_flat, q4, kn4, vn4, sga4, cache_k, cache_v, n_pages):
    n_seq, heads, hd = q4.shape
    n_sel = MOBA_TOPK * PAGES_PER_BLOCK
    vec = pl.BlockSpec((1, heads, hd), lambda b, sel, pt: (b, 0, 0))
    hbm = pl.BlockSpec(memory_space=pl.ANY)
    return pl.pallas_call(
        functools.partial(_moba_sample_kernel, n_seq=n_seq, n_pages=n_pages, heads=heads),
        grid_spec=pltpu.PrefetchScalarGridSpec(
            num_scalar_prefetch=2,
            grid=(n_seq,),
            in_specs=[vec, vec, vec, vec, hbm, hbm],
            out_specs=vec,
            scratch_shapes=[pltpu.VMEM((2, heads, n_sel, PAGE_SIZE, hd), F32),
                            pltpu.VMEM((2, heads, n_sel, PAGE_SIZE, hd), F32),
                            pltpu.SemaphoreType.DMA((2, 2, heads))],
        ),
        out_shape=jax.ShapeDtypeStruct(q4.shape, F32),
        compiler_params=pltpu.CompilerParams(dimension_semantics=("arbitrary",), vmem_limit_bytes=VMEM_LIMIT),
        name="moba_sample",
    )(sel_flat, page_table_flat, q4, kn4, vn4, sga4, cache_k, cache_v)


def kernel(x_prompt, x_sample, p_prompt, p_sample, cache_k, cache_v, state_conv, page_table, w_in, b_in, w_dw,
           b_dw, g_cn, b_cn, w_pw, b_pw, w_out, b_out, g_ln, b_ln, w_pe, w_pg, b_pg):
    depth = w_in.shape[0]
    assert depth == 1
    batch, seq, d_model = x_prompt.shape
    n_seq, dec_seq, _ = x_sample.shape
    assert dec_seq == 1
    n_phys, page, heads, head_dim = cache_k.shape[1:]
    assert head_dim == HEAD_DIM and page == PAGE_SIZE
    attn_w = heads * head_dim
    conv_w = w_pw.shape[1]
    assert conv_w == attn_w and w_in.shape[2] == 7 * attn_w
    n_pages = page_table.shape[1]
    hist = CONV_K - 1
    alpha = (2 * depth) ** 0.25
    n = batch * seq

    row2 = lambda a: a.reshape(1, -1)
    w_in_b = w_in[0].astype(BF16)
    w_pw_b = w_pw[0].astype(BF16)
    w_out_b = w_out[0].astype(BF16)
    w_pg_b = w_pg[0].astype(BF16)
    w_pe_b = w_pe[0].astype(BF16)
    b_in2, b_dw2, g_cn2, b_cn2, b_pw2 = row2(b_in[0]), row2(b_dw[0]), row2(g_cn[0]), row2(b_cn[0]), row2(b_pw[0])
    b_out2, g_ln2, b_ln2, b_pg2 = row2(b_out[0]), row2(g_ln[0]), row2(b_ln[0]), row2(b_pg[0])
    conv_w_args = (w_dw[0], b_dw2, g_cn2, b_cn2, w_pw_b, b_pw2)
    out_w_args = (w_out_b, b_out2, g_ln2, b_ln2, w_pg_b, b_pg2, w_pe_b, alpha)

    xp = x_prompt.reshape(n, d_model)
    q, k, v, sga, u, sgc = _in_proj(xp, w_in_b, b_in2, tm=512)
    attn = _moba_prompt(q, k, v, sga, batch, seq, heads, group=4)
    conv = _conv_prompt(u, sgc, *conv_w_args, seq=seq, tile=256)
    y_prompt = _out_proj(xp, attn, conv, p_prompt[0].reshape(n, -1), *out_w_args, tm=256)
    conv_prompt_new = u.reshape(batch, seq, conv_w)[:, seq - hist:, :]

    xs = x_sample.reshape(n_seq, d_model)
    qs, ks, vs, sgas, us, sgcs = _in_proj(xs, w_in_b, b_in2, tm=n_seq)
    conv_s, conv_sample_new = _conv_sample(state_conv[0], us, sgcs, *conv_w_args)
    pt_flat = page_table.reshape(-1)
    ck = cache_k.reshape(n_phys, page, heads, head_dim)
    cv = cache_v.reshape(n_phys, page, heads, head_dim)
    kmean = _kmean_sample(pt_flat, ck, n_seq, n_pages, pages_per_step=8)
    as4 = lambda a: a.reshape(n_seq, heads, head_dim)
    sel = _select_sample(as4(qs), kmean)[:, :, :, 0]
    attn_s = _moba_sample(sel.reshape(-1), pt_flat, as4(qs), as4(ks), as4(vs), as4(sgas), ck, cv, n_pages)
    y_sample = _out_proj(xs, attn_s.reshape(n_seq, attn_w), conv_s, p_sample[0].reshape(n_seq, -1),
                         *out_w_args, tm=n_seq)

    kv_shape = (depth, batch, seq, heads, head_dim)
    kvs_shape = (depth, n_seq, dec_seq, heads, head_dim)
    return (y_prompt.reshape(batch, seq, d_model), y_sample.reshape(n_seq, dec_seq, d_model),
            k.reshape(kv_shape), v.reshape(kv_shape), conv_prompt_new.reshape(depth, batch, hist, conv_w),
            ks.reshape(kvs_shape), vs.reshape(kvs_shape), conv_sample_new.reshape(depth, n_seq, hist, conv_w))
```

```python
import functools

import jax
import jax.numpy as jnp
from jax import lax
from jax.experimental import pallas as pl
from jax.experimental.pallas import tpu as pltpu

F32 = jnp.float32
BF16 = jnp.bfloat16

HEAD_DIM = 128
CONV_K = 31
MOBA_BLOCK = 256
MOBA_TOPK = 3
PAGE_SIZE = 128
PAGES_PER_BLOCK = MOBA_BLOCK // PAGE_SIZE
LN_EPS = 1e-5
NEG = -1e30
LOG2_E = 1.4426950408889634
LANES = 128
SUBLANES = 8
SUM_ROWS = 16
CONV_HALO = 32
VMEM_LIMIT = 56 * 1024 * 1024


def _sigmoid(z):
    return 0.5 * jnp.tanh(0.5 * z) + 0.5


def _silu(z):
    return z * _sigmoid(z)


def _dot_nt(a, b, **kw):
    return lax.dot_general(a, b, (((1,), (1,)), ((), ())), preferred_element_type=F32, **kw)


def _in_proj_kernel(x_ref, w_ref, b_ref, q_ref, k_ref, v_ref, sga_ref, u_ref, sgc_ref, xb_ref, a_ref):
    j = pl.program_id(1)

    @pl.when(j == 0)
    def _():
        xb_ref[...] = x_ref[...].astype(BF16)

    z = jnp.dot(xb_ref[...], w_ref[...], preferred_element_type=F32) + b_ref[...]

    @pl.when(j == 0)
    def _():
        q_ref[...] = z

    @pl.when(j == 1)
    def _():
        k_ref[...] = z

    @pl.when(j == 2)
    def _():
        v_ref[...] = z

    @pl.when(j == 3)
    def _():
        sga_ref[...] = _silu(z)

    @pl.when(j == 4)
    def _():
        a_ref[...] = z

    @pl.when(j == 5)
    def _():
        u_ref[...] = a_ref[...] * _sigmoid(z)

    @pl.when(j == 6)
    def _():
        sgc_ref[...] = _silu(z)


def _in_proj(x, w_b, b, tm):
    n, d = x.shape
    gw = w_b.shape[1] // 7
    row = pl.BlockSpec((tm, gw), lambda i, j: (i, 0))
    out = jax.ShapeDtypeStruct((n, gw), F32)
    return pl.pallas_call(
        _in_proj_kernel,
        grid=(n // tm, 7),
        in_specs=[
            pl.BlockSpec((tm, d), lambda i, j: (i, 0)),
            pl.BlockSpec((d, gw), lambda i, j: (0, j)),
            pl.BlockSpec((1, gw), lambda i, j: (0, j)),
        ],
        out_specs=[row] * 6,
        out_shape=[out] * 6,
        scratch_shapes=[pltpu.VMEM((tm, d), BF16), pltpu.VMEM((tm, gw), F32)],
        compiler_params=pltpu.CompilerParams(
            dimension_semantics=("arbitrary", "arbitrary"), vmem_limit_bytes=VMEM_LIMIT),
        name="in_proj",
    )(x, w_b, b)


def _topk_mask(g, idx, n_valid, axis):
    sel = jnp.zeros(g.shape, F32)
    big = g.shape[axis]
    for t in range(MOBA_TOPK):
        m = jnp.max(g, axis=axis, keepdims=True)
        first = jnp.min(jnp.where(g == m, idx, big), axis=axis, keepdims=True)
        hit = idx == first
        counts = jnp.where(t < n_valid, 1.0, 0.0)
        sel = jnp.maximum(sel, jnp.where(hit, counts, 0.0))
        g = jnp.where(hit, -jnp.inf, g)
    return sel


def _moba_prompt_kernel(qa_ref, qb_ref, k_ref, v_ref, sgaa_ref, sgab_ref, o_ref,
                        kb_ref, vt_ref, kmean_ref, qt_ref, bias_ref, s_ref, acc_ref, *, n_blocks):
    i = pl.program_id(2)
    blk = MOBA_BLOCK
    last = n_blocks - 1
    qscale = HEAD_DIM ** -0.5 * LOG2_E
    own = (i, last - i)
    q_refs, sga_refs = (qa_ref, qb_ref), (sgaa_ref, sgab_ref)

    @pl.when(i == 0)
    def _():
        kmean_ref[...] = jnp.mean(k_ref[...].reshape(n_blocks, blk, HEAD_DIM), axis=1)
        for jb in range(n_blocks):
            rows = slice(jb * blk, (jb + 1) * blk)
            kb_ref[jb] = k_ref[rows, :].astype(BF16)
            vt_ref[jb, 0:HEAD_DIM, :] = v_ref[rows, :].T.astype(BF16)
            vt_ref[jb, HEAD_DIM:, :] = jnp.ones((SUM_ROWS, blk), BF16)

    for w in range(2):
        qf_t = q_refs[w][...].T
        qt_ref[w] = (qf_t * qscale).astype(BF16)
        gate = jnp.dot(kmean_ref[...], qf_t, preferred_element_type=F32, precision=lax.Precision.HIGHEST)
        row = lax.broadcasted_iota(jnp.int32, gate.shape, 0)
        gate = jnp.where(row < own[w], gate, -jnp.inf)
        sel = _topk_mask(gate, row, own[w], axis=0)
        bias_ref[w] = jnp.where(sel > 0.5, 0.0, NEG)

    def past_slot(t):
        first = t < i
        return jnp.where(first, 1.0, 0.0), jnp.where(first, 0, 1), jnp.where(first, t, t - i)

    k_id = lax.broadcasted_iota(jnp.int32, (blk, blk), 0)
    q_id = lax.broadcasted_iota(jnp.int32, (blk, blk), 1)
    m = []
    for w in range(2):
        s = jnp.dot(kb_ref[own[w]], qt_ref[w], preferred_element_type=F32)
        s = jnp.where(k_id <= q_id, s, NEG)
        s_ref[w] = s
        m.append(jnp.max(s, axis=0, keepdims=True))
    for t in range(last):
        fa, w, kblk = past_slot(t)
        s = jnp.dot(kb_ref[kblk], qt_ref[w], preferred_element_type=F32)
        s_ref[2 + t] = s
        c = jnp.max(s, axis=0, keepdims=True) + bias_ref[w, pl.ds(kblk, 1), :]
        m[0] = jnp.maximum(m[0], c + (fa - 1.0) * -NEG)
        m[1] = jnp.maximum(m[1], c + fa * NEG)

    for w in range(2):
        p = jnp.exp2(s_ref[w] - m[w])
        acc_ref[w] = jnp.dot(vt_ref[own[w]], p.astype(BF16), preferred_element_type=F32)
    for t in range(last):
        fa, w, kblk = past_slot(t)
        shift = bias_ref[w, pl.ds(kblk, 1), :] - (m[1] + fa * (m[0] - m[1]))
        p = jnp.exp2(s_ref[2 + t] + shift)
        acc_ref[2 + t] = jnp.dot(vt_ref[kblk], p.astype(BF16), preferred_element_type=F32)

    bounds = ((0, i), (i, last))
    for w in range(2):
        acc = lax.fori_loop(bounds[w][0], bounds[w][1], lambda t, a: a + acc_ref[2 + t], acc_ref[w])
        out_t = acc[0:HEAD_DIM, :] / acc[HEAD_DIM:HEAD_DIM + 1, :]
        o_ref[0, 0, w] = (out_t.T * sga_refs[w][...]).astype(o_ref.dtype)


def _moba_prompt(q, k, v, sga, batch, seq, heads):
    n_blocks = seq // MOBA_BLOCK
    assert n_blocks % 2 == 0
    last = n_blocks - 1
    qa = pl.BlockSpec((MOBA_BLOCK, HEAD_DIM), lambda b, h, i: (b * n_blocks + i, h))
    qb = pl.BlockSpec((MOBA_BLOCK, HEAD_DIM), lambda b, h, i: (b * n_blocks + last - i, h))
    kvspec = pl.BlockSpec((seq, HEAD_DIM), lambda b, h, i: (b, h))
    return pl.pallas_call(
        functools.partial(_moba_prompt_kernel, n_blocks=n_blocks),
        grid=(batch, heads, n_blocks // 2),
        in_specs=[qa, qb, kvspec, kvspec, qa, qb],
        out_specs=pl.BlockSpec((1, 1, 2, MOBA_BLOCK, HEAD_DIM), lambda b, h, i: (b, i, 0, 0, h)),
        out_shape=jax.ShapeDtypeStruct((batch, n_blocks // 2, 2, MOBA_BLOCK, q.shape[1]), BF16),
        scratch_shapes=[pltpu.VMEM((n_blocks, MOBA_BLOCK, HEAD_DIM), BF16),
                        pltpu.VMEM((n_blocks, HEAD_DIM + SUM_ROWS, MOBA_BLOCK), BF16),
                        pltpu.VMEM((n_blocks, HEAD_DIM), F32),
                        pltpu.VMEM((2, HEAD_DIM, MOBA_BLOCK), BF16),
                        pltpu.VMEM((2, n_blocks, MOBA_BLOCK), F32),
                        pltpu.VMEM((n_blocks + 1, MOBA_BLOCK, MOBA_BLOCK), F32),
                        pltpu.VMEM((n_blocks + 1, HEAD_DIM + SUM_ROWS, MOBA_BLOCK), F32)],
        compiler_params=pltpu.CompilerParams(
            dimension_semantics=("arbitrary", "arbitrary", "arbitrary"), vmem_limit_bytes=VMEM_LIMIT),
        name="moba_prompt",
    )(q, q, k, v, sga, sga)


def _conv_tail(c, sgc, gcn_ref, bcn_ref, wpw_ref, bpw_ref):
    mu = jnp.mean(c, axis=-1, keepdims=True)
    d = c - mu
    var = jnp.mean(d * d, axis=-1, keepdims=True)
    y = d * lax.rsqrt(var + LN_EPS) * gcn_ref[...] + bcn_ref[...]
    y = _silu(y)
    return (jnp.dot(y.astype(BF16), wpw_ref[...], preferred_element_type=F32) + bpw_ref[...]) * sgc


def _conv_prompt_kernel(u_ref, prev_ref, sgc_ref, wdw_ref, bdw_ref, gcn_ref, bcn_ref, wpw_ref, bpw_ref,
                        o_ref, ext_ref, sh_ref, c_ref, *, tiles_per_seq, rows):
    i = pl.program_id(0)
    t, w = u_ref.shape
    first = (i % tiles_per_seq) == 0
    ext_ref[0:CONV_HALO, :] = jnp.where(first, 0.0, prev_ref[...])
    ext_ref[CONV_HALO:, :] = u_ref[...]
    n_sh = sh_ref.shape[1]
    for r in range(1, SUBLANES):
        sh_ref[r - 1] = ext_ref[r:r + n_sh, :]
    off = CONV_HALO - (CONV_K - 1)
    for c in range(w // LANES):
        cs = slice(c * LANES, (c + 1) * LANES)
        for rc in range(t // rows):
            acc = jnp.broadcast_to(bdw_ref[:, cs], (rows, LANES))
            for j in range(CONV_K):
                a, r = divmod(off + j, SUBLANES)
                lo = rc * rows + a * SUBLANES
                win = ext_ref[lo:lo + rows, cs] if r == 0 else sh_ref[r - 1, lo:lo + rows, cs]
                acc = acc + win * wdw_ref[j:j + 1, cs]
            c_ref[rc * rows:(rc + 1) * rows, cs] = acc
    o_ref[...] = _conv_tail(c_ref[...], sgc_ref[...], gcn_ref, bcn_ref, wpw_ref, bpw_ref).astype(o_ref.dtype)


def _conv_prompt(u, sgc, w_dw, b_dw, g_cn, b_cn, w_pw_b, b_pw, seq, tile):
    n, w = u.shape
    halo_per_tile = tile // CONV_HALO
    full = lambda shape: pl.BlockSpec(shape, lambda i: (0, 0))
    rowspec = pl.BlockSpec((tile, w), lambda i: (i, 0))
    return pl.pallas_call(
        functools.partial(_conv_prompt_kernel, tiles_per_seq=seq // tile, rows=64),
        grid=(n // tile,),
        in_specs=[
            rowspec,
            pl.BlockSpec((CONV_HALO, w), lambda i: (jnp.maximum(i * halo_per_tile - 1, 0), 0)),
            rowspec,
            full((CONV_K, w)), full((1, w)), full((1, w)), full((1, w)), full((w, w)), full((1, w)),
        ],
        out_specs=rowspec,
        out_shape=jax.ShapeDtypeStruct((n, w), BF16),
        scratch_shapes=[pltpu.VMEM((tile + CONV_HALO, w), F32),
                        pltpu.VMEM((SUBLANES - 1, tile + CONV_HALO - SUBLANES, w), F32),
                        pltpu.VMEM((tile, w), F32)],
        compiler_params=pltpu.CompilerParams(dimension_semantics=("arbitrary",), vmem_limit_bytes=VMEM_LIMIT),
        name="conv_prompt",
    )(u, u, sgc, w_dw, b_dw, g_cn, b_cn, w_pw_b, b_pw)


def _conv_sample_kernel(state_ref, u_ref, sgc_ref, wdw_ref, bdw_ref, gcn_ref, bcn_ref, wpw_ref, bpw_ref,
                        o_ref, new_ref, c_ref):
    nb = state_ref.shape[0]
    hist = CONV_K - 1
    for b in range(nb):
        st = state_ref[b]
        u_row = u_ref[b:b + 1, :]
        c_ref[b:b + 1, :] = (jnp.sum(st * wdw_ref[0:hist, :], axis=0, keepdims=True)
                             + u_row * wdw_ref[hist:hist + 1, :] + bdw_ref[...])
        new_ref[b, 0:hist - 1, :] = state_ref[b, 1:hist, :]
        new_ref[b, hist - 1:hist, :] = u_row
    o_ref[...] = _conv_tail(c_ref[...], sgc_ref[...], gcn_ref, bcn_ref, wpw_ref, bpw_ref).astype(o_ref.dtype)


def _conv_sample(state, u, sgc, w_dw, b_dw, g_cn, b_cn, w_pw_b, b_pw):
    nb, hist, w = state.shape
    full2 = lambda shape: pl.BlockSpec(shape, lambda i: (0, 0))
    full3 = lambda shape: pl.BlockSpec(shape, lambda i: (0, 0, 0))
    return pl.pallas_call(
        _conv_sample_kernel,
        grid=(1,),
        in_specs=[full3((nb, hist, w)), full2((nb, w)), full2((nb, w)),
                  full2((CONV_K, w)), full2((1, w)), full2((1, w)), full2((1, w)), full2((w, w)), full2((1, w))],
        out_specs=[full2((nb, w)), full3((nb, hist, w))],
        out_shape=[jax.ShapeDtypeStruct((nb, w), BF16), jax.ShapeDtypeStruct((nb, hist, w), F32)],
        scratch_shapes=[pltpu.VMEM((nb, w), F32)],
        compiler_params=pltpu.CompilerParams(dimension_semantics=("arbitrary",), vmem_limit_bytes=VMEM_LIMIT),
        name="conv_sample",
    )(state, u, sgc, w_dw, b_dw, g_cn, b_cn, w_pw_b, b_pw)


def _out_kernel(x_ref, a_ref, c_ref, p_ref, woa_ref, woc_ref, bo_ref, g_ref, b_ref, wpg_ref, bpg_ref, wpe_ref,
                o_ref, *, alpha):
    attn = a_ref[...].reshape(c_ref.shape).astype(BF16)
    mix = (jnp.dot(attn, woa_ref[...], preferred_element_type=F32)
           + jnp.dot(c_ref[...].astype(BF16), woc_ref[...], preferred_element_type=F32) + bo_ref[...])
    t = alpha * x_ref[...] + mix
    mu = jnp.mean(t, axis=-1, keepdims=True)
    d = t - mu
    var = jnp.mean(d * d, axis=-1, keepdims=True)
    h = d * lax.rsqrt(var + LN_EPS) * g_ref[...] + b_ref[...]
    gate = _sigmoid(jnp.dot(h.astype(BF16), wpg_ref[...], preferred_element_type=F32) + bpg_ref[...])
    pe = jnp.dot(p_ref[...].astype(BF16), wpe_ref[...], preferred_element_type=F32)
    o_ref[...] = h + gate * pe


def _out_proj(x, attn, attn_spec, conv, p, w_out_b, b_out, g_ln, b_ln, w_pg_b, b_pg, w_pe_b, alpha, tm):
    n, d = x.shape
    half = conv.shape[1]
    pd = p.shape[1]
    const = lambda shape, r=0: pl.BlockSpec(shape, lambda i: (r, 0), pipeline_mode=pl.Buffered(1))
    row = lambda w: pl.BlockSpec((tm, w), lambda i: (i, 0))
    return pl.pallas_call(
        functools.partial(_out_kernel, alpha=alpha),
        grid=(n // tm,),
        in_specs=[row(d), attn_spec, row(half), row(pd),
                  const((half, d), 0), const((half, d), 1), const((1, d)), const((1, d)), const((1, d)),
                  const((d, d)), const((1, d)), const((pd, d))],
        out_specs=row(d),
        out_shape=jax.ShapeDtypeStruct((n, d), F32),
        compiler_params=pltpu.CompilerParams(dimension_semantics=("arbitrary",), vmem_limit_bytes=VMEM_LIMIT),
        name="out_proj",
    )(x, attn, conv, p, w_out_b, w_out_b, b_out, g_ln, b_ln, w_pg_b, b_pg, w_pe_b)


def _kmean_kernel(pt_ref, *refs, pages_per_step):
    del pt_ref
    page_refs, o_ref = refs[:pages_per_step], refs[pages_per_step]
    s = pl.program_id(1)
    blocks_per_step = pages_per_step // PAGES_PER_BLOCK
    for a in range(blocks_per_step):
        tot = jnp.sum(page_refs[PAGES_PER_BLOCK * a][0], axis=0)
        for r in range(1, PAGES_PER_BLOCK):
            tot = tot + jnp.sum(page_refs[PAGES_PER_BLOCK * a + r][0], axis=0)
        o_ref[0, s * blocks_per_step + a] = tot * (1.0 / MOBA_BLOCK)


def _kmean_sample(page_table_flat, cache, n_seq, n_pages, pages_per_step):
    n_phys, page, heads, hd = cache.shape
    n_blocks = n_pages // PAGES_PER_BLOCK
    steps = n_pages // pages_per_step

    def page_spec(r):
        return pl.BlockSpec(
            (1, page, heads, hd), lambda b, s, pt: (pt[b * n_pages + s * pages_per_step + r], 0, 0, 0))

    return pl.pallas_call(
        functools.partial(_kmean_kernel, pages_per_step=pages_per_step),
        grid_spec=pltpu.PrefetchScalarGridSpec(
            num_scalar_prefetch=1,
            grid=(n_seq, steps),
            in_specs=[page_spec(r) for r in range(pages_per_step)],
            out_specs=pl.BlockSpec((1, n_blocks, heads, hd), lambda b, s, pt: (b, 0, 0, 0)),
        ),
        out_shape=jax.ShapeDtypeStruct((n_seq, n_blocks, heads, hd), F32),
        compiler_params=pltpu.CompilerParams(
            dimension_semantics=("arbitrary", "arbitrary"), vmem_limit_bytes=VMEM_LIMIT),
        name="kmean_sample",
    )(page_table_flat, *([cache] * pages_per_step))


def _select_kernel(q_ref, kmean_ref, o_ref):
    g = jnp.sum(kmean_ref[0] * q_ref[...], axis=-1, keepdims=True)
    n_blocks = g.shape[0]
    idx = lax.broadcasted_iota(jnp.int32, g.shape, 0)
    for t in range(MOBA_TOPK):
        m = jnp.max(g, axis=0, keepdims=True)
        first = jnp.min(jnp.where(g == m, idx, n_blocks), axis=0, keepdims=True)
        o_ref[0, t] = jnp.broadcast_to(first[0], o_ref.shape[2:])
        g = jnp.where(idx == first, -jnp.inf, g)


def _select_sample(q4, kmean):
    n_seq, n_blocks, heads, hd = kmean.shape
    assert n_blocks >= MOBA_TOPK
    return pl.pallas_call(
        _select_kernel,
        grid=(n_seq,),
        in_specs=[pl.BlockSpec((1, heads, hd), lambda b: (b, 0, 0)),
                  pl.BlockSpec((1, n_blocks, heads, hd), lambda b: (b, 0, 0, 0))],
        out_specs=pl.BlockSpec((1, MOBA_TOPK, heads, LANES), lambda b: (b, 0, 0, 0)),
        out_shape=jax.ShapeDtypeStruct((n_seq, MOBA_TOPK, heads, LANES), jnp.int32),
        compiler_params=pltpu.CompilerParams(dimension_semantics=("arbitrary",), vmem_limit_bytes=VMEM_LIMIT),
        name="select_sample",
    )(q4, kmean)


def _moba_sample_kernel(sel_ref, pt_ref, q_ref, kn_ref, vn_ref, sga_ref, ck_ref, cv_ref, o_ref, kbuf, vbuf, sems,
                        *, n_seq, n_pages, heads):
    b = pl.program_id(0)
    n_sel = MOBA_TOPK * PAGES_PER_BLOCK
    scale = HEAD_DIM ** -0.5

    def slab_copies(seq, slot):
        copies = []
        for h in range(heads):
            for t in range(MOBA_TOPK):
                blk = sel_ref[(seq * MOBA_TOPK + t) * heads + h]
                for r in range(PAGES_PER_BLOCK):
                    page = pt_ref[seq * n_pages + blk * PAGES_PER_BLOCK + r]
                    dst = t * PAGES_PER_BLOCK + r
                    copies.append(pltpu.make_async_copy(
                        ck_ref.at[page, :, h, :], kbuf.at[slot, h, dst], sems.at[slot, 0, h]))
                    copies.append(pltpu.make_async_copy(
                        cv_ref.at[page, :, h, :], vbuf.at[slot, h, dst], sems.at[slot, 1, h]))
        return copies

    slot = b % 2

    @pl.when(b == 0)
    def _():
        for c in slab_copies(b, slot):
            c.start()

    @pl.when(b + 1 < n_seq)
    def _():
        for c in slab_copies(b + 1, 1 - slot):
            c.start()

    for c in slab_copies(b, slot):
        c.wait()

    for h in range(heads):
        hs = slice(h, h + 1)
        qh = q_ref[0, hs, :]
        kh = kbuf[slot, h].reshape(n_sel * PAGE_SIZE, HEAD_DIM).astype(BF16)
        vh = vbuf[slot, h].reshape(n_sel * PAGE_SIZE, HEAD_DIM).astype(BF16)
        s = _dot_nt(qh.astype(BF16), kh) * scale
        s_new = jnp.sum(qh * kn_ref[0, hs, :], axis=1, keepdims=True) * scale
        m = jnp.maximum(jnp.max(s, axis=1, keepdims=True), s_new)
        p = jnp.exp(s - m)
        p_new = jnp.exp(s_new - m)
        l = jnp.sum(p, axis=1, keepdims=True) + p_new
        acc = jnp.dot(p.astype(BF16), vh, preferred_element_type=F32) + p_new * vn_ref[0, hs, :]
        o_ref[0, hs, :] = acc / l * sga_ref[0, hs, :]


def _moba_sample(sel_flat, page_table_flat, q4, kn4, vn4, sga4, cache_k, cache_v, n_pages):
    n_seq, heads, hd = q4.shape
    n_sel = MOBA_TOPK * PAGES_PER_BLOCK
    vec = pl.BlockSpec((1, heads, hd), lambda b, sel, pt: (b, 0, 0))
    hbm = pl.BlockSpec(memory_space=pl.ANY)
    return pl.pallas_call(
        functools.partial(_moba_sample_kernel, n_seq=n_seq, n_pages=n_pages, heads=heads),
        grid_spec=pltpu.PrefetchScalarGridSpec(
            num_scalar_prefetch=2,
            grid=(n_seq,),
            in_specs=[vec, vec, vec, vec, hbm, hbm],
            out_specs=vec,
            scratch_shapes=[pltpu.VMEM((2, heads, n_sel, PAGE_SIZE, hd), F32),
                            pltpu.VMEM((2, heads, n_sel, PAGE_SIZE, hd), F32),
                            pltpu.SemaphoreType.DMA((2, 2, heads))],
        ),
        out_shape=jax.ShapeDtypeStruct(q4.shape, F32),
        compiler_params=pltpu.CompilerParams(dimension_semantics=("arbitrary",), vmem_limit_bytes=VMEM_LIMIT),
        name="moba_sample",
    )(sel_flat, page_table_flat, q4, kn4, vn4, sga4, cache_k, cache_v)


def kernel(x_prompt, x_sample, p_prompt, p_sample, cache_k, cache_v, state_conv, page_table, w_in, b_in, w_dw,
           b_dw, g_cn, b_cn, w_pw, b_pw, w_out, b_out, g_ln, b_ln, w_pe, w_pg, b_pg):
    depth = w_in.shape[0]
    assert depth == 1
    batch, seq, d_model = x_prompt.shape
    n_seq, dec_seq, _ = x_sample.shape
    assert dec_seq == 1
    n_phys, page, heads, head_dim = cache_k.shape[1:]
    assert head_dim == HEAD_DIM and page == PAGE_SIZE
    attn_w = heads * head_dim
    conv_w = w_pw.shape[1]
    assert conv_w == attn_w and w_in.shape[2] == 7 * attn_w
    n_pages = page_table.shape[1]
    hist = CONV_K - 1
    alpha = (2 * depth) ** 0.25
    n = batch * seq

    row2 = lambda a: a.reshape(1, -1)
    w_in_b = w_in[0].astype(BF16)
    w_pw_b = w_pw[0].astype(BF16)
    w_out_b = w_out[0].astype(BF16)
    w_pg_b = w_pg[0].astype(BF16)
    w_pe_b = w_pe[0].astype(BF16)
    b_in2, b_dw2, g_cn2, b_cn2, b_pw2 = row2(b_in[0]), row2(b_dw[0]), row2(g_cn[0]), row2(b_cn[0]), row2(b_pw[0])
    b_out2, g_ln2, b_ln2, b_pg2 = row2(b_out[0]), row2(g_ln[0]), row2(b_ln[0]), row2(b_pg[0])
    conv_w_args = (w_dw[0], b_dw2, g_cn2, b_cn2, w_pw_b, b_pw2)
    out_w_args = (w_out_b, b_out2, g_ln2, b_ln2, w_pg_b, b_pg2, w_pe_b, alpha)

    xp = x_prompt.reshape(n, d_model)
    q, k, v, sga, u, sgc = _in_proj(xp, w_in_b, b_in2, tm=512)
    attn = _moba_prompt(q, k, v, sga, batch, seq, heads)
    conv = _conv_prompt(u, sgc, *conv_w_args, seq=seq, tile=MOBA_BLOCK)
    n_blocks = seq // MOBA_BLOCK

    def paired_rows(r):
        blk = r % n_blocks
        mirror = n_blocks - 1 - blk
        return (r // n_blocks, jnp.minimum(blk, mirror), jnp.where(blk > mirror, 1, 0), 0, 0)

    y_prompt = _out_proj(xp, attn, pl.BlockSpec((1, 1, 1, MOBA_BLOCK, attn_w), paired_rows), conv,
                         p_prompt[0].reshape(n, -1), *out_w_args, tm=MOBA_BLOCK)
    conv_prompt_new = u.reshape(batch, seq, conv_w)[:, seq - hist:, :]

    xs = x_sample.reshape(n_seq, d_model)
    qs, ks, vs, sgas, us, sgcs = _in_proj(xs, w_in_b, b_in2, tm=n_seq)
    conv_s, conv_sample_new = _conv_sample(state_conv[0], us, sgcs, *conv_w_args)
    pt_flat = page_table.reshape(-1)
    ck = cache_k.reshape(n_phys, page, heads, head_dim)
    cv = cache_v.reshape(n_phys, page, heads, head_dim)
    kmean = _kmean_sample(pt_flat, ck, n_seq, n_pages, pages_per_step=16)
    as4 = lambda a: a.reshape(n_seq, heads, head_dim)
    sel = _select_sample(as4(qs), kmean)[:, :, :, 0]
    attn_s = _moba_sample(sel.reshape(-1), pt_flat, as4(qs), as4(ks), as4(vs), as4(sgas), ck, cv, n_pages)
    y_sample = _out_proj(xs, attn_s.reshape(n_seq, attn_w), pl.BlockSpec((n_seq, attn_w), lambda r: (r, 0)), conv_s,
                         p_sample[0].reshape(n_seq, -1), *out_w_args, tm=n_seq)

    kv_shape = (depth, batch, seq, heads, head_dim)
    kvs_shape = (depth, n_seq, dec_seq, heads, head_dim)
    return (y_prompt.reshape(batch, seq, d_model), y_sample.reshape(n_seq, dec_seq, d_model),
            k.reshape(kv_shape), v.reshape(kv_shape), conv_prompt_new.reshape(depth, batch, hist, conv_w),
            ks.reshape(kvs_shape), vs.reshape(kvs_shape), conv_sample_new.reshape(depth, n_seq, hist, conv_w))
```

```python
import functools

import jax
import jax.numpy as jnp
from jax import lax
from jax.experimental import pallas as pl
from jax.experimental.pallas import tpu as pltpu

F32 = jnp.float32
BF16 = jnp.bfloat16

HEAD_DIM = 128
CONV_K = 31
MOBA_BLOCK = 256
MOBA_TOPK = 3
PAGE_SIZE = 128
PAGES_PER_BLOCK = MOBA_BLOCK // PAGE_SIZE
LN_EPS = 1e-5
NEG = -1e30
LOG2_E = 1.4426950408889634
LANES = 128
SUBLANES = 8
SUM_ROWS = 16
CONV_HALO = 32
VMEM_LIMIT = 56 * 1024 * 1024


def _sigmoid(z):
    return 0.5 * jnp.tanh(0.5 * z) + 0.5


def _silu(z):
    return z * _sigmoid(z)


def _dot_nt(a, b, **kw):
    return lax.dot_general(a, b, (((1,), (1,)), ((), ())), preferred_element_type=F32, **kw)


def _in_proj_kernel(x_ref, w_ref, b_ref, q_ref, k_ref, v_ref, sga_ref, u_ref, sgc_ref, xb_ref, a_ref):
    j = pl.program_id(1)

    @pl.when(j == 0)
    def _():
        xb_ref[...] = x_ref[...].astype(BF16)

    def column_group(g, finish):
        @pl.when(j == g)
        def _():
            finish(jnp.dot(xb_ref[...], w_ref[...], preferred_element_type=F32) + b_ref[...])

    def store(ref, fn=lambda z: z):
        def finish(z):
            ref[...] = fn(z)
        return finish

    def glu(z):
        u_ref[...] = a_ref[...] * _sigmoid(z)

    for g, finish in enumerate((store(q_ref), store(k_ref), store(v_ref), store(sga_ref, _silu), store(a_ref), glu,
                                store(sgc_ref, _silu))):
        column_group(g, finish)


def _in_proj(x, w_b, b, tm):
    n, d = x.shape
    gw = w_b.shape[1] // 7
    row = pl.BlockSpec((tm, gw), lambda i, j: (i, 0))
    out = jax.ShapeDtypeStruct((n, gw), F32)
    return pl.pallas_call(
        _in_proj_kernel,
        grid=(n // tm, 7),
        in_specs=[
            pl.BlockSpec((tm, d), lambda i, j: (i, 0)),
            pl.BlockSpec((d, gw), lambda i, j: (0, j)),
            pl.BlockSpec((1, gw), lambda i, j: (0, j)),
        ],
        out_specs=[row] * 6,
        out_shape=[out] * 6,
        scratch_shapes=[pltpu.VMEM((tm, d), BF16), pltpu.VMEM((tm, gw), F32)],
        compiler_params=pltpu.CompilerParams(
            dimension_semantics=("arbitrary", "arbitrary"), vmem_limit_bytes=VMEM_LIMIT),
        name="in_proj",
    )(x, w_b, b)


def _topk_mask(g, idx, n_valid, axis):
    sel = jnp.zeros(g.shape, F32)
    big = g.shape[axis]
    for t in range(MOBA_TOPK):
        m = jnp.max(g, axis=axis, keepdims=True)
        first = jnp.min(jnp.where(g == m, idx, big), axis=axis, keepdims=True)
        hit = idx == first
        counts = jnp.where(t < n_valid, 1.0, 0.0)
        sel = jnp.maximum(sel, jnp.where(hit, counts, 0.0))
        g = jnp.where(hit, -jnp.inf, g)
    return sel


def _moba_prompt_kernel(pt_ref, qa_ref, qb_ref, k_ref, v_ref, sgaa_ref, sgab_ref, *refs, n_blocks, pages_per_step):
    del pt_ref
    page_refs = refs[:pages_per_step]
    o_ref, pmean_ref, kb_ref, vt_ref, kmean_ref, qt_ref, bias_ref, s_ref, acc_ref = refs[pages_per_step:]
    i = pl.program_id(2)
    blk = MOBA_BLOCK
    last = n_blocks - 1
    qscale = HEAD_DIM ** -0.5 * LOG2_E
    own = (i, last - i)
    q_refs, sga_refs = (qa_ref, qb_ref), (sgaa_ref, sgab_ref)

    @pl.when(i == 0)
    def _():
        kmean_ref[...] = jnp.mean(k_ref[...].reshape(n_blocks, blk, HEAD_DIM), axis=1)
        for jb in range(n_blocks):
            rows = slice(jb * blk, (jb + 1) * blk)
            kb_ref[jb] = k_ref[rows, :].astype(BF16)
            vt_ref[jb, 0:HEAD_DIM, :] = v_ref[rows, :].T.astype(BF16)
            vt_ref[jb, HEAD_DIM:, :] = jnp.ones((SUM_ROWS, blk), BF16)

    def cache_block_mean(a):
        tot = jnp.sum(page_refs[PAGES_PER_BLOCK * a][0], axis=0)
        for r in range(1, PAGES_PER_BLOCK):
            tot = tot + jnp.sum(page_refs[PAGES_PER_BLOCK * a + r][0], axis=0)
        pmean_ref[0, a] = tot * (1.0 / MOBA_BLOCK)

    for w in range(2):
        qf_t = q_refs[w][...].T
        qt_ref[w] = (qf_t * qscale).astype(BF16)
        gate = jnp.dot(kmean_ref[...], qf_t, preferred_element_type=F32, precision=lax.Precision.HIGHEST)
        row = lax.broadcasted_iota(jnp.int32, gate.shape, 0)
        gate = jnp.where(row < own[w], gate, -jnp.inf)
        sel = _topk_mask(gate, row, own[w], axis=0)
        bias_ref[w] = jnp.where(sel > 0.5, 0.0, NEG)

    def past_slot(t):
        first = t < i
        return jnp.where(first, 1.0, 0.0), jnp.where(first, 0, 1), jnp.where(first, t, t - i)

    k_id = lax.broadcasted_iota(jnp.int32, (blk, blk), 0)
    q_id = lax.broadcasted_iota(jnp.int32, (blk, blk), 1)
    m = []
    for w in range(2):
        s = jnp.dot(kb_ref[own[w]], qt_ref[w], preferred_element_type=F32)
        s = jnp.where(k_id <= q_id, s, NEG)
        s_ref[w] = s
        m.append(jnp.max(s, axis=0, keepdims=True))
    for t in range(last):
        fa, w, kblk = past_slot(t)
        s = jnp.dot(kb_ref[kblk], qt_ref[w], preferred_element_type=F32)
        s_ref[2 + t] = s
        c = jnp.max(s, axis=0, keepdims=True) + bias_ref[w, pl.ds(kblk, 1), :]
        m[0] = jnp.maximum(m[0], c + (fa - 1.0) * -NEG)
        m[1] = jnp.maximum(m[1], c + fa * NEG)
        if t < pages_per_step // PAGES_PER_BLOCK:
            cache_block_mean(t)

    for w in range(2):
        p = jnp.exp2(s_ref[w] - m[w])
        acc_ref[w] = jnp.dot(vt_ref[own[w]], p.astype(BF16), preferred_element_type=F32)
    for t in range(last):
        fa, w, kblk = past_slot(t)
        shift = bias_ref[w, pl.ds(kblk, 1), :] - (m[1] + fa * (m[0] - m[1]))
        p = jnp.exp2(s_ref[2 + t] + shift)
        acc_ref[2 + t] = jnp.dot(vt_ref[kblk], p.astype(BF16), preferred_element_type=F32)

    bounds = ((0, i), (i, last))
    for w in range(2):
        acc = lax.fori_loop(bounds[w][0], bounds[w][1], lambda t, a: a + acc_ref[2 + t], acc_ref[w])
        out_t = acc[0:HEAD_DIM, :] / acc[HEAD_DIM:HEAD_DIM + 1, :]
        o_ref[0, 0, w] = (out_t.T * sga_refs[w][...]).astype(o_ref.dtype)


def _moba_prompt(q, k, v, sga, batch, seq, heads, page_table_flat, cache, n_seq, n_pages):
    n_blocks = seq // MOBA_BLOCK
    assert n_blocks % 2 == 0
    last = n_blocks - 1
    pairs = n_blocks // 2
    n_phys, page, c_heads, hd = cache.shape
    steps = batch * heads * pairs
    pages_per_step = n_seq * n_pages // steps
    parts = n_pages // pages_per_step
    assert pages_per_step * steps == n_seq * n_pages and parts * pages_per_step == n_pages
    blocks_per_step = pages_per_step // PAGES_PER_BLOCK
    assert blocks_per_step * PAGES_PER_BLOCK == pages_per_step and blocks_per_step <= last

    def step_id(b, h, i):
        return (b * heads + h) * pairs + i

    def page_spec(r):
        return pl.BlockSpec((1, page, c_heads, hd),
                            lambda b, h, i, pt: (pt[step_id(b, h, i) * pages_per_step + r], 0, 0, 0))

    qa = pl.BlockSpec((MOBA_BLOCK, HEAD_DIM), lambda b, h, i, pt: (b * n_blocks + i, h))
    qb = pl.BlockSpec((MOBA_BLOCK, HEAD_DIM), lambda b, h, i, pt: (b * n_blocks + last - i, h))
    kvspec = pl.BlockSpec((seq, HEAD_DIM), lambda b, h, i, pt: (b, h))
    return pl.pallas_call(
        functools.partial(_moba_prompt_kernel, n_blocks=n_blocks, pages_per_step=pages_per_step),
        grid_spec=pltpu.PrefetchScalarGridSpec(
            num_scalar_prefetch=1,
            grid=(batch, heads, pairs),
            in_specs=[qa, qb, kvspec, kvspec, qa, qb] + [page_spec(r) for r in range(pages_per_step)],
            out_specs=[
                pl.BlockSpec((1, 1, 2, MOBA_BLOCK, HEAD_DIM), lambda b, h, i, pt: (b, i, 0, 0, h)),
                pl.BlockSpec((1, blocks_per_step, c_heads, hd),
                             lambda b, h, i, pt: (step_id(b, h, i) // parts, step_id(b, h, i) % parts, 0, 0)),
            ],
            scratch_shapes=[pltpu.VMEM((n_blocks, MOBA_BLOCK, HEAD_DIM), BF16),
                            pltpu.VMEM((n_blocks, HEAD_DIM + SUM_ROWS, MOBA_BLOCK), BF16),
                            pltpu.VMEM((n_blocks, HEAD_DIM), F32),
                            pltpu.VMEM((2, HEAD_DIM, MOBA_BLOCK), BF16),
                            pltpu.VMEM((2, n_blocks, MOBA_BLOCK), F32),
                            pltpu.VMEM((n_blocks + 1, MOBA_BLOCK, MOBA_BLOCK), F32),
                            pltpu.VMEM((n_blocks + 1, HEAD_DIM + SUM_ROWS, MOBA_BLOCK), F32)],
        ),
        out_shape=[jax.ShapeDtypeStruct((batch, pairs, 2, MOBA_BLOCK, q.shape[1]), BF16),
                   jax.ShapeDtypeStruct((n_seq, n_pages // PAGES_PER_BLOCK, c_heads, hd), F32)],
        compiler_params=pltpu.CompilerParams(
            dimension_semantics=("arbitrary", "arbitrary", "arbitrary"), vmem_limit_bytes=VMEM_LIMIT),
        name="moba_prompt",
    )(page_table_flat, q, q, k, v, sga, sga, *([cache] * pages_per_step))


def _conv_tail(c, sgc, gcn_ref, bcn_ref, wpw_ref, bpw_ref):
    mu = jnp.mean(c, axis=-1, keepdims=True)
    d = c - mu
    var = jnp.mean(d * d, axis=-1, keepdims=True)
    y = d * lax.rsqrt(var + LN_EPS) * gcn_ref[...] + bcn_ref[...]
    y = _silu(y)
    return (jnp.dot(y.astype(BF16), wpw_ref[...], preferred_element_type=F32) + bpw_ref[...]) * sgc


def _conv_prompt_kernel(u_ref, prev_ref, sgc_ref, wdw_ref, bdw_ref, gcn_ref, bcn_ref, wpw_ref, bpw_ref,
                        o_ref, ext_ref, sh_ref, c_ref, *, tiles_per_seq, rows):
    i = pl.program_id(0)
    t, w = u_ref.shape
    first = (i % tiles_per_seq) == 0
    ext_ref[0:CONV_HALO, :] = jnp.where(first, 0.0, prev_ref[...])
    ext_ref[CONV_HALO:, :] = u_ref[...]
    n_sh = sh_ref.shape[1]
    for r in range(1, SUBLANES):
        sh_ref[r - 1] = ext_ref[r:r + n_sh, :]
    off = CONV_HALO - (CONV_K - 1)
    for c in range(w // LANES):
        cs = slice(c * LANES, (c + 1) * LANES)
        for rc in range(t // rows):
            acc = jnp.broadcast_to(bdw_ref[:, cs], (rows, LANES))
            for j in range(CONV_K):
                a, r = divmod(off + j, SUBLANES)
                lo = rc * rows + a * SUBLANES
                win = ext_ref[lo:lo + rows, cs] if r == 0 else sh_ref[r - 1, lo:lo + rows, cs]
                acc = acc + win * wdw_ref[j:j + 1, cs]
            c_ref[rc * rows:(rc + 1) * rows, cs] = acc
    o_ref[...] = _conv_tail(c_ref[...], sgc_ref[...], gcn_ref, bcn_ref, wpw_ref, bpw_ref).astype(o_ref.dtype)


def _conv_prompt(u, sgc, w_dw, b_dw, g_cn, b_cn, w_pw_b, b_pw, seq, tile):
    n, w = u.shape
    halo_per_tile = tile // CONV_HALO
    full = lambda shape: pl.BlockSpec(shape, lambda i: (0, 0))
    rowspec = pl.BlockSpec((tile, w), lambda i: (i, 0))
    return pl.pallas_call(
        functools.partial(_conv_prompt_kernel, tiles_per_seq=seq // tile, rows=64),
        grid=(n // tile,),
        in_specs=[
            rowspec,
            pl.BlockSpec((CONV_HALO, w), lambda i: (jnp.maximum(i * halo_per_tile - 1, 0), 0)),
            rowspec,
            full((CONV_K, w)), full((1, w)), full((1, w)), full((1, w)), full((w, w)), full((1, w)),
        ],
        out_specs=rowspec,
        out_shape=jax.ShapeDtypeStruct((n, w), BF16),
        scratch_shapes=[pltpu.VMEM((tile + CONV_HALO, w), F32),
                        pltpu.VMEM((SUBLANES - 1, tile + CONV_HALO - SUBLANES, w), F32),
                        pltpu.VMEM((tile, w), F32)],
        compiler_params=pltpu.CompilerParams(dimension_semantics=("arbitrary",), vmem_limit_bytes=VMEM_LIMIT),
        name="conv_prompt",
    )(u, u, sgc, w_dw, b_dw, g_cn, b_cn, w_pw_b, b_pw)


def _conv_sample_kernel(state_ref, u_ref, sgc_ref, wdw_ref, bdw_ref, gcn_ref, bcn_ref, wpw_ref, bpw_ref,
                        o_ref, new_ref, c_ref):
    nb = state_ref.shape[0]
    hist = CONV_K - 1
    for b in range(nb):
        st = state_ref[b]
        u_row = u_ref[b:b + 1, :]
        c_ref[b:b + 1, :] = (jnp.sum(st * wdw_ref[0:hist, :], axis=0, keepdims=True)
                             + u_row * wdw_ref[hist:hist + 1, :] + bdw_ref[...])
        new_ref[b, 0:hist - 1, :] = state_ref[b, 1:hist, :]
        new_ref[b, hist - 1:hist, :] = u_row
    o_ref[...] = _conv_tail(c_ref[...], sgc_ref[...], gcn_ref, bcn_ref, wpw_ref, bpw_ref).astype(o_ref.dtype)


def _conv_sample(state, u, sgc, w_dw, b_dw, g_cn, b_cn, w_pw_b, b_pw):
    nb, hist, w = state.shape
    full2 = lambda shape: pl.BlockSpec(shape, lambda i: (0, 0))
    full3 = lambda shape: pl.BlockSpec(shape, lambda i: (0, 0, 0))
    return pl.pallas_call(
        _conv_sample_kernel,
        grid=(1,),
        in_specs=[full3((nb, hist, w)), full2((nb, w)), full2((nb, w)),
                  full2((CONV_K, w)), full2((1, w)), full2((1, w)), full2((1, w)), full2((w, w)), full2((1, w))],
        out_specs=[full2((nb, w)), full3((nb, hist, w))],
        out_shape=[jax.ShapeDtypeStruct((nb, w), BF16), jax.ShapeDtypeStruct((nb, hist, w), F32)],
        scratch_shapes=[pltpu.VMEM((nb, w), F32)],
        compiler_params=pltpu.CompilerParams(dimension_semantics=("arbitrary",), vmem_limit_bytes=VMEM_LIMIT),
        name="conv_sample",
    )(state, u, sgc, w_dw, b_dw, g_cn, b_cn, w_pw_b, b_pw)


def _out_kernel(x_ref, a_ref, c_ref, p_ref, woa_ref, woc_ref, bo_ref, g_ref, b_ref, wpg_ref, bpg_ref, wpe_ref,
                o_ref, *, alpha):
    attn = a_ref[...].reshape(c_ref.shape).astype(BF16)
    mix = (jnp.dot(attn, woa_ref[...], preferred_element_type=F32)
           + jnp.dot(c_ref[...].astype(BF16), woc_ref[...], preferred_element_type=F32) + bo_ref[...])
    t = alpha * x_ref[...] + mix
    mu = jnp.mean(t, axis=-1, keepdims=True)
    d = t - mu
    var = jnp.mean(d * d, axis=-1, keepdims=True)
    h = d * lax.rsqrt(var + LN_EPS) * g_ref[...] + b_ref[...]
    gate = _sigmoid(jnp.dot(h.astype(BF16), wpg_ref[...], preferred_element_type=F32) + bpg_ref[...])
    pe = jnp.dot(p_ref[...].astype(BF16), wpe_ref[...], preferred_element_type=F32)
    o_ref[...] = h + gate * pe


def _out_proj(x, attn, attn_spec, conv, p, w_out_b, b_out, g_ln, b_ln, w_pg_b, b_pg, w_pe_b, alpha, tm):
    n, d = x.shape
    half = conv.shape[1]
    pd = p.shape[1]
    const = lambda shape, r=0: pl.BlockSpec(shape, lambda i: (r, 0), pipeline_mode=pl.Buffered(1))
    row = lambda w: pl.BlockSpec((tm, w), lambda i: (i, 0))
    return pl.pallas_call(
        functools.partial(_out_kernel, alpha=alpha),
        grid=(n // tm,),
        in_specs=[row(d), attn_spec, row(half), row(pd),
                  const((half, d), 0), const((half, d), 1), const((1, d)), const((1, d)), const((1, d)),
                  const((d, d)), const((1, d)), const((pd, d))],
        out_specs=row(d),
        out_shape=jax.ShapeDtypeStruct((n, d), F32),
        compiler_params=pltpu.CompilerParams(dimension_semantics=("arbitrary",), vmem_limit_bytes=VMEM_LIMIT),
        name="out_proj",
    )(x, attn, conv, p, w_out_b, w_out_b, b_out, g_ln, b_ln, w_pg_b, b_pg, w_pe_b)


def _select_kernel(q_ref, kmean_ref, o_ref):
    g = jnp.sum(kmean_ref[0] * q_ref[...], axis=-1, keepdims=True)
    n_blocks = g.shape[0]
    idx = lax.broadcasted_iota(jnp.int32, g.shape, 0)
    for t in range(MOBA_TOPK):
        m = jnp.max(g, axis=0, keepdims=True)
        first = jnp.min(jnp.where(g == m, idx, n_blocks), axis=0, keepdims=True)
        o_ref[0, t] = jnp.broadcast_to(first[0], o_ref.shape[2:])
        g = jnp.where(idx == first, -jnp.inf, g)


def _select_sample(q4, kmean):
    n_seq, n_blocks, heads, hd = kmean.shape
    assert n_blocks >= MOBA_TOPK
    return pl.pallas_call(
        _select_kernel,
        grid=(n_seq,),
        in_specs=[pl.BlockSpec((1, heads, hd), lambda b: (b, 0, 0)),
                  pl.BlockSpec((1, n_blocks, heads, hd), lambda b: (b, 0, 0, 0))],
        out_specs=pl.BlockSpec((1, MOBA_TOPK, heads, LANES), lambda b: (b, 0, 0, 0)),
        out_shape=jax.ShapeDtypeStruct((n_seq, MOBA_TOPK, heads, LANES), jnp.int32),
        compiler_params=pltpu.CompilerParams(dimension_semantics=("arbitrary",), vmem_limit_bytes=VMEM_LIMIT),
        name="select_sample",
    )(q4, kmean)


def _moba_sample_kernel(sel_ref, pt_ref, q_ref, kn_ref, vn_ref, sga_ref, ck_ref, cv_ref, o_ref, kbuf, vbuf, sems,
                        *, n_seq, n_pages, heads):
    b = pl.program_id(0)
    n_sel = MOBA_TOPK * PAGES_PER_BLOCK
    scale = HEAD_DIM ** -0.5

    def slab_copies(seq, slot):
        copies = []
        for h in range(heads):
            for t in range(MOBA_TOPK):
                blk = sel_ref[(seq * MOBA_TOPK + t) * heads + h]
                for r in range(PAGES_PER_BLOCK):
                    page = pt_ref[seq * n_pages + blk * PAGES_PER_BLOCK + r]
                    dst = t * PAGES_PER_BLOCK + r
                    copies.append(pltpu.make_async_copy(
                        ck_ref.at[page, :, h, :], kbuf.at[slot, h, dst], sems.at[slot, 0, h]))
                    copies.append(pltpu.make_async_copy(
                        cv_ref.at[page, :, h, :], vbuf.at[slot, h, dst], sems.at[slot, 1, h]))
        return copies

    slot = b % 2

    @pl.when(b == 0)
    def _():
        for c in slab_copies(b, slot):
            c.start()

    @pl.when(b + 1 < n_seq)
    def _():
        for c in slab_copies(b + 1, 1 - slot):
            c.start()

    for c in slab_copies(b, slot):
        c.wait()

    for h in range(heads):
        hs = slice(h, h + 1)
        qh = q_ref[0, hs, :]
        kh = kbuf[slot, h].reshape(n_sel * PAGE_SIZE, HEAD_DIM).astype(BF16)
        vh = vbuf[slot, h].reshape(n_sel * PAGE_SIZE, HEAD_DIM).astype(BF16)
        s = _dot_nt(qh.astype(BF16), kh) * scale
        s_new = jnp.sum(qh * kn_ref[0, hs, :], axis=1, keepdims=True) * scale
        m = jnp.maximum(jnp.max(s, axis=1, keepdims=True), s_new)
        p = jnp.exp(s - m)
        p_new = jnp.exp(s_new - m)
        l = jnp.sum(p, axis=1, keepdims=True) + p_new
        acc = jnp.dot(p.astype(BF16), vh, preferred_element_type=F32) + p_new * vn_ref[0, hs, :]
        o_ref[0, hs, :] = acc / l * sga_ref[0, hs, :]


def _moba_sample(sel_flat, page_table_flat, q4, kn4, vn4, sga4, cache_k, cache_v, n_pages):
    n_seq, heads, hd = q4.shape
    n_sel = MOBA_TOPK * PAGES_PER_BLOCK
    vec = pl.BlockSpec((1, heads, hd), lambda b, sel, pt: (b, 0, 0))
    hbm = pl.BlockSpec(memory_space=pl.ANY)
    return pl.pallas_call(
        functools.partial(_moba_sample_kernel, n_seq=n_seq, n_pages=n_pages, heads=heads),
        grid_spec=pltpu.PrefetchScalarGridSpec(
            num_scalar_prefetch=2,
            grid=(n_seq,),
            in_specs=[vec, vec, vec, vec, hbm, hbm],
            out_specs=vec,
            scratch_shapes=[pltpu.VMEM((2, heads, n_sel, PAGE_SIZE, hd), F32),
                            pltpu.VMEM((2, heads, n_sel, PAGE_SIZE, hd), F32),
                            pltpu.SemaphoreType.DMA((2, 2, heads))],
        ),
        out_shape=jax.ShapeDtypeStruct(q4.shape, F32),
        compiler_params=pltpu.CompilerParams(dimension_semantics=("arbitrary",), vmem_limit_bytes=VMEM_LIMIT),
        name="moba_sample",
    )(sel_flat, page_table_flat, q4, kn4, vn4, sga4, cache_k, cache_v)


def kernel(x_prompt, x_sample, p_prompt, p_sample, cache_k, cache_v, state_conv, page_table, w_in, b_in, w_dw,
           b_dw, g_cn, b_cn, w_pw, b_pw, w_out, b_out, g_ln, b_ln, w_pe, w_pg, b_pg):
    depth = w_in.shape[0]
    assert depth == 1
    batch, seq, d_model = x_prompt.shape
    n_seq, dec_seq, _ = x_sample.shape
    assert dec_seq == 1
    n_phys, page, heads, head_dim = cache_k.shape[1:]
    assert head_dim == HEAD_DIM and page == PAGE_SIZE
    attn_w = heads * head_dim
    conv_w = w_pw.shape[1]
    assert conv_w == attn_w and w_in.shape[2] == 7 * attn_w
    n_pages = page_table.shape[1]
    hist = CONV_K - 1
    alpha = (2 * depth) ** 0.25
    n = batch * seq

    row2 = lambda a: a.reshape(1, -1)
    w_in_b = w_in[0].astype(BF16)
    w_pw_b = w_pw[0].astype(BF16)
    w_out_b = w_out[0].astype(BF16)
    w_pg_b = w_pg[0].astype(BF16)
    w_pe_b = w_pe[0].astype(BF16)
    b_in2, b_dw2, g_cn2, b_cn2, b_pw2 = row2(b_in[0]), row2(b_dw[0]), row2(g_cn[0]), row2(b_cn[0]), row2(b_pw[0])
    b_out2, g_ln2, b_ln2, b_pg2 = row2(b_out[0]), row2(g_ln[0]), row2(b_ln[0]), row2(b_pg[0])
    conv_w_args = (w_dw[0], b_dw2, g_cn2, b_cn2, w_pw_b, b_pw2)
    out_w_args = (w_out_b, b_out2, g_ln2, b_ln2, w_pg_b, b_pg2, w_pe_b, alpha)

    xp = x_prompt.reshape(n, d_model)
    q, k, v, sga, u, sgc = _in_proj(xp, w_in_b, b_in2, tm=512)
    pt_flat = page_table.reshape(-1)
    ck = cache_k.reshape(n_phys, page, heads, head_dim)
    cv = cache_v.reshape(n_phys, page, heads, head_dim)
    attn, kmean = _moba_prompt(q, k, v, sga, batch, seq, heads, pt_flat, ck, n_seq, n_pages)
    conv = _conv_prompt(u, sgc, *conv_w_args, seq=seq, tile=MOBA_BLOCK)
    n_blocks = seq // MOBA_BLOCK

    def paired_rows(r):
        blk = r % n_blocks
        mirror = n_blocks - 1 - blk
        return (r // n_blocks, jnp.minimum(blk, mirror), jnp.where(blk > mirror, 1, 0), 0, 0)

    y_prompt = _out_proj(xp, attn, pl.BlockSpec((1, 1, 1, MOBA_BLOCK, attn_w), paired_rows), conv,
                         p_prompt[0].reshape(n, -1), *out_w_args, tm=MOBA_BLOCK)
    conv_prompt_new = u.reshape(batch, seq, conv_w)[:, seq - hist:, :]

    xs = x_sample.reshape(n_seq, d_model)
    qs, ks, vs, sgas, us, sgcs = _in_proj(xs, w_in_b, b_in2, tm=n_seq)
    conv_s, conv_sample_new = _conv_sample(state_conv[0], us, sgcs, *conv_w_args)
    as4 = lambda a: a.reshape(n_seq, heads, head_dim)
    sel = _select_sample(as4(qs), kmean)[:, :, :, 0]
    attn_s = _moba_sample(sel.reshape(-1), pt_flat, as4(qs), as4(ks), as4(vs), as4(sgas), ck, cv, n_pages)
    y_sample = _out_proj(xs, attn_s.reshape(n_seq, attn_w), pl.BlockSpec((n_seq, attn_w), lambda r: (r, 0)), conv_s,
                         p_sample[0].reshape(n_seq, -1), *out_w_args, tm=n_seq)

    kv_shape = (depth, batch, seq, heads, head_dim)
    kvs_shape = (depth, n_seq, dec_seq, heads, head_dim)
    return (y_prompt.reshape(batch, seq, d_model), y_sample.reshape(n_seq, dec_seq, d_model),
            k.reshape(kv_shape), v.reshape(kv_shape), conv_prompt_new.reshape(depth, batch, hist, conv_w),
            ks.reshape(kvs_shape), vs.reshape(kvs_shape), conv_sample_new.reshape(depth, n_seq, hist, conv_w))
```

```python
import functools

import jax
import jax.numpy as jnp
from jax import lax
from jax.experimental import pallas as pl
from jax.experimental.pallas import tpu as pltpu

F32 = jnp.float32
BF16 = jnp.bfloat16

HEAD_DIM = 128
CONV_K = 31
MOBA_BLOCK = 256
MOBA_TOPK = 3
PAGE_SIZE = 128
PAGES_PER_BLOCK = MOBA_BLOCK // PAGE_SIZE
LN_EPS = 1e-5
NEG = -1e30
LOG2_E = 1.4426950408889634
LANES = 128
SUBLANES = 8
SUM_ROWS = 16
CONV_HALO = 32
VMEM_LIMIT = 56 * 1024 * 1024


def _sigmoid(z):
    return 0.5 * jnp.tanh(0.5 * z) + 0.5


def _silu(z):
    return z * _sigmoid(z)


def _dot_nt(a, b, **kw):
    return lax.dot_general(a, b, (((1,), (1,)), ((), ())), preferred_element_type=F32, **kw)


def _in_proj_kernel(x_ref, w_ref, b_ref, q_ref, k_ref, v_ref, sga_ref, u_ref, sgc_ref, xb_ref, a_ref):
    j = pl.program_id(1)

    @pl.when(j == 0)
    def _():
        xb_ref[...] = x_ref[...].astype(BF16)

    def column_group(g, finish):
        @pl.when(j == g)
        def _():
            finish(jnp.dot(xb_ref[...], w_ref[...], preferred_element_type=F32) + b_ref[...])

    def store(ref, fn=lambda z: z):
        def finish(z):
            ref[...] = fn(z)
        return finish

    def glu(z):
        u_ref[...] = a_ref[...] * _sigmoid(z)

    for g, finish in enumerate((store(q_ref), store(k_ref), store(v_ref), store(sga_ref, _silu), store(a_ref), glu,
                                store(sgc_ref, _silu))):
        column_group(g, finish)


def _in_proj(x, w_b, b, tm):
    n, d = x.shape
    gw = w_b.shape[1] // 7
    out = jax.ShapeDtypeStruct((n, gw), F32)

    def out_spec(written_at):
        return pl.BlockSpec((tm, gw), lambda i, j: (jnp.where(j >= written_at, i, jnp.maximum(i - 1, 0)), 0))

    return pl.pallas_call(
        _in_proj_kernel,
        grid=(n // tm, 7),
        in_specs=[
            pl.BlockSpec((tm, d), lambda i, j: (i, 0)),
            pl.BlockSpec((d, gw), lambda i, j: (0, j)),
            pl.BlockSpec((1, gw), lambda i, j: (0, j)),
        ],
        out_specs=[out_spec(g) for g in (0, 1, 2, 3, 5, 6)],
        out_shape=[out] * 6,
        scratch_shapes=[pltpu.VMEM((tm, d), BF16), pltpu.VMEM((tm, gw), F32)],
        compiler_params=pltpu.CompilerParams(
            dimension_semantics=("arbitrary", "arbitrary"), vmem_limit_bytes=VMEM_LIMIT),
        name="in_proj",
    )(x, w_b, b)


def _topk_mask(g, idx, n_valid, axis):
    sel = jnp.zeros(g.shape, F32)
    big = g.shape[axis]
    for t in range(MOBA_TOPK):
        m = jnp.max(g, axis=axis, keepdims=True)
        first = jnp.min(jnp.where(g == m, idx, big), axis=axis, keepdims=True)
        hit = idx == first
        counts = jnp.where(t < n_valid, 1.0, 0.0)
        sel = jnp.maximum(sel, jnp.where(hit, counts, 0.0))
        g = jnp.where(hit, -jnp.inf, g)
    return sel


def _moba_prompt_kernel(pt_ref, qa_ref, qb_ref, k_ref, v_ref, sgaa_ref, sgab_ref, *refs, n_blocks, pages_per_step):
    del pt_ref
    page_refs = refs[:pages_per_step]
    o_ref, pmean_ref, kb_ref, vt_ref, kmean_ref, qt_ref, bias_ref, s_ref, acc_ref = refs[pages_per_step:]
    i = pl.program_id(2)
    blk = MOBA_BLOCK
    last = n_blocks - 1
    qscale = HEAD_DIM ** -0.5 * LOG2_E
    own = (i, last - i)
    q_refs, sga_refs = (qa_ref, qb_ref), (sgaa_ref, sgab_ref)

    @pl.when(i == 0)
    def _():
        kmean_ref[...] = jnp.mean(k_ref[...].reshape(n_blocks, blk, HEAD_DIM), axis=1)
        for jb in range(n_blocks):
            rows = slice(jb * blk, (jb + 1) * blk)
            kb_ref[jb] = k_ref[rows, :].astype(BF16)
            vt_ref[jb, 0:HEAD_DIM, :] = v_ref[rows, :].T.astype(BF16)
            vt_ref[jb, HEAD_DIM:, :] = jnp.ones((SUM_ROWS, blk), BF16)

    def cache_block_mean(a):
        tot = jnp.sum(page_refs[PAGES_PER_BLOCK * a][0], axis=0)
        for r in range(1, PAGES_PER_BLOCK):
            tot = tot + jnp.sum(page_refs[PAGES_PER_BLOCK * a + r][0], axis=0)
        pmean_ref[0, a] = tot * (1.0 / MOBA_BLOCK)

    for w in range(2):
        qf_t = q_refs[w][...].T
        qt_ref[w] = (qf_t * qscale).astype(BF16)
        gate = jnp.dot(kmean_ref[...], qf_t, preferred_element_type=F32, precision=lax.Precision.HIGHEST)
        row = lax.broadcasted_iota(jnp.int32, gate.shape, 0)
        gate = jnp.where(row < own[w], gate, -jnp.inf)
        sel = _topk_mask(gate, row, own[w], axis=0)
        bias_ref[w] = jnp.where(sel > 0.5, 0.0, NEG)

    def past_slot(t):
        first = t < i
        return jnp.where(first, 1.0, 0.0), jnp.where(first, 0, 1), jnp.where(first, t, t - i)

    k_id = lax.broadcasted_iota(jnp.int32, (blk, blk), 0)
    q_id = lax.broadcasted_iota(jnp.int32, (blk, blk), 1)
    m = []
    for w in range(2):
        s = jnp.dot(kb_ref[own[w]], qt_ref[w], preferred_element_type=F32)
        s = jnp.where(k_id <= q_id, s, NEG)
        s_ref[w] = s
        m.append(jnp.max(s, axis=0, keepdims=True))
    for t in range(last):
        fa, w, kblk = past_slot(t)
        s = jnp.dot(kb_ref[kblk], qt_ref[w], preferred_element_type=F32)
        s_ref[2 + t] = s
        c = jnp.max(s, axis=0, keepdims=True) + bias_ref[w, pl.ds(kblk, 1), :]
        m[0] = jnp.maximum(m[0], c + (fa - 1.0) * -NEG)
        m[1] = jnp.maximum(m[1], c + fa * NEG)
        if t < pages_per_step // PAGES_PER_BLOCK:
            cache_block_mean(t)

    for w in range(2):
        p = jnp.exp2((s_ref[w] - m[w]).astype(BF16))
        acc_ref[w] = jnp.dot(vt_ref[own[w]], p, preferred_element_type=F32)
    for t in range(last):
        fa, w, kblk = past_slot(t)
        shift = bias_ref[w, pl.ds(kblk, 1), :] - (m[1] + fa * (m[0] - m[1]))
        p = jnp.exp2((s_ref[2 + t] + shift).astype(BF16))
        acc_ref[2 + t] = jnp.dot(vt_ref[kblk], p, preferred_element_type=F32)

    bounds = ((0, i), (i, last))
    for w in range(2):
        acc = lax.fori_loop(bounds[w][0], bounds[w][1], lambda t, a: a + acc_ref[2 + t], acc_ref[w])
        out_t = acc[0:HEAD_DIM, :] / acc[HEAD_DIM:HEAD_DIM + 1, :]
        o_ref[0, 0, w] = (out_t.T * sga_refs[w][...]).astype(o_ref.dtype)


def _moba_prompt(q, k, v, sga, batch, seq, heads, page_table_flat, cache, n_seq, n_pages):
    n_blocks = seq // MOBA_BLOCK
    assert n_blocks % 2 == 0
    last = n_blocks - 1
    pairs = n_blocks // 2
    n_phys, page, c_heads, hd = cache.shape
    steps = batch * heads * pairs
    pages_per_step = n_seq * n_pages // steps
    parts = n_pages // pages_per_step
    assert pages_per_step * steps == n_seq * n_pages and parts * pages_per_step == n_pages
    blocks_per_step = pages_per_step // PAGES_PER_BLOCK
    assert blocks_per_step * PAGES_PER_BLOCK == pages_per_step and blocks_per_step <= last

    def step_id(b, h, i):
        return (b * heads + h) * pairs + i

    def page_spec(r):
        return pl.BlockSpec((1, page, c_heads, hd),
                            lambda b, h, i, pt: (pt[step_id(b, h, i) * pages_per_step + r], 0, 0, 0))

    qa = pl.BlockSpec((MOBA_BLOCK, HEAD_DIM), lambda b, h, i, pt: (b * n_blocks + i, h))
    qb = pl.BlockSpec((MOBA_BLOCK, HEAD_DIM), lambda b, h, i, pt: (b * n_blocks + last - i, h))
    kvspec = pl.BlockSpec((seq, HEAD_DIM), lambda b, h, i, pt: (b, h))
    return pl.pallas_call(
        functools.partial(_moba_prompt_kernel, n_blocks=n_blocks, pages_per_step=pages_per_step),
        grid_spec=pltpu.PrefetchScalarGridSpec(
            num_scalar_prefetch=1,
            grid=(batch, heads, pairs),
            in_specs=[qa, qb, kvspec, kvspec, qa, qb] + [page_spec(r) for r in range(pages_per_step)],
            out_specs=[
                pl.BlockSpec((1, 1, 2, MOBA_BLOCK, HEAD_DIM), lambda b, h, i, pt: (b, i, 0, 0, h)),
                pl.BlockSpec((1, blocks_per_step, c_heads, hd),
                             lambda b, h, i, pt: (step_id(b, h, i) // parts, step_id(b, h, i) % parts, 0, 0)),
            ],
            scratch_shapes=[pltpu.VMEM((n_blocks, MOBA_BLOCK, HEAD_DIM), BF16),
                            pltpu.VMEM((n_blocks, HEAD_DIM + SUM_ROWS, MOBA_BLOCK), BF16),
                            pltpu.VMEM((n_blocks, HEAD_DIM), F32),
                            pltpu.VMEM((2, HEAD_DIM, MOBA_BLOCK), BF16),
                            pltpu.VMEM((2, n_blocks, MOBA_BLOCK), F32),
                            pltpu.VMEM((n_blocks + 1, MOBA_BLOCK, MOBA_BLOCK), F32),
                            pltpu.VMEM((n_blocks + 1, HEAD_DIM + SUM_ROWS, MOBA_BLOCK), F32)],
        ),
        out_shape=[jax.ShapeDtypeStruct((batch, pairs, 2, MOBA_BLOCK, q.shape[1]), BF16),
                   jax.ShapeDtypeStruct((n_seq, n_pages // PAGES_PER_BLOCK, c_heads, hd), F32)],
        compiler_params=pltpu.CompilerParams(
            dimension_semantics=("arbitrary", "arbitrary", "arbitrary"), vmem_limit_bytes=VMEM_LIMIT),
        name="moba_prompt",
    )(page_table_flat, q, q, k, v, sga, sga, *([cache] * pages_per_step))


def _conv_tail(c, sgc, gcn_ref, bcn_ref, wpw_ref, bpw_ref):
    mu = jnp.mean(c, axis=-1, keepdims=True)
    d = c - mu
    var = jnp.mean(d * d, axis=-1, keepdims=True)
    y = d * lax.rsqrt(var + LN_EPS) * gcn_ref[...] + bcn_ref[...]
    y = _silu(y)
    return (jnp.dot(y.astype(BF16), wpw_ref[...], preferred_element_type=F32) + bpw_ref[...]) * sgc


def _conv_prompt_kernel(u_ref, prev_ref, sgc_ref, wdw_ref, bdw_ref, gcn_ref, bcn_ref, wpw_ref, bpw_ref,
                        o_ref, ext_ref, sh_ref, c_ref, *, tiles_per_seq, rows):
    i = pl.program_id(0)
    t, w = u_ref.shape
    first = (i % tiles_per_seq) == 0
    ext_ref[0:CONV_HALO, :] = jnp.where(first, 0.0, prev_ref[...])
    ext_ref[CONV_HALO:, :] = u_ref[...]
    n_sh = sh_ref.shape[1]
    for r in range(1, SUBLANES):
        sh_ref[r - 1] = ext_ref[r:r + n_sh, :]
    off = CONV_HALO - (CONV_K - 1)
    for c in range(w // LANES):
        cs = slice(c * LANES, (c + 1) * LANES)
        for rc in range(t // rows):
            acc = jnp.broadcast_to(bdw_ref[:, cs], (rows, LANES))
            for j in range(CONV_K):
                a, r = divmod(off + j, SUBLANES)
                lo = rc * rows + a * SUBLANES
                win = ext_ref[lo:lo + rows, cs] if r == 0 else sh_ref[r - 1, lo:lo + rows, cs]
                acc = acc + win * wdw_ref[j:j + 1, cs]
            c_ref[rc * rows:(rc + 1) * rows, cs] = acc
    o_ref[...] = _conv_tail(c_ref[...], sgc_ref[...], gcn_ref, bcn_ref, wpw_ref, bpw_ref).astype(o_ref.dtype)


def _conv_prompt(u, sgc, w_dw, b_dw, g_cn, b_cn, w_pw_b, b_pw, seq, tile):
    n, w = u.shape
    halo_per_tile = tile // CONV_HALO
    full = lambda shape: pl.BlockSpec(shape, lambda i: (0, 0))
    rowspec = pl.BlockSpec((tile, w), lambda i: (i, 0))
    return pl.pallas_call(
        functools.partial(_conv_prompt_kernel, tiles_per_seq=seq // tile, rows=64),
        grid=(n // tile,),
        in_specs=[
            rowspec,
            pl.BlockSpec((CONV_HALO, w), lambda i: (jnp.maximum(i * halo_per_tile - 1, 0), 0)),
            rowspec,
            full((CONV_K, w)), full((1, w)), full((1, w)), full((1, w)), full((w, w)), full((1, w)),
        ],
        out_specs=rowspec,
        out_shape=jax.ShapeDtypeStruct((n, w), BF16),
        scratch_shapes=[pltpu.VMEM((tile + CONV_HALO, w), F32),
                        pltpu.VMEM((SUBLANES - 1, tile + CONV_HALO - SUBLANES, w), F32),
                        pltpu.VMEM((tile, w), F32)],
        compiler_params=pltpu.CompilerParams(dimension_semantics=("arbitrary",), vmem_limit_bytes=VMEM_LIMIT),
        name="conv_prompt",
    )(u, u, sgc, w_dw, b_dw, g_cn, b_cn, w_pw_b, b_pw)


def _conv_sample_kernel(state_ref, u_ref, sgc_ref, wdw_ref, bdw_ref, gcn_ref, bcn_ref, wpw_ref, bpw_ref,
                        o_ref, new_ref, c_ref):
    nb = state_ref.shape[0]
    hist = CONV_K - 1
    for b in range(nb):
        st = state_ref[b]
        u_row = u_ref[b:b + 1, :]
        c_ref[b:b + 1, :] = (jnp.sum(st * wdw_ref[0:hist, :], axis=0, keepdims=True)
                             + u_row * wdw_ref[hist:hist + 1, :] + bdw_ref[...])
        new_ref[b, 0:hist - 1, :] = state_ref[b, 1:hist, :]
        new_ref[b, hist - 1:hist, :] = u_row
    o_ref[...] = _conv_tail(c_ref[...], sgc_ref[...], gcn_ref, bcn_ref, wpw_ref, bpw_ref).astype(o_ref.dtype)


def _conv_sample(state, u, sgc, w_dw, b_dw, g_cn, b_cn, w_pw_b, b_pw):
    nb, hist, w = state.shape
    full2 = lambda shape: pl.BlockSpec(shape, lambda i: (0, 0))
    full3 = lambda shape: pl.BlockSpec(shape, lambda i: (0, 0, 0))
    return pl.pallas_call(
        _conv_sample_kernel,
        grid=(1,),
        in_specs=[full3((nb, hist, w)), full2((nb, w)), full2((nb, w)),
                  full2((CONV_K, w)), full2((1, w)), full2((1, w)), full2((1, w)), full2((w, w)), full2((1, w))],
        out_specs=[full2((nb, w)), full3((nb, hist, w))],
        out_shape=[jax.ShapeDtypeStruct((nb, w), BF16), jax.ShapeDtypeStruct((nb, hist, w), F32)],
        scratch_shapes=[pltpu.VMEM((nb, w), F32)],
        compiler_params=pltpu.CompilerParams(dimension_semantics=("arbitrary",), vmem_limit_bytes=VMEM_LIMIT),
        name="conv_sample",
    )(state, u, sgc, w_dw, b_dw, g_cn, b_cn, w_pw_b, b_pw)


def _out_kernel(x_ref, a_ref, c_ref, p_ref, woa_ref, woc_ref, bo_ref, g_ref, b_ref, wpg_ref, bpg_ref, wpe_ref,
                o_ref, *, alpha):
    attn = a_ref[...].reshape(c_ref.shape).astype(BF16)
    mix = (jnp.dot(attn, woa_ref[...], preferred_element_type=F32)
           + jnp.dot(c_ref[...].astype(BF16), woc_ref[...], preferred_element_type=F32) + bo_ref[...])
    t = alpha * x_ref[...] + mix
    mu = jnp.mean(t, axis=-1, keepdims=True)
    d = t - mu
    var = jnp.mean(d * d, axis=-1, keepdims=True)
    h = d * lax.rsqrt(var + LN_EPS) * g_ref[...] + b_ref[...]
    gate = _sigmoid(jnp.dot(h.astype(BF16), wpg_ref[...], preferred_element_type=F32) + bpg_ref[...])
    pe = jnp.dot(p_ref[...].astype(BF16), wpe_ref[...], preferred_element_type=F32)
    o_ref[...] = h + gate * pe


def _out_proj(x, attn, attn_spec, conv, p, w_out_b, b_out, g_ln, b_ln, w_pg_b, b_pg, w_pe_b, alpha, tm):
    n, d = x.shape
    half = conv.shape[1]
    pd = p.shape[1]
    const = lambda shape, r=0: pl.BlockSpec(shape, lambda i: (r, 0), pipeline_mode=pl.Buffered(1))
    row = lambda w: pl.BlockSpec((tm, w), lambda i: (i, 0))
    return pl.pallas_call(
        functools.partial(_out_kernel, alpha=alpha),
        grid=(n // tm,),
        in_specs=[row(d), attn_spec, row(half), row(pd),
                  const((half, d), 0), const((half, d), 1), const((1, d)), const((1, d)), const((1, d)),
                  const((d, d)), const((1, d)), const((pd, d))],
        out_specs=row(d),
        out_shape=jax.ShapeDtypeStruct((n, d), F32),
        compiler_params=pltpu.CompilerParams(dimension_semantics=("arbitrary",), vmem_limit_bytes=VMEM_LIMIT),
        name="out_proj",
    )(x, attn, conv, p, w_out_b, w_out_b, b_out, g_ln, b_ln, w_pg_b, b_pg, w_pe_b)


def _select_kernel(q_ref, kmean_ref, o_ref):
    g = jnp.sum(kmean_ref[0] * q_ref[...], axis=-1, keepdims=True)
    n_blocks = g.shape[0]
    idx = lax.broadcasted_iota(jnp.int32, g.shape, 0)
    for t in range(MOBA_TOPK):
        m = jnp.max(g, axis=0, keepdims=True)
        first = jnp.min(jnp.where(g == m, idx, n_blocks), axis=0, keepdims=True)
        o_ref[0, t] = jnp.broadcast_to(first[0], o_ref.shape[2:])
        g = jnp.where(idx == first, -jnp.inf, g)


def _select_sample(q4, kmean):
    n_seq, n_blocks, heads, hd = kmean.shape
    assert n_blocks >= MOBA_TOPK
    return pl.pallas_call(
        _select_kernel,
        grid=(n_seq,),
        in_specs=[pl.BlockSpec((1, heads, hd), lambda b: (b, 0, 0)),
                  pl.BlockSpec((1, n_blocks, heads, hd), lambda b: (b, 0, 0, 0))],
        out_specs=pl.BlockSpec((1, MOBA_TOPK, heads, LANES), lambda b: (b, 0, 0, 0)),
        out_shape=jax.ShapeDtypeStruct((n_seq, MOBA_TOPK, heads, LANES), jnp.int32),
        compiler_params=pltpu.CompilerParams(dimension_semantics=("arbitrary",), vmem_limit_bytes=VMEM_LIMIT),
        name="select_sample",
    )(q4, kmean)


def _moba_sample_kernel(sel_ref, pt_ref, q_ref, kn_ref, vn_ref, sga_ref, ck_ref, cv_ref, o_ref, kbuf, vbuf, sems,
                        *, n_seq, n_pages, heads):
    b = pl.program_id(0)
    n_sel = MOBA_TOPK * PAGES_PER_BLOCK
    scale = HEAD_DIM ** -0.5

    def slab_copies(seq, slot):
        copies = []
        for h in range(heads):
            for t in range(MOBA_TOPK):
                blk = sel_ref[(seq * MOBA_TOPK + t) * heads + h]
                for r in range(PAGES_PER_BLOCK):
                    page = pt_ref[seq * n_pages + blk * PAGES_PER_BLOCK + r]
                    dst = t * PAGES_PER_BLOCK + r
                    copies.append(pltpu.make_async_copy(
                        ck_ref.at[page, :, h, :], kbuf.at[slot, h, dst], sems.at[slot, 0, h]))
                    copies.append(pltpu.make_async_copy(
                        cv_ref.at[page, :, h, :], vbuf.at[slot, h, dst], sems.at[slot, 1, h]))
        return copies

    slot = b % 2

    @pl.when(b == 0)
    def _():
        for c in slab_copies(b, slot):
            c.start()

    @pl.when(b + 1 < n_seq)
    def _():
        for c in slab_copies(b + 1, 1 - slot):
            c.start()

    for c in slab_copies(b, slot):
        c.wait()

    for h in range(heads):
        hs = slice(h, h + 1)
        qh = q_ref[0, hs, :]
        kh = kbuf[slot, h].reshape(n_sel * PAGE_SIZE, HEAD_DIM).astype(BF16)
        vh = vbuf[slot, h].reshape(n_sel * PAGE_SIZE, HEAD_DIM).astype(BF16)
        s = _dot_nt(qh.astype(BF16), kh) * scale
        s_new = jnp.sum(qh * kn_ref[0, hs, :], axis=1, keepdims=True) * scale
        m = jnp.maximum(jnp.max(s, axis=1, keepdims=True), s_new)
        p = jnp.exp(s - m)
        p_new = jnp.exp(s_new - m)
        l = jnp.sum(p, axis=1, keepdims=True) + p_new
        acc = jnp.dot(p.astype(BF16), vh, preferred_element_type=F32) + p_new * vn_ref[0, hs, :]
        o_ref[0, hs, :] = acc / l * sga_ref[0, hs, :]


def _moba_sample(sel_flat, page_table_flat, q4, kn4, vn4, sga4, cache_k, cache_v, n_pages):
    n_seq, heads, hd = q4.shape
    n_sel = MOBA_TOPK * PAGES_PER_BLOCK
    vec = pl.BlockSpec((1, heads, hd), lambda b, sel, pt: (b, 0, 0))
    hbm = pl.BlockSpec(memory_space=pl.ANY)
    return pl.pallas_call(
        functools.partial(_moba_sample_kernel, n_seq=n_seq, n_pages=n_pages, heads=heads),
        grid_spec=pltpu.PrefetchScalarGridSpec(
            num_scalar_prefetch=2,
            grid=(n_seq,),
            in_specs=[vec, vec, vec, vec, hbm, hbm],
            out_specs=vec,
            scratch_shapes=[pltpu.VMEM((2, heads, n_sel, PAGE_SIZE, hd), F32),
                            pltpu.VMEM((2, heads, n_sel, PAGE_SIZE, hd), F32),
                            pltpu.SemaphoreType.DMA((2, 2, heads))],
        ),
        out_shape=jax.ShapeDtypeStruct(q4.shape, F32),
        compiler_params=pltpu.CompilerParams(dimension_semantics=("arbitrary",), vmem_limit_bytes=VMEM_LIMIT),
        name="moba_sample",
    )(sel_flat, page_table_flat, q4, kn4, vn4, sga4, cache_k, cache_v)


def kernel(x_prompt, x_sample, p_prompt, p_sample, cache_k, cache_v, state_conv, page_table, w_in, b_in, w_dw,
           b_dw, g_cn, b_cn, w_pw, b_pw, w_out, b_out, g_ln, b_ln, w_pe, w_pg, b_pg):
    depth = w_in.shape[0]
    assert depth == 1
    batch, seq, d_model = x_prompt.shape
    n_seq, dec_seq, _ = x_sample.shape
    assert dec_seq == 1
    n_phys, page, heads, head_dim = cache_k.shape[1:]
    assert head_dim == HEAD_DIM and page == PAGE_SIZE
    attn_w = heads * head_dim
    conv_w = w_pw.shape[1]
    assert conv_w == attn_w and w_in.shape[2] == 7 * attn_w
    n_pages = page_table.shape[1]
    hist = CONV_K - 1
    alpha = (2 * depth) ** 0.25
    n = batch * seq

    row2 = lambda a: a.reshape(1, -1)
    w_in_b = w_in[0].astype(BF16)
    w_pw_b = w_pw[0].astype(BF16)
    w_out_b = w_out[0].astype(BF16)
    w_pg_b = w_pg[0].astype(BF16)
    w_pe_b = w_pe[0].astype(BF16)
    b_in2, b_dw2, g_cn2, b_cn2, b_pw2 = row2(b_in[0]), row2(b_dw[0]), row2(g_cn[0]), row2(b_cn[0]), row2(b_pw[0])
    b_out2, g_ln2, b_ln2, b_pg2 = row2(b_out[0]), row2(g_ln[0]), row2(b_ln[0]), row2(b_pg[0])
    conv_w_args = (w_dw[0], b_dw2, g_cn2, b_cn2, w_pw_b, b_pw2)
    out_w_args = (w_out_b, b_out2, g_ln2, b_ln2, w_pg_b, b_pg2, w_pe_b, alpha)

    xp = x_prompt.reshape(n, d_model)
    q, k, v, sga, u, sgc = _in_proj(xp, w_in_b, b_in2, tm=512)
    pt_flat = page_table.reshape(-1)
    ck = cache_k.reshape(n_phys, page, heads, head_dim)
    cv = cache_v.reshape(n_phys, page, heads, head_dim)
    attn, kmean = _moba_prompt(q, k, v, sga, batch, seq, heads, pt_flat, ck, n_seq, n_pages)
    conv = _conv_prompt(u, sgc, *conv_w_args, seq=seq, tile=MOBA_BLOCK)
    n_blocks = seq // MOBA_BLOCK

    def paired_rows(r):
        blk = r % n_blocks
        mirror = n_blocks - 1 - blk
        return (r // n_blocks, jnp.minimum(blk, mirror), jnp.where(blk > mirror, 1, 0), 0, 0)

    y_prompt = _out_proj(xp, attn, pl.BlockSpec((1, 1, 1, MOBA_BLOCK, attn_w), paired_rows), conv,
                         p_prompt[0].reshape(n, -1), *out_w_args, tm=MOBA_BLOCK)
    conv_prompt_new = u.reshape(batch, seq, conv_w)[:, seq - hist:, :]

    xs = x_sample.reshape(n_seq, d_model)
    qs, ks, vs, sgas, us, sgcs = _in_proj(xs, w_in_b, b_in2, tm=n_seq)
    conv_s, conv_sample_new = _conv_sample(state_conv[0], us, sgcs, *conv_w_args)
    as4 = lambda a: a.reshape(n_seq, heads, head_dim)
    sel = _select_sample(as4(qs), kmean)[:, :, :, 0]
    attn_s = _moba_sample(sel.reshape(-1), pt_flat, as4(qs), as4(ks), as4(vs), as4(sgas), ck, cv, n_pages)
    y_sample = _out_proj(xs, attn_s.reshape(n_seq, attn_w), pl.BlockSpec((n_seq, attn_w), lambda r: (r, 0)), conv_s,
                         p_sample[0].reshape(n_seq, -1), *out_w_args, tm=n_seq)

    kv_shape = (depth, batch, seq, heads, head_dim)
    kvs_shape = (depth, n_seq, dec_seq, heads, head_dim)
    return (y_prompt.reshape(batch, seq, d_model), y_sample.reshape(n_seq, dec_seq, d_model),
            k.reshape(kv_shape), v.reshape(kv_shape), conv_prompt_new.reshape(depth, batch, hist, conv_w),
            ks.reshape(kvs_shape), vs.reshape(kvs_shape), conv_sample_new.reshape(depth, n_seq, hist, conv_w))
```

```python
import functools

import jax
import jax.numpy as jnp
from jax import lax
from jax.experimental import pallas as pl
from jax.experimental.pallas import tpu as pltpu

F32 = jnp.float32
BF16 = jnp.bfloat16

HEAD_DIM = 128
CONV_K = 31
MOBA_BLOCK = 256
MOBA_TOPK = 3
PAGE_SIZE = 128
PAGES_PER_BLOCK = MOBA_BLOCK // PAGE_SIZE
LN_EPS = 1e-5
NEG = -1e30
LOG2_E = 1.4426950408889634
LANES = 128
SUBLANES = 8
SUM_ROWS = 16
CONV_HALO = 32
VMEM_LIMIT = 56 * 1024 * 1024


def _sigmoid(z):
    return 0.5 * jnp.tanh(0.5 * z) + 0.5


def _silu(z):
    return z * _sigmoid(z)


def _dot_nt(a, b, **kw):
    return lax.dot_general(a, b, (((1,), (1,)), ((), ())), preferred_element_type=F32, **kw)


def _in_proj_kernel(x_ref, w_ref, b_ref, q_ref, k_ref, v_ref, sga_ref, u_ref, sgc_ref, xb_ref, a_ref):
    j = pl.program_id(1)

    @pl.when(j == 0)
    def _():
        xb_ref[...] = x_ref[...].astype(BF16)

    def column_group(g, finish):
        @pl.when(j == g)
        def _():
            finish(jnp.dot(xb_ref[...], w_ref[...], preferred_element_type=F32) + b_ref[...])

    def store(ref, fn=lambda z: z):
        def finish(z):
            ref[...] = fn(z)
        return finish

    def glu(z):
        u_ref[...] = a_ref[...] * _sigmoid(z)

    for g, finish in enumerate((store(q_ref), store(k_ref), store(v_ref), store(sga_ref, _silu), store(a_ref), glu,
                                store(sgc_ref, _silu))):
        column_group(g, finish)


def _in_proj(x, w_b, b, tm):
    n, d = x.shape
    gw = w_b.shape[1] // 7
    out = jax.ShapeDtypeStruct((n, gw), F32)

    def out_spec(written_at):
        return pl.BlockSpec((tm, gw), lambda i, j: (jnp.where(j >= written_at, i, jnp.maximum(i - 1, 0)), 0))

    return pl.pallas_call(
        _in_proj_kernel,
        grid=(n // tm, 7),
        in_specs=[
            pl.BlockSpec((tm, d), lambda i, j: (i, 0)),
            pl.BlockSpec((d, gw), lambda i, j: (0, j)),
            pl.BlockSpec((1, gw), lambda i, j: (0, j)),
        ],
        out_specs=[out_spec(g) for g in (0, 1, 2, 3, 5, 6)],
        out_shape=[out] * 6,
        scratch_shapes=[pltpu.VMEM((tm, d), BF16), pltpu.VMEM((tm, gw), F32)],
        compiler_params=pltpu.CompilerParams(
            dimension_semantics=("arbitrary", "arbitrary"), vmem_limit_bytes=VMEM_LIMIT),
        name="in_proj",
    )(x, w_b, b)


def _topk_mask(g, idx, n_valid, axis):
    sel = jnp.zeros(g.shape, F32)
    big = g.shape[axis]
    for t in range(MOBA_TOPK):
        m = jnp.max(g, axis=axis, keepdims=True)
        first = jnp.min(jnp.where(g == m, idx, big), axis=axis, keepdims=True)
        hit = idx == first
        counts = jnp.where(t < n_valid, 1.0, 0.0)
        sel = jnp.maximum(sel, jnp.where(hit, counts, 0.0))
        g = jnp.where(hit, -jnp.inf, g)
    return sel


def _moba_prompt_kernel(pt_ref, qa_ref, qb_ref, qna_ref, qnb_ref, k_ref, v_ref, sgaa_ref, sgab_ref, *refs,
                        n_blocks, pages_per_step):
    del pt_ref
    page_refs = refs[:pages_per_step]
    o_ref, pmean_ref, kb_ref, vt_ref, kmean_ref, acc_ref = refs[pages_per_step:pages_per_step + 6]
    bufs = refs[pages_per_step + 6:]
    stage = (bufs[0:4], bufs[4:8])
    i = pl.program_id(2)
    blk = MOBA_BLOCK
    last = n_blocks - 1
    pairs = n_blocks // 2
    qscale = HEAD_DIM ** -0.5 * LOG2_E
    sga_refs = (sgaa_ref, sgab_ref)

    def past_slot(t, pair):
        first = t < pair
        return jnp.where(first, 1.0, 0.0), jnp.where(first, 0, 1), jnp.where(first, t, t - pair)

    def cache_block_mean(a):
        tot = jnp.sum(page_refs[PAGES_PER_BLOCK * a][0], axis=0)
        for r in range(1, PAGES_PER_BLOCK):
            tot = tot + jnp.sum(page_refs[PAGES_PER_BLOCK * a + r][0], axis=0)
        pmean_ref[0, a] = tot * (1.0 / MOBA_BLOCK)

    def score_pass(pair, q_refs, buf, with_cache_stream):
        qt_ref, bias_ref, m_ref, s_ref = buf
        own = (pair, last - pair)
        for w in range(2):
            qf_t = q_refs[w][...].T
            qt_ref[w] = (qf_t * qscale).astype(BF16)
            gate = jnp.dot(kmean_ref[...], qf_t, preferred_element_type=F32, precision=lax.Precision.HIGHEST)
            row = lax.broadcasted_iota(jnp.int32, gate.shape, 0)
            gate = jnp.where(row < own[w], gate, -jnp.inf)
            sel = _topk_mask(gate, row, own[w], axis=0)
            bias_ref[w] = jnp.where(sel > 0.5, 0.0, NEG)
        k_id = lax.broadcasted_iota(jnp.int32, (blk, blk), 0)
        q_id = lax.broadcasted_iota(jnp.int32, (blk, blk), 1)
        m = []
        for w in range(2):
            s = jnp.dot(kb_ref[own[w]], qt_ref[w], preferred_element_type=F32)
            s = jnp.where(k_id <= q_id, s, NEG)
            s_ref[w] = s
            m.append(jnp.max(s, axis=0, keepdims=True))
        for t in range(last):
            fa, w, kblk = past_slot(t, pair)
            s = jnp.dot(kb_ref[kblk], qt_ref[w], preferred_element_type=F32)
            s_ref[2 + t] = s
            c = jnp.max(s, axis=0, keepdims=True) + bias_ref[w, pl.ds(kblk, 1), :]
            m[0] = jnp.maximum(m[0], c + (fa - 1.0) * -NEG)
            m[1] = jnp.maximum(m[1], c + fa * NEG)
            if with_cache_stream and t < pages_per_step // PAGES_PER_BLOCK:
                cache_block_mean(t)
        for w in range(2):
            m_ref[w] = m[w]

    def value_pass(pair, buf):
        _, bias_ref, m_ref, s_ref = buf
        own = (pair, last - pair)
        m = [m_ref[0], m_ref[1]]
        for w in range(2):
            p = jnp.exp2((s_ref[w] - m[w]).astype(BF16))
            acc_ref[w] = jnp.dot(vt_ref[own[w]], p, preferred_element_type=F32)
        for t in range(last):
            fa, w, kblk = past_slot(t, pair)
            shift = bias_ref[w, pl.ds(kblk, 1), :] - (m[1] + fa * (m[0] - m[1]))
            p = jnp.exp2((s_ref[2 + t] + shift).astype(BF16))
            acc_ref[2 + t] = jnp.dot(vt_ref[kblk], p, preferred_element_type=F32)
        bounds = ((0, pair), (pair, last))
        for w in range(2):
            acc = lax.fori_loop(bounds[w][0], bounds[w][1], lambda t, a: a + acc_ref[2 + t], acc_ref[w])
            out_t = acc[0:HEAD_DIM, :] / acc[HEAD_DIM:HEAD_DIM + 1, :]
            o_ref[0, 0, w] = (out_t.T * sga_refs[w][...]).astype(o_ref.dtype)

    @pl.when(i == 0)
    def _():
        kmean_ref[...] = jnp.mean(k_ref[...].reshape(n_blocks, blk, HEAD_DIM), axis=1)
        for jb in range(n_blocks):
            rows = slice(jb * blk, (jb + 1) * blk)
            kb_ref[jb] = k_ref[rows, :].astype(BF16)
            vt_ref[jb, 0:HEAD_DIM, :] = v_ref[rows, :].T.astype(BF16)
            vt_ref[jb, HEAD_DIM:, :] = jnp.ones((SUM_ROWS, blk), BF16)
        score_pass(0, (qa_ref, qb_ref), stage[0], False)

    nxt = jnp.minimum(i + 1, pairs - 1)
    for parity in range(2):
        @pl.when(i % 2 == parity)
        def _():
            score_pass(nxt, (qna_ref, qnb_ref), stage[1 - parity], True)
            value_pass(i, stage[parity])


def _moba_prompt(q, k, v, sga, batch, seq, heads, page_table_flat, cache, n_seq, n_pages):
    n_blocks = seq // MOBA_BLOCK
    assert n_blocks % 2 == 0
    last = n_blocks - 1
    pairs = n_blocks // 2
    n_phys, page, c_heads, hd = cache.shape
    steps = batch * heads * pairs
    pages_per_step = n_seq * n_pages // steps
    parts = n_pages // pages_per_step
    assert pages_per_step * steps == n_seq * n_pages and parts * pages_per_step == n_pages
    blocks_per_step = pages_per_step // PAGES_PER_BLOCK
    assert blocks_per_step * PAGES_PER_BLOCK == pages_per_step and blocks_per_step <= last

    def step_id(b, h, i):
        return (b * heads + h) * pairs + i

    def page_spec(r):
        return pl.BlockSpec((1, page, c_heads, hd),
                            lambda b, h, i, pt: (pt[step_id(b, h, i) * pages_per_step + r], 0, 0, 0))

    qa = pl.BlockSpec((MOBA_BLOCK, HEAD_DIM), lambda b, h, i, pt: (b * n_blocks + i, h))
    qb = pl.BlockSpec((MOBA_BLOCK, HEAD_DIM), lambda b, h, i, pt: (b * n_blocks + last - i, h))
    qna = pl.BlockSpec((MOBA_BLOCK, HEAD_DIM), lambda b, h, i, pt: (b * n_blocks + jnp.minimum(i + 1, pairs - 1), h))
    qnb = pl.BlockSpec((MOBA_BLOCK, HEAD_DIM),
                       lambda b, h, i, pt: (b * n_blocks + last - jnp.minimum(i + 1, pairs - 1), h))
    score_stage = [pltpu.VMEM((2, HEAD_DIM, MOBA_BLOCK), BF16),
                   pltpu.VMEM((2, n_blocks, MOBA_BLOCK), F32),
                   pltpu.VMEM((2, 1, MOBA_BLOCK), F32),
                   pltpu.VMEM((n_blocks + 1, MOBA_BLOCK, MOBA_BLOCK), F32)]
    kvspec = pl.BlockSpec((seq, HEAD_DIM), lambda b, h, i, pt: (b, h))
    return pl.pallas_call(
        functools.partial(_moba_prompt_kernel, n_blocks=n_blocks, pages_per_step=pages_per_step),
        grid_spec=pltpu.PrefetchScalarGridSpec(
            num_scalar_prefetch=1,
            grid=(batch, heads, pairs),
            in_specs=[qa, qb, qna, qnb, kvspec, kvspec, qa, qb] + [page_spec(r) for r in range(pages_per_step)],
            out_specs=[
                pl.BlockSpec((1, 1, 2, MOBA_BLOCK, HEAD_DIM), lambda b, h, i, pt: (b, i, 0, 0, h)),
                pl.BlockSpec((1, blocks_per_step, c_heads, hd),
                             lambda b, h, i, pt: (step_id(b, h, i) // parts, step_id(b, h, i) % parts, 0, 0)),
            ],
            scratch_shapes=[pltpu.VMEM((n_blocks, MOBA_BLOCK, HEAD_DIM), BF16),
                            pltpu.VMEM((n_blocks, HEAD_DIM + SUM_ROWS, MOBA_BLOCK), BF16),
                            pltpu.VMEM((n_blocks, HEAD_DIM), F32),
                            pltpu.VMEM((n_blocks + 1, HEAD_DIM + SUM_ROWS, MOBA_BLOCK), F32)] + score_stage * 2,
        ),
        out_shape=[jax.ShapeDtypeStruct((batch, pairs, 2, MOBA_BLOCK, q.shape[1]), BF16),
                   jax.ShapeDtypeStruct((n_seq, n_pages // PAGES_PER_BLOCK, c_heads, hd), F32)],
        compiler_params=pltpu.CompilerParams(
            dimension_semantics=("arbitrary", "arbitrary", "arbitrary"), vmem_limit_bytes=VMEM_LIMIT),
        name="moba_prompt",
    )(page_table_flat, q, q, q, q, k, v, sga, sga, *([cache] * pages_per_step))


def _conv_tail(c, sgc, gcn_ref, bcn_ref, wpw_ref, bpw_ref):
    mu = jnp.mean(c, axis=-1, keepdims=True)
    d = c - mu
    var = jnp.mean(d * d, axis=-1, keepdims=True)
    y = d * lax.rsqrt(var + LN_EPS) * gcn_ref[...] + bcn_ref[...]
    y = _silu(y)
    return (jnp.dot(y.astype(BF16), wpw_ref[...], preferred_element_type=F32) + bpw_ref[...]) * sgc


def _conv_prompt_kernel(u_ref, prev_ref, sgc_ref, wdw_ref, bdw_ref, gcn_ref, bcn_ref, wpw_ref, bpw_ref,
                        o_ref, ext_ref, sh_ref, c_ref, *, tiles_per_seq, rows):
    i = pl.program_id(0)
    t, w = u_ref.shape
    first = (i % tiles_per_seq) == 0
    ext_ref[0:CONV_HALO, :] = jnp.where(first, 0.0, prev_ref[...])
    ext_ref[CONV_HALO:, :] = u_ref[...]
    n_sh = sh_ref.shape[1]
    for r in range(1, SUBLANES):
        sh_ref[r - 1] = ext_ref[r:r + n_sh, :]
    off = CONV_HALO - (CONV_K - 1)
    for c in range(w // LANES):
        cs = slice(c * LANES, (c + 1) * LANES)
        for rc in range(t // rows):
            acc = jnp.broadcast_to(bdw_ref[:, cs], (rows, LANES))
            for j in range(CONV_K):
                a, r = divmod(off + j, SUBLANES)
                lo = rc * rows + a * SUBLANES
                win = ext_ref[lo:lo + rows, cs] if r == 0 else sh_ref[r - 1, lo:lo + rows, cs]
                acc = acc + win * wdw_ref[j:j + 1, cs]
            c_ref[rc * rows:(rc + 1) * rows, cs] = acc
    o_ref[...] = _conv_tail(c_ref[...], sgc_ref[...], gcn_ref, bcn_ref, wpw_ref, bpw_ref).astype(o_ref.dtype)


def _conv_prompt(u, sgc, w_dw, b_dw, g_cn, b_cn, w_pw_b, b_pw, seq, tile):
    n, w = u.shape
    halo_per_tile = tile // CONV_HALO
    full = lambda shape: pl.BlockSpec(shape, lambda i: (0, 0))
    rowspec = pl.BlockSpec((tile, w), lambda i: (i, 0))
    return pl.pallas_call(
        functools.partial(_conv_prompt_kernel, tiles_per_seq=seq // tile, rows=64),
        grid=(n // tile,),
        in_specs=[
            rowspec,
            pl.BlockSpec((CONV_HALO, w), lambda i: (jnp.maximum(i * halo_per_tile - 1, 0), 0)),
            rowspec,
            full((CONV_K, w)), full((1, w)), full((1, w)), full((1, w)), full((w, w)), full((1, w)),
        ],
        out_specs=rowspec,
        out_shape=jax.ShapeDtypeStruct((n, w), BF16),
        scratch_shapes=[pltpu.VMEM((tile + CONV_HALO, w), F32),
                        pltpu.VMEM((SUBLANES - 1, tile + CONV_HALO - SUBLANES, w), F32),
                        pltpu.VMEM((tile, w), F32)],
        compiler_params=pltpu.CompilerParams(dimension_semantics=("arbitrary",), vmem_limit_bytes=VMEM_LIMIT),
        name="conv_prompt",
    )(u, u, sgc, w_dw, b_dw, g_cn, b_cn, w_pw_b, b_pw)


def _conv_sample_kernel(state_ref, u_ref, sgc_ref, wdw_ref, bdw_ref, gcn_ref, bcn_ref, wpw_ref, bpw_ref,
                        o_ref, new_ref, c_ref):
    nb = state_ref.shape[0]
    hist = CONV_K - 1
    for b in range(nb):
        st = state_ref[b]
        u_row = u_ref[b:b + 1, :]
        c_ref[b:b + 1, :] = (jnp.sum(st * wdw_ref[0:hist, :], axis=0, keepdims=True)
                             + u_row * wdw_ref[hist:hist + 1, :] + bdw_ref[...])
        new_ref[b, 0:hist - 1, :] = state_ref[b, 1:hist, :]
        new_ref[b, hist - 1:hist, :] = u_row
    o_ref[...] = _conv_tail(c_ref[...], sgc_ref[...], gcn_ref, bcn_ref, wpw_ref, bpw_ref).astype(o_ref.dtype)


def _conv_sample(state, u, sgc, w_dw, b_dw, g_cn, b_cn, w_pw_b, b_pw):
    nb, hist, w = state.shape
    full2 = lambda shape: pl.BlockSpec(shape, lambda i: (0, 0))
    full3 = lambda shape: pl.BlockSpec(shape, lambda i: (0, 0, 0))
    return pl.pallas_call(
        _conv_sample_kernel,
        grid=(1,),
        in_specs=[full3((nb, hist, w)), full2((nb, w)), full2((nb, w)),
                  full2((CONV_K, w)), full2((1, w)), full2((1, w)), full2((1, w)), full2((w, w)), full2((1, w))],
        out_specs=[full2((nb, w)), full3((nb, hist, w))],
        out_shape=[jax.ShapeDtypeStruct((nb, w), BF16), jax.ShapeDtypeStruct((nb, hist, w), F32)],
        scratch_shapes=[pltpu.VMEM((nb, w), F32)],
        compiler_params=pltpu.CompilerParams(dimension_semantics=("arbitrary",), vmem_limit_bytes=VMEM_LIMIT),
        name="conv_sample",
    )(state, u, sgc, w_dw, b_dw, g_cn, b_cn, w_pw_b, b_pw)


def _out_kernel(x_ref, a_ref, c_ref, p_ref, woa_ref, woc_ref, bo_ref, g_ref, b_ref, wpg_ref, bpg_ref, wpe_ref,
                o_ref, *, alpha):
    attn = a_ref[...].reshape(c_ref.shape).astype(BF16)
    mix = (jnp.dot(attn, woa_ref[...], preferred_element_type=F32)
           + jnp.dot(c_ref[...].astype(BF16), woc_ref[...], preferred_element_type=F32) + bo_ref[...])
    t = alpha * x_ref[...] + mix
    mu = jnp.mean(t, axis=-1, keepdims=True)
    d = t - mu
    var = jnp.mean(d * d, axis=-1, keepdims=True)
    h = d * lax.rsqrt(var + LN_EPS) * g_ref[...] + b_ref[...]
    gate = _sigmoid(jnp.dot(h.astype(BF16), wpg_ref[...], preferred_element_type=F32) + bpg_ref[...])
    pe = jnp.dot(p_ref[...].astype(BF16), wpe_ref[...], preferred_element_type=F32)
    o_ref[...] = h + gate * pe


def _out_proj(x, attn, attn_spec, conv, p, w_out_b, b_out, g_ln, b_ln, w_pg_b, b_pg, w_pe_b, alpha, tm):
    n, d = x.shape
    half = conv.shape[1]
    pd = p.shape[1]
    const = lambda shape, r=0: pl.BlockSpec(shape, lambda i: (r, 0), pipeline_mode=pl.Buffered(1))
    row = lambda w: pl.BlockSpec((tm, w), lambda i: (i, 0))
    return pl.pallas_call(
        functools.partial(_out_kernel, alpha=alpha),
        grid=(n // tm,),
        in_specs=[row(d), attn_spec, row(half), row(pd),
                  const((half, d), 0), const((half, d), 1), const((1, d)), const((1, d)), const((1, d)),
                  const((d, d)), const((1, d)), const((pd, d))],
        out_specs=row(d),
        out_shape=jax.ShapeDtypeStruct((n, d), F32),
        compiler_params=pltpu.CompilerParams(dimension_semantics=("arbitrary",), vmem_limit_bytes=VMEM_LIMIT),
        name="out_proj",
    )(x, attn, conv, p, w_out_b, w_out_b, b_out, g_ln, b_ln, w_pg_b, b_pg, w_pe_b)


def _select_kernel(q_ref, kmean_ref, o_ref):
    g = jnp.sum(kmean_ref[0] * q_ref[...], axis=-1, keepdims=True)
    n_blocks = g.shape[0]
    idx = lax.broadcasted_iota(jnp.int32, g.shape, 0)
    for t in range(MOBA_TOPK):
        m = jnp.max(g, axis=0, keepdims=True)
        first = jnp.min(jnp.where(g == m, idx, n_blocks), axis=0, keepdims=True)
        o_ref[0, t] = jnp.broadcast_to(first[0], o_ref.shape[2:])
        g = jnp.where(idx == first, -jnp.inf, g)


def _select_sample(q4, kmean):
    n_seq, n_blocks, heads, hd = kmean.shape
    assert n_blocks >= MOBA_TOPK
    return pl.pallas_call(
        _select_kernel,
        grid=(n_seq,),
        in_specs=[pl.BlockSpec((1, heads, hd), lambda b: (b, 0, 0)),
                  pl.BlockSpec((1, n_blocks, heads, hd), lambda b: (b, 0, 0, 0))],
        out_specs=pl.BlockSpec((1, MOBA_TOPK, heads, LANES), lambda b: (b, 0, 0, 0)),
        out_shape=jax.ShapeDtypeStruct((n_seq, MOBA_TOPK, heads, LANES), jnp.int32),
        compiler_params=pltpu.CompilerParams(dimension_semantics=("arbitrary",), vmem_limit_bytes=VMEM_LIMIT),
        name="select_sample",
    )(q4, kmean)


def _moba_sample_kernel(sel_ref, pt_ref, q_ref, kn_ref, vn_ref, sga_ref, ck_ref, cv_ref, o_ref, kbuf, vbuf, sems,
                        *, n_seq, n_pages, heads):
    b = pl.program_id(0)
    n_sel = MOBA_TOPK * PAGES_PER_BLOCK
    scale = HEAD_DIM ** -0.5

    def slab_copies(seq, slot):
        copies = []
        for h in range(heads):
            for t in range(MOBA_TOPK):
                blk = sel_ref[(seq * MOBA_TOPK + t) * heads + h]
                for r in range(PAGES_PER_BLOCK):
                    page = pt_ref[seq * n_pages + blk * PAGES_PER_BLOCK + r]
                    dst = t * PAGES_PER_BLOCK + r
                    copies.append(pltpu.make_async_copy(
                        ck_ref.at[page, :, h, :], kbuf.at[slot, h, dst], sems.at[slot, 0, h]))
                    copies.append(pltpu.make_async_copy(
                        cv_ref.at[page, :, h, :], vbuf.at[slot, h, dst], sems.at[slot, 1, h]))
        return copies

    slot = b % 2

    @pl.when(b == 0)
    def _():
        for c in slab_copies(b, slot):
            c.start()

    @pl.when(b + 1 < n_seq)
    def _():
        for c in slab_copies(b + 1, 1 - slot):
            c.start()

    for c in slab_copies(b, slot):
        c.wait()

    for h in range(heads):
        hs = slice(h, h + 1)
        qh = q_ref[0, hs, :]
        kh = kbuf[slot, h].reshape(n_sel * PAGE_SIZE, HEAD_DIM).astype(BF16)
        vh = vbuf[slot, h].reshape(n_sel * PAGE_SIZE, HEAD_DIM).astype(BF16)
        s = _dot_nt(qh.astype(BF16), kh) * scale
        s_new = jnp.sum(qh * kn_ref[0, hs, :], axis=1, keepdims=True) * scale
        m = jnp.maximum(jnp.max(s, axis=1, keepdims=True), s_new)
        p = jnp.exp(s - m)
        p_new = jnp.exp(s_new - m)
        l = jnp.sum(p, axis=1, keepdims=True) + p_new
        acc = jnp.dot(p.astype(BF16), vh, preferred_element_type=F32) + p_new * vn_ref[0, hs, :]
        o_ref[0, hs, :] = acc / l * sga_ref[0, hs, :]


def _moba_sample(sel_flat, page_table_flat, q4, kn4, vn4, sga4, cache_k, cache_v, n_pages):
    n_seq, heads, hd = q4.shape
    n_sel = MOBA_TOPK * PAGES_PER_BLOCK
    vec = pl.BlockSpec((1, heads, hd), lambda b, sel, pt: (b, 0, 0))
    hbm = pl.BlockSpec(memory_space=pl.ANY)
    return pl.pallas_call(
        functools.partial(_moba_sample_kernel, n_seq=n_seq, n_pages=n_pages, heads=heads),
        grid_spec=pltpu.PrefetchScalarGridSpec(
            num_scalar_prefetch=2,
            grid=(n_seq,),
            in_specs=[vec, vec, vec, vec, hbm, hbm],
            out_specs=vec,
            scratch_shapes=[pltpu.VMEM((2, heads, n_sel, PAGE_SIZE, hd), F32),
                            pltpu.VMEM((2, heads, n_sel, PAGE_SIZE, hd), F32),
                            pltpu.SemaphoreType.DMA((2, 2, heads))],
        ),
        out_shape=jax.ShapeDtypeStruct(q4.shape, F32),
        compiler_params=pltpu.CompilerParams(dimension_semantics=("arbitrary",), vmem_limit_bytes=VMEM_LIMIT),
        name="moba_sample",
    )(sel_flat, page_table_flat, q4, kn4, vn4, sga4, cache_k, cache_v)


def kernel(x_prompt, x_sample, p_prompt, p_sample, cache_k, cache_v, state_conv, page_table, w_in, b_in, w_dw,
           b_dw, g_cn, b_cn, w_pw, b_pw, w_out, b_out, g_ln, b_ln, w_pe, w_pg, b_pg):
    depth = w_in.shape[0]
    assert depth == 1
    batch, seq, d_model = x_prompt.shape
    n_seq, dec_seq, _ = x_sample.shape
    assert dec_seq == 1
    n_phys, page, heads, head_dim = cache_k.shape[1:]
    assert head_dim == HEAD_DIM and page == PAGE_SIZE
    attn_w = heads * head_dim
    conv_w = w_pw.shape[1]
    assert conv_w == attn_w and w_in.shape[2] == 7 * attn_w
    n_pages = page_table.shape[1]
    hist = CONV_K - 1
    alpha = (2 * depth) ** 0.25
    n = batch * seq

    row2 = lambda a: a.reshape(1, -1)
    w_in_b = w_in[0].astype(BF16)
    w_pw_b = w_pw[0].astype(BF16)
    w_out_b = w_out[0].astype(BF16)
    w_pg_b = w_pg[0].astype(BF16)
    w_pe_b = w_pe[0].astype(BF16)
    b_in2, b_dw2, g_cn2, b_cn2, b_pw2 = row2(b_in[0]), row2(b_dw[0]), row2(g_cn[0]), row2(b_cn[0]), row2(b_pw[0])
    b_out2, g_ln2, b_ln2, b_pg2 = row2(b_out[0]), row2(g_ln[0]), row2(b_ln[0]), row2(b_pg[0])
    conv_w_args = (w_dw[0], b_dw2, g_cn2, b_cn2, w_pw_b, b_pw2)
    out_w_args = (w_out_b, b_out2, g_ln2, b_ln2, w_pg_b, b_pg2, w_pe_b, alpha)

    xp = x_prompt.reshape(n, d_model)
    q, k, v, sga, u, sgc = _in_proj(xp, w_in_b, b_in2, tm=512)
    pt_flat = page_table.reshape(-1)
    ck = cache_k.reshape(n_phys, page, heads, head_dim)
    cv = cache_v.reshape(n_phys, page, heads, head_dim)
    attn, kmean = _moba_prompt(q, k, v, sga, batch, seq, heads, pt_flat, ck, n_seq, n_pages)
    conv = _conv_prompt(u, sgc, *conv_w_args, seq=seq, tile=MOBA_BLOCK)
    n_blocks = seq // MOBA_BLOCK

    def paired_rows(r):
        blk = r % n_blocks
        mirror = n_blocks - 1 - blk
        return (r // n_blocks, jnp.minimum(blk, mirror), jnp.where(blk > mirror, 1, 0), 0, 0)

    y_prompt = _out_proj(xp, attn, pl.BlockSpec((1, 1, 1, MOBA_BLOCK, attn_w), paired_rows), conv,
                         p_prompt[0].reshape(n, -1), *out_w_args, tm=MOBA_BLOCK)
    conv_prompt_new = u.reshape(batch, seq, conv_w)[:, seq - hist:, :]

    xs = x_sample.reshape(n_seq, d_model)
    qs, ks, vs, sgas, us, sgcs = _in_proj(xs, w_in_b, b_in2, tm=n_seq)
    conv_s, conv_sample_new = _conv_sample(state_conv[0], us, sgcs, *conv_w_args)
    as4 = lambda a: a.reshape(n_seq, heads, head_dim)
    sel = _select_sample(as4(qs), kmean)[:, :, :, 0]
    attn_s = _moba_sample(sel.reshape(-1), pt_flat, as4(qs), as4(ks), as4(vs), as4(sgas), ck, cv, n_pages)
    y_sample = _out_proj(xs, attn_s.reshape(n_seq, attn_w), pl.BlockSpec((n_seq, attn_w), lambda r: (r, 0)), conv_s,
                         p_sample[0].reshape(n_seq, -1), *out_w_args, tm=n_seq)

    kv_shape = (depth, batch, seq, heads, head_dim)
    kvs_shape = (depth, n_seq, dec_seq, heads, head_dim)
    return (y_prompt.reshape(batch, seq, d_model), y_sample.reshape(n_seq, dec_seq, d_model),
            k.reshape(kv_shape), v.reshape(kv_shape), conv_prompt_new.reshape(depth, batch, hist, conv_w),
            ks.reshape(kvs_shape), vs.reshape(kvs_shape), conv_sample_new.reshape(depth, n_seq, hist, conv_w))
```

```python
import functools

import jax
import jax.numpy as jnp
from jax import lax
from jax.experimental import pallas as pl
from jax.experimental.pallas import tpu as pltpu

F32 = jnp.float32
BF16 = jnp.bfloat16

HEAD_DIM = 128
CONV_K = 31
MOBA_BLOCK = 256
MOBA_TOPK = 3
PAGE_SIZE = 128
PAGES_PER_BLOCK = MOBA_BLOCK // PAGE_SIZE
LN_EPS = 1e-5
NEG = -1e30
LOG2_E = 1.4426950408889634
LANES = 128
SUBLANES = 8
SUM_ROWS = 16
CONV_HALO = 32
VMEM_LIMIT = 60 * 1024 * 1024


def _sigmoid(z):
    return 0.5 * jnp.tanh(0.5 * z) + 0.5


def _silu(z):
    return z * _sigmoid(z)


def _dot_nt(a, b, **kw):
    return lax.dot_general(a, b, (((1,), (1,)), ((), ())), preferred_element_type=F32, **kw)


def _in_proj_kernel(x_ref, w_ref, b_ref, *refs, finishers, n_out):
    out_refs, (xb_ref, *hold) = refs[:n_out], refs[n_out:]
    j = pl.program_id(1)

    @pl.when(j == 0)
    def _():
        xb_ref[...] = x_ref[...].astype(BF16)

    for g, finish in enumerate(finishers):
        @pl.when(j == g)
        def _(finish=finish):
            finish(jnp.dot(xb_ref[...], w_ref[...], preferred_element_type=F32) + b_ref[...], out_refs, *hold)


def _emit(k, fn=lambda z: z):
    def finish(z, out_refs, *hold):
        out_refs[k][...] = fn(z).astype(out_refs[k].dtype)
    return finish


def _hold(z, out_refs, hold_ref):
    hold_ref[...] = z


def _glu_into(k):
    def finish(z, out_refs, hold_ref):
        out_refs[k][...] = hold_ref[...] * _sigmoid(z)
    return finish


QKV_GROUPS = (_emit(0), _emit(1), _emit(2))
GATE_GROUPS = (_emit(0, _silu), _hold, _glu_into(1), _emit(2, _silu))


def _in_proj(x, w_b, b, tm, first_group, finishers, out_dtypes, needs_hold):
    n, d = x.shape
    gw = w_b.shape[1] // 7
    row = pl.BlockSpec((tm, gw), lambda i, j: (i, 0))
    return pl.pallas_call(
        functools.partial(_in_proj_kernel, finishers=finishers, n_out=len(out_dtypes)),
        grid=(n // tm, len(finishers)),
        in_specs=[
            pl.BlockSpec((tm, d), lambda i, j: (i, 0)),
            pl.BlockSpec((d, gw), lambda i, j: (0, first_group + j)),
            pl.BlockSpec((1, gw), lambda i, j: (0, first_group + j)),
        ],
        out_specs=[row] * len(out_dtypes),
        out_shape=[jax.ShapeDtypeStruct((n, gw), dt) for dt in out_dtypes],
        scratch_shapes=[pltpu.VMEM((tm, d), BF16)] + ([pltpu.VMEM((tm, gw), F32)] if needs_hold else []),
        compiler_params=pltpu.CompilerParams(
            dimension_semantics=("arbitrary", "arbitrary"), vmem_limit_bytes=VMEM_LIMIT),
        name="in_proj",
    )(x, w_b, b)


def _in_proj_all(x, w_b, b, tm):
    q, k, v = _in_proj(x, w_b, b, tm, 0, QKV_GROUPS, (F32, F32, F32), False)
    sga, u, sgc = _in_proj(x, w_b, b, tm, len(QKV_GROUPS), GATE_GROUPS, (BF16, F32, BF16), True)
    return q, k, v, sga, u, sgc


def _topk_mask(g, idx, n_valid, axis):
    sel = jnp.zeros(g.shape, F32)
    big = g.shape[axis]
    for t in range(MOBA_TOPK):
        m = jnp.max(g, axis=axis, keepdims=True)
        first = jnp.min(jnp.where(g == m, idx, big), axis=axis, keepdims=True)
        hit = idx == first
        counts = jnp.where(t < n_valid, 1.0, 0.0)
        sel = jnp.maximum(sel, jnp.where(hit, counts, 0.0))
        g = jnp.where(hit, -jnp.inf, g)
    return sel


def _moba_prompt_kernel(pt_ref, qa_ref, qb_ref, qna_ref, qnb_ref, k_ref, v_ref, sgaa_ref, sgab_ref, *refs,
                        n_blocks, pages_per_step):
    del pt_ref
    page_refs = refs[:pages_per_step]
    o_ref, pmean_ref, kb_ref, vt_ref, kmean_ref, acc_ref = refs[pages_per_step:pages_per_step + 6]
    bufs = refs[pages_per_step + 6:]
    stage = (bufs[0:4], bufs[4:8])
    i = pl.program_id(2)
    blk = MOBA_BLOCK
    last = n_blocks - 1
    pairs = n_blocks // 2
    qscale = HEAD_DIM ** -0.5 * LOG2_E
    sga_refs = (sgaa_ref, sgab_ref)

    def past_slot(t, pair):
        first = t < pair
        return jnp.where(first, 1.0, 0.0), jnp.where(first, 0, 1), jnp.where(first, t, t - pair)

    def cache_block_mean(a):
        tot = jnp.sum(page_refs[PAGES_PER_BLOCK * a][0], axis=0)
        for r in range(1, PAGES_PER_BLOCK):
            tot = tot + jnp.sum(page_refs[PAGES_PER_BLOCK * a + r][0], axis=0)
        pmean_ref[0, a] = tot * (1.0 / MOBA_BLOCK)

    def score_pass(pair, q_refs, buf, with_cache_stream):
        qt_ref, bias_ref, m_ref, s_ref = buf
        own = (pair, last - pair)
        for w in range(2):
            qf_t = q_refs[w][...].T
            qt_ref[w] = (qf_t * qscale).astype(BF16)
            gate = jnp.dot(kmean_ref[...], qf_t, preferred_element_type=F32, precision=lax.Precision.HIGHEST)
            row = lax.broadcasted_iota(jnp.int32, gate.shape, 0)
            gate = jnp.where(row < own[w], gate, -jnp.inf)
            sel = _topk_mask(gate, row, own[w], axis=0)
            bias_ref[w] = jnp.where(sel > 0.5, 0.0, NEG)
        k_id = lax.broadcasted_iota(jnp.int32, (blk, blk), 0)
        q_id = lax.broadcasted_iota(jnp.int32, (blk, blk), 1)
        m = []
        for w in range(2):
            s = jnp.dot(kb_ref[own[w]], qt_ref[w], preferred_element_type=F32)
            s = jnp.where(k_id <= q_id, s, NEG)
            s_ref[w] = s
            m.append(jnp.max(s, axis=0, keepdims=True))
        for t in range(last):
            fa, w, kblk = past_slot(t, pair)
            s = jnp.dot(kb_ref[kblk], qt_ref[w], preferred_element_type=F32)
            s_ref[2 + t] = s
            c = jnp.max(s, axis=0, keepdims=True) + bias_ref[w, pl.ds(kblk, 1), :]
            m[0] = jnp.maximum(m[0], c + (fa - 1.0) * -NEG)
            m[1] = jnp.maximum(m[1], c + fa * NEG)
            if with_cache_stream and t < pages_per_step // PAGES_PER_BLOCK:
                cache_block_mean(t)
        for w in range(2):
            m_ref[w] = m[w]

    def value_pass(pair, buf):
        _, bias_ref, m_ref, s_ref = buf
        own = (pair, last - pair)
        m = [m_ref[0], m_ref[1]]
        for w in range(2):
            p = jnp.exp2((s_ref[w] - m[w]).astype(BF16))
            acc_ref[w] = jnp.dot(vt_ref[own[w]], p, preferred_element_type=F32)
        for t in range(last):
            fa, w, kblk = past_slot(t, pair)
            shift = bias_ref[w, pl.ds(kblk, 1), :] - (m[1] + fa * (m[0] - m[1]))
            p = jnp.exp2((s_ref[2 + t] + shift).astype(BF16))
            acc_ref[2 + t] = jnp.dot(vt_ref[kblk], p, preferred_element_type=F32)
        bounds = ((0, pair), (pair, last))
        for w in range(2):
            acc = lax.fori_loop(bounds[w][0], bounds[w][1], lambda t, a: a + acc_ref[2 + t], acc_ref[w])
            out_t = acc[0:HEAD_DIM, :] / acc[HEAD_DIM:HEAD_DIM + 1, :]
            o_ref[0, 0, w] = (out_t.T * sga_refs[w][...]).astype(o_ref.dtype)

    @pl.when(i == 0)
    def _():
        kmean_ref[...] = jnp.mean(k_ref[...].reshape(n_blocks, blk, HEAD_DIM), axis=1)
        for jb in range(n_blocks):
            rows = slice(jb * blk, (jb + 1) * blk)
            kb_ref[jb] = k_ref[rows, :].astype(BF16)
            vt_ref[jb, 0:HEAD_DIM, :] = v_ref[rows, :].T.astype(BF16)
            vt_ref[jb, HEAD_DIM:, :] = jnp.ones((SUM_ROWS, blk), BF16)
        score_pass(0, (qa_ref, qb_ref), stage[0], False)

    nxt = jnp.minimum(i + 1, pairs - 1)
    for parity in range(2):
        @pl.when(i % 2 == parity)
        def _():
            score_pass(nxt, (qna_ref, qnb_ref), stage[1 - parity], True)
            value_pass(i, stage[parity])


def _moba_prompt(q, k, v, sga, batch, seq, heads, page_table_flat, cache, n_seq, n_pages):
    n_blocks = seq // MOBA_BLOCK
    assert n_blocks % 2 == 0
    last = n_blocks - 1
    pairs = n_blocks // 2
    n_phys, page, c_heads, hd = cache.shape
    steps = batch * heads * pairs
    pages_per_step = n_seq * n_pages // steps
    parts = n_pages // pages_per_step
    assert pages_per_step * steps == n_seq * n_pages and parts * pages_per_step == n_pages
    blocks_per_step = pages_per_step // PAGES_PER_BLOCK
    assert blocks_per_step * PAGES_PER_BLOCK == pages_per_step and blocks_per_step <= last

    def step_id(b, h, i):
        return (b * heads + h) * pairs + i

    def page_spec(r):
        return pl.BlockSpec((1, page, c_heads, hd),
                            lambda b, h, i, pt: (pt[step_id(b, h, i) * pages_per_step + r], 0, 0, 0))

    qa = pl.BlockSpec((MOBA_BLOCK, HEAD_DIM), lambda b, h, i, pt: (b * n_blocks + i, h))
    qb = pl.BlockSpec((MOBA_BLOCK, HEAD_DIM), lambda b, h, i, pt: (b * n_blocks + last - i, h))
    qna = pl.BlockSpec((MOBA_BLOCK, HEAD_DIM), lambda b, h, i, pt: (b * n_blocks + jnp.minimum(i + 1, pairs - 1), h))
    qnb = pl.BlockSpec((MOBA_BLOCK, HEAD_DIM),
                       lambda b, h, i, pt: (b * n_blocks + last - jnp.minimum(i + 1, pairs - 1), h))
    score_stage = [pltpu.VMEM((2, HEAD_DIM, MOBA_BLOCK), BF16),
                   pltpu.VMEM((2, n_blocks, MOBA_BLOCK), F32),
                   pltpu.VMEM((2, 1, MOBA_BLOCK), F32),
                   pltpu.VMEM((n_blocks + 1, MOBA_BLOCK, MOBA_BLOCK), F32)]
    kvspec = pl.BlockSpec((seq, HEAD_DIM), lambda b, h, i, pt: (b, h))
    return pl.pallas_call(
        functools.partial(_moba_prompt_kernel, n_blocks=n_blocks, pages_per_step=pages_per_step),
        grid_spec=pltpu.PrefetchScalarGridSpec(
            num_scalar_prefetch=1,
            grid=(batch, heads, pairs),
            in_specs=[qa, qb, qna, qnb, kvspec, kvspec, qa, qb] + [page_spec(r) for r in range(pages_per_step)],
            out_specs=[
                pl.BlockSpec((1, 1, 2, MOBA_BLOCK, HEAD_DIM), lambda b, h, i, pt: (b, i, 0, 0, h)),
                pl.BlockSpec((1, blocks_per_step, c_heads, hd),
                             lambda b, h, i, pt: (step_id(b, h, i) // parts, step_id(b, h, i) % parts, 0, 0)),
            ],
            scratch_shapes=[pltpu.VMEM((n_blocks, MOBA_BLOCK, HEAD_DIM), BF16),
                            pltpu.VMEM((n_blocks, HEAD_DIM + SUM_ROWS, MOBA_BLOCK), BF16),
                            pltpu.VMEM((n_blocks, HEAD_DIM), F32),
                            pltpu.VMEM((n_blocks + 1, HEAD_DIM + SUM_ROWS, MOBA_BLOCK), F32)] + score_stage * 2,
        ),
        out_shape=[jax.ShapeDtypeStruct((batch, pairs, 2, MOBA_BLOCK, q.shape[1]), BF16),
                   jax.ShapeDtypeStruct((n_seq, n_pages // PAGES_PER_BLOCK, c_heads, hd), F32)],
        compiler_params=pltpu.CompilerParams(
            dimension_semantics=("arbitrary", "arbitrary", "arbitrary"), vmem_limit_bytes=VMEM_LIMIT),
        name="moba_prompt",
    )(page_table_flat, q, q, q, q, k, v, sga, sga, *([cache] * pages_per_step))


def _conv_tail(c, sgc, gcn_ref, bcn_ref, wpw_ref, bpw_ref):
    mu = jnp.mean(c, axis=-1, keepdims=True)
    d = c - mu
    var = jnp.mean(d * d, axis=-1, keepdims=True)
    y = d * lax.rsqrt(var + LN_EPS) * gcn_ref[...] + bcn_ref[...]
    y = _silu(y)
    return (jnp.dot(y.astype(BF16), wpw_ref[...], preferred_element_type=F32) + bpw_ref[...]) * sgc


def _conv_prompt_kernel(u_ref, prev_ref, sgc_ref, wdw_ref, bdw_ref, gcn_ref, bcn_ref, wpw_ref, bpw_ref,
                        o_ref, ext_ref, sh_ref, c_ref, *, tiles_per_seq, rows):
    i = pl.program_id(0)
    t, w = u_ref.shape
    first = (i % tiles_per_seq) == 0
    ext_ref[0:CONV_HALO, :] = jnp.where(first, 0.0, prev_ref[...])
    ext_ref[CONV_HALO:, :] = u_ref[...]
    n_sh = sh_ref.shape[1]
    for r in range(1, SUBLANES):
        sh_ref[r - 1] = ext_ref[r:r + n_sh, :]
    off = CONV_HALO - (CONV_K - 1)
    for c in range(w // LANES):
        cs = slice(c * LANES, (c + 1) * LANES)
        for rc in range(t // rows):
            acc = jnp.broadcast_to(bdw_ref[:, cs], (rows, LANES))
            for j in range(CONV_K):
                a, r = divmod(off + j, SUBLANES)
                lo = rc * rows + a * SUBLANES
                win = ext_ref[lo:lo + rows, cs] if r == 0 else sh_ref[r - 1, lo:lo + rows, cs]
                acc = acc + win * wdw_ref[j:j + 1, cs]
            c_ref[rc * rows:(rc + 1) * rows, cs] = acc
    o_ref[...] = _conv_tail(c_ref[...], sgc_ref[...], gcn_ref, bcn_ref, wpw_ref, bpw_ref).astype(o_ref.dtype)


def _conv_prompt(u, sgc, w_dw, b_dw, g_cn, b_cn, w_pw_b, b_pw, seq, tile):
    n, w = u.shape
    halo_per_tile = tile // CONV_HALO
    full = lambda shape: pl.BlockSpec(shape, lambda i: (0, 0))
    rowspec = pl.BlockSpec((tile, w), lambda i: (i, 0))
    return pl.pallas_call(
        functools.partial(_conv_prompt_kernel, tiles_per_seq=seq // tile, rows=64),
        grid=(n // tile,),
        in_specs=[
            rowspec,
            pl.BlockSpec((CONV_HALO, w), lambda i: (jnp.maximum(i * halo_per_tile - 1, 0), 0)),
            rowspec,
            full((CONV_K, w)), full((1, w)), full((1, w)), full((1, w)), full((w, w)), full((1, w)),
        ],
        out_specs=rowspec,
        out_shape=jax.ShapeDtypeStruct((n, w), BF16),
        scratch_shapes=[pltpu.VMEM((tile + CONV_HALO, w), F32),
                        pltpu.VMEM((SUBLANES - 1, tile + CONV_HALO - SUBLANES, w), F32),
                        pltpu.VMEM((tile, w), F32)],
        compiler_params=pltpu.CompilerParams(dimension_semantics=("arbitrary",), vmem_limit_bytes=VMEM_LIMIT),
        name="conv_prompt",
    )(u, u, sgc, w_dw, b_dw, g_cn, b_cn, w_pw_b, b_pw)


def _conv_sample_kernel(state_ref, u_ref, sgc_ref, wdw_ref, bdw_ref, gcn_ref, bcn_ref, wpw_ref, bpw_ref,
                        o_ref, new_ref, c_ref):
    nb = state_ref.shape[0]
    hist = CONV_K - 1
    for b in range(nb):
        st = state_ref[b]
        u_row = u_ref[b:b + 1, :]
        c_ref[b:b + 1, :] = (jnp.sum(st * wdw_ref[0:hist, :], axis=0, keepdims=True)
                             + u_row * wdw_ref[hist:hist + 1, :] + bdw_ref[...])
        new_ref[b, 0:hist - 1, :] = state_ref[b, 1:hist, :]
        new_ref[b, hist - 1:hist, :] = u_row
    o_ref[...] = _conv_tail(c_ref[...], sgc_ref[...], gcn_ref, bcn_ref, wpw_ref, bpw_ref).astype(o_ref.dtype)


def _conv_sample(state, u, sgc, w_dw, b_dw, g_cn, b_cn, w_pw_b, b_pw):
    nb, hist, w = state.shape
    full2 = lambda shape: pl.BlockSpec(shape, lambda i: (0, 0))
    full3 = lambda shape: pl.BlockSpec(shape, lambda i: (0, 0, 0))
    return pl.pallas_call(
        _conv_sample_kernel,
        grid=(1,),
        in_specs=[full3((nb, hist, w)), full2((nb, w)), full2((nb, w)),
                  full2((CONV_K, w)), full2((1, w)), full2((1, w)), full2((1, w)), full2((w, w)), full2((1, w))],
        out_specs=[full2((nb, w)), full3((nb, hist, w))],
        out_shape=[jax.ShapeDtypeStruct((nb, w), BF16), jax.ShapeDtypeStruct((nb, hist, w), F32)],
        scratch_shapes=[pltpu.VMEM((nb, w), F32)],
        compiler_params=pltpu.CompilerParams(dimension_semantics=("arbitrary",), vmem_limit_bytes=VMEM_LIMIT),
        name="conv_sample",
    )(state, u, sgc, w_dw, b_dw, g_cn, b_cn, w_pw_b, b_pw)


def _out_kernel(x_ref, a_ref, c_ref, p_ref, woa_ref, woc_ref, bo_ref, g_ref, b_ref, wpg_ref, bpg_ref, wpe_ref,
                o_ref, *, alpha):
    attn = a_ref[...].reshape(c_ref.shape).astype(BF16)
    mix = (jnp.dot(attn, woa_ref[...], preferred_element_type=F32)
           + jnp.dot(c_ref[...].astype(BF16), woc_ref[...], preferred_element_type=F32) + bo_ref[...])
    t = alpha * x_ref[...] + mix
    mu = jnp.mean(t, axis=-1, keepdims=True)
    d = t - mu
    var = jnp.mean(d * d, axis=-1, keepdims=True)
    h = d * lax.rsqrt(var + LN_EPS) * g_ref[...] + b_ref[...]
    gate = _sigmoid(jnp.dot(h.astype(BF16), wpg_ref[...], preferred_element_type=F32) + bpg_ref[...])
    pe = jnp.dot(p_ref[...].astype(BF16), wpe_ref[...], preferred_element_type=F32)
    o_ref[...] = h + gate * pe


def _out_proj(x, attn, attn_spec, conv, p, w_out_b, b_out, g_ln, b_ln, w_pg_b, b_pg, w_pe_b, alpha, tm):
    n, d = x.shape
    half = conv.shape[1]
    pd = p.shape[1]
    const = lambda shape, r=0: pl.BlockSpec(shape, lambda i: (r, 0), pipeline_mode=pl.Buffered(1))
    row = lambda w: pl.BlockSpec((tm, w), lambda i: (i, 0))
    return pl.pallas_call(
        functools.partial(_out_kernel, alpha=alpha),
        grid=(n // tm,),
        in_specs=[row(d), attn_spec, row(half), row(pd),
                  const((half, d), 0), const((half, d), 1), const((1, d)), const((1, d)), const((1, d)),
                  const((d, d)), const((1, d)), const((pd, d))],
        out_specs=row(d),
        out_shape=jax.ShapeDtypeStruct((n, d), F32),
        compiler_params=pltpu.CompilerParams(dimension_semantics=("arbitrary",), vmem_limit_bytes=VMEM_LIMIT),
        name="out_proj",
    )(x, attn, conv, p, w_out_b, w_out_b, b_out, g_ln, b_ln, w_pg_b, b_pg, w_pe_b)


def _select_kernel(q_ref, kmean_ref, o_ref):
    g = jnp.sum(kmean_ref[0] * q_ref[...], axis=-1, keepdims=True)
    n_blocks = g.shape[0]
    idx = lax.broadcasted_iota(jnp.int32, g.shape, 0)
    for t in range(MOBA_TOPK):
        m = jnp.max(g, axis=0, keepdims=True)
        first = jnp.min(jnp.where(g == m, idx, n_blocks), axis=0, keepdims=True)
        o_ref[0, t] = jnp.broadcast_to(first[0], o_ref.shape[2:])
        g = jnp.where(idx == first, -jnp.inf, g)


def _select_sample(q4, kmean):
    n_seq, n_blocks, heads, hd = kmean.shape
    assert n_blocks >= MOBA_TOPK
    return pl.pallas_call(
        _select_kernel,
        grid=(n_seq,),
        in_specs=[pl.BlockSpec((1, heads, hd), lambda b: (b, 0, 0)),
                  pl.BlockSpec((1, n_blocks, heads, hd), lambda b: (b, 0, 0, 0))],
        out_specs=pl.BlockSpec((1, MOBA_TOPK, heads, LANES), lambda b: (b, 0, 0, 0)),
        out_shape=jax.ShapeDtypeStruct((n_seq, MOBA_TOPK, heads, LANES), jnp.int32),
        compiler_params=pltpu.CompilerParams(dimension_semantics=("arbitrary",), vmem_limit_bytes=VMEM_LIMIT),
        name="select_sample",
    )(q4, kmean)


def _moba_sample_kernel(sel_ref, pt_ref, q_ref, kn_ref, vn_ref, sga_ref, ck_ref, cv_ref, o_ref, kbuf, vbuf, sems,
                        *, n_seq, n_pages, heads):
    b = pl.program_id(0)
    n_sel = MOBA_TOPK * PAGES_PER_BLOCK
    scale = HEAD_DIM ** -0.5

    def slab_copies(seq, slot):
        copies = []
        for h in range(heads):
            for t in range(MOBA_TOPK):
                blk = sel_ref[(seq * MOBA_TOPK + t) * heads + h]
                for r in range(PAGES_PER_BLOCK):
                    page = pt_ref[seq * n_pages + blk * PAGES_PER_BLOCK + r]
                    dst = t * PAGES_PER_BLOCK + r
                    copies.append(pltpu.make_async_copy(
                        ck_ref.at[page, :, h, :], kbuf.at[slot, h, dst], sems.at[slot, 0, h]))
                    copies.append(pltpu.make_async_copy(
                        cv_ref.at[page, :, h, :], vbuf.at[slot, h, dst], sems.at[slot, 1, h]))
        return copies

    slot = b % 2

    @pl.when(b == 0)
    def _():
        for c in slab_copies(b, slot):
            c.start()

    @pl.when(b + 1 < n_seq)
    def _():
        for c in slab_copies(b + 1, 1 - slot):
            c.start()

    for c in slab_copies(b, slot):
        c.wait()

    for h in range(heads):
        hs = slice(h, h + 1)
        qh = q_ref[0, hs, :]
        kh = kbuf[slot, h].reshape(n_sel * PAGE_SIZE, HEAD_DIM).astype(BF16)
        vh = vbuf[slot, h].reshape(n_sel * PAGE_SIZE, HEAD_DIM).astype(BF16)
        s = _dot_nt(qh.astype(BF16), kh) * scale
        s_new = jnp.sum(qh * kn_ref[0, hs, :], axis=1, keepdims=True) * scale
        m = jnp.maximum(jnp.max(s, axis=1, keepdims=True), s_new)
        p = jnp.exp(s - m)
        p_new = jnp.exp(s_new - m)
        l = jnp.sum(p, axis=1, keepdims=True) + p_new
        acc = jnp.dot(p.astype(BF16), vh, preferred_element_type=F32) + p_new * vn_ref[0, hs, :]
        o_ref[0, hs, :] = acc / l * sga_ref[0, hs, :]


def _moba_sample(sel_flat, page_table_flat, q4, kn4, vn4, sga4, cache_k, cache_v, n_pages):
    n_seq, heads, hd = q4.shape
    n_sel = MOBA_TOPK * PAGES_PER_BLOCK
    vec = pl.BlockSpec((1, heads, hd), lambda b, sel, pt: (b, 0, 0))
    hbm = pl.BlockSpec(memory_space=pl.ANY)
    return pl.pallas_call(
        functools.partial(_moba_sample_kernel, n_seq=n_seq, n_pages=n_pages, heads=heads),
        grid_spec=pltpu.PrefetchScalarGridSpec(
            num_scalar_prefetch=2,
            grid=(n_seq,),
            in_specs=[vec, vec, vec, vec, hbm, hbm],
            out_specs=vec,
            scratch_shapes=[pltpu.VMEM((2, heads, n_sel, PAGE_SIZE, hd), F32),
                            pltpu.VMEM((2, heads, n_sel, PAGE_SIZE, hd), F32),
                            pltpu.SemaphoreType.DMA((2, 2, heads))],
        ),
        out_shape=jax.ShapeDtypeStruct(q4.shape, F32),
        compiler_params=pltpu.CompilerParams(dimension_semantics=("arbitrary",), vmem_limit_bytes=VMEM_LIMIT),
        name="moba_sample",
    )(sel_flat, page_table_flat, q4, kn4, vn4, sga4, cache_k, cache_v)


def kernel(x_prompt, x_sample, p_prompt, p_sample, cache_k, cache_v, state_conv, page_table, w_in, b_in, w_dw,
           b_dw, g_cn, b_cn, w_pw, b_pw, w_out, b_out, g_ln, b_ln, w_pe, w_pg, b_pg):
    depth = w_in.shape[0]
    assert depth == 1
    batch, seq, d_model = x_prompt.shape
    n_seq, dec_seq, _ = x_sample.shape
    assert dec_seq == 1
    n_phys, page, heads, head_dim = cache_k.shape[1:]
    assert head_dim == HEAD_DIM and page == PAGE_SIZE
    attn_w = heads * head_dim
    conv_w = w_pw.shape[1]
    assert conv_w == attn_w and w_in.shape[2] == 7 * attn_w
    n_pages = page_table.shape[1]
    hist = CONV_K - 1
    alpha = (2 * depth) ** 0.25
    n = batch * seq

    row2 = lambda a: a.reshape(1, -1)
    w_in_b = w_in[0].astype(BF16)
    w_pw_b = w_pw[0].astype(BF16)
    w_out_b = w_out[0].astype(BF16)
    w_pg_b = w_pg[0].astype(BF16)
    w_pe_b = w_pe[0].astype(BF16)
    b_in2, b_dw2, g_cn2, b_cn2, b_pw2 = row2(b_in[0]), row2(b_dw[0]), row2(g_cn[0]), row2(b_cn[0]), row2(b_pw[0])
    b_out2, g_ln2, b_ln2, b_pg2 = row2(b_out[0]), row2(g_ln[0]), row2(b_ln[0]), row2(b_pg[0])
    conv_w_args = (w_dw[0], b_dw2, g_cn2, b_cn2, w_pw_b, b_pw2)
    out_w_args = (w_out_b, b_out2, g_ln2, b_ln2, w_pg_b, b_pg2, w_pe_b, alpha)

    xp = x_prompt.reshape(n, d_model)
    q, k, v, sga, u, sgc = _in_proj_all(xp, w_in_b, b_in2, tm=1024)
    pt_flat = page_table.reshape(-1)
    ck = cache_k.reshape(n_phys, page, heads, head_dim)
    cv = cache_v.reshape(n_phys, page, heads, head_dim)
    attn, kmean = _moba_prompt(q, k, v, sga, batch, seq, heads, pt_flat, ck, n_seq, n_pages)
    conv = _conv_prompt(u, sgc, *conv_w_args, seq=seq, tile=MOBA_BLOCK)
    n_blocks = seq // MOBA_BLOCK

    def paired_rows(r):
        blk = r % n_blocks
        mirror = n_blocks - 1 - blk
        return (r // n_blocks, jnp.minimum(blk, mirror), jnp.where(blk > mirror, 1, 0), 0, 0)

    y_prompt = _out_proj(xp, attn, pl.BlockSpec((1, 1, 1, MOBA_BLOCK, attn_w), paired_rows), conv,
                         p_prompt[0].reshape(n, -1), *out_w_args, tm=MOBA_BLOCK)
    conv_prompt_new = u.reshape(batch, seq, conv_w)[:, seq - hist:, :]

    xs = x_sample.reshape(n_seq, d_model)
    qs, ks, vs, sgas, us, sgcs = _in_proj_all(xs, w_in_b, b_in2, tm=n_seq)
    conv_s, conv_sample_new = _conv_sample(state_conv[0], us, sgcs, *conv_w_args)
    as4 = lambda a: a.reshape(n_seq, heads, head_dim)
    sel = _select_sample(as4(qs), kmean)[:, :, :, 0]
    attn_s = _moba_sample(sel.reshape(-1), pt_flat, as4(qs), as4(ks), as4(vs), as4(sgas), ck, cv, n_pages)
    y_sample = _out_proj(xs, attn_s.reshape(n_seq, attn_w), pl.BlockSpec((n_seq, attn_w), lambda r: (r, 0)), conv_s,
                         p_sample[0].reshape(n_seq, -1), *out_w_args, tm=n_seq)

    kv_shape = (depth, batch, seq, heads, head_dim)
    kvs_shape = (depth, n_seq, dec_seq, heads, head_dim)
    return (y_prompt.reshape(batch, seq, d_model), y_sample.reshape(n_seq, dec_seq, d_model),
            k.reshape(kv_shape), v.reshape(kv_shape), conv_prompt_new.reshape(depth, batch, hist, conv_w),
            ks.reshape(kvs_shape), vs.reshape(kvs_shape), conv_sample_new.reshape(depth, n_seq, hist, conv_w))
```

```python
import functools

import jax
import jax.numpy as jnp
from jax import lax
from jax.experimental import pallas as pl
from jax.experimental.pallas import tpu as pltpu

F32 = jnp.float32
BF16 = jnp.bfloat16

HEAD_DIM = 128
CONV_K = 31
MOBA_BLOCK = 256
MOBA_TOPK = 3
PAGE_SIZE = 128
PAGES_PER_BLOCK = MOBA_BLOCK // PAGE_SIZE
LN_EPS = 1e-5
NEG = -1e30
LOG2_E = 1.4426950408889634
LANES = 128
SUBLANES = 8
SUM_ROWS = 16
CONV_HALO = 32
VMEM_LIMIT = 60 * 1024 * 1024


def _sigmoid(z):
    return 0.5 * jnp.tanh(0.5 * z) + 0.5


def _silu(z):
    return z * _sigmoid(z)


def _dot_nt(a, b, **kw):
    return lax.dot_general(a, b, (((1,), (1,)), ((), ())), preferred_element_type=F32, **kw)


def _in_proj_kernel(x_ref, xs_ref, w_ref, b_ref, *refs, finishers, n_out):
    outs_p, outs_s, (xb_ref, *hold) = refs[:n_out], refs[n_out:2 * n_out], refs[2 * n_out:]
    j = pl.program_id(1)
    tm = x_ref.shape[0]

    @pl.when(j == 0)
    def _():
        xb_ref[0:tm, :] = x_ref[...].astype(BF16)
        xb_ref[tm:, :] = xs_ref[...].astype(BF16)

    def write(k, value):
        outs_p[k][...] = value[0:tm].astype(outs_p[k].dtype)
        outs_s[k][...] = value[tm:].astype(outs_s[k].dtype)

    for g, finish in enumerate(finishers):
        @pl.when(j == g)
        def _(finish=finish):
            finish(jnp.dot(xb_ref[...], w_ref[...], preferred_element_type=F32) + b_ref[...], write, *hold)


def _emit(k, fn=lambda z: z):
    return lambda z, write, *hold: write(k, fn(z))


def _hold(z, write, hold_ref):
    hold_ref[...] = z


def _glu_into(k):
    return lambda z, write, hold_ref: write(k, hold_ref[...] * _sigmoid(z))


QKV_GROUPS = (_emit(0), _emit(1), _emit(2))
GATE_GROUPS = (_emit(0, _silu), _hold, _glu_into(1), _emit(2, _silu))


def _in_proj(x, xs, w_b, b, tm, first_group, finishers, out_dtypes, needs_hold):
    n, d = x.shape
    ns = xs.shape[0]
    gw = w_b.shape[1] // 7
    row = pl.BlockSpec((tm, gw), lambda i, j: (i, 0))
    fixed = pl.BlockSpec((ns, gw), lambda i, j: (0, 0))
    return pl.pallas_call(
        functools.partial(_in_proj_kernel, finishers=finishers, n_out=len(out_dtypes)),
        grid=(n // tm, len(finishers)),
        in_specs=[
            pl.BlockSpec((tm, d), lambda i, j: (i, 0)),
            pl.BlockSpec((ns, d), lambda i, j: (0, 0)),
            pl.BlockSpec((d, gw), lambda i, j: (0, first_group + j)),
            pl.BlockSpec((1, gw), lambda i, j: (0, first_group + j)),
        ],
        out_specs=[row] * len(out_dtypes) + [fixed] * len(out_dtypes),
        out_shape=([jax.ShapeDtypeStruct((n, gw), dt) for dt in out_dtypes]
                   + [jax.ShapeDtypeStruct((ns, gw), dt) for dt in out_dtypes]),
        scratch_shapes=([pltpu.VMEM((tm + ns, d), BF16)]
                        + ([pltpu.VMEM((tm + ns, gw), F32)] if needs_hold else [])),
        compiler_params=pltpu.CompilerParams(
            dimension_semantics=("arbitrary", "arbitrary"), vmem_limit_bytes=VMEM_LIMIT),
        name="in_proj",
    )(x, xs, w_b, b)


def _in_proj_all(x, xs, w_b, b, tm):
    q, k, v, qs, ks, vs = _in_proj(x, xs, w_b, b, tm, 0, QKV_GROUPS, (F32, F32, F32), False)
    sga, u, sgc, sgas, us, sgcs = _in_proj(x, xs, w_b, b, tm, len(QKV_GROUPS), GATE_GROUPS, (BF16, F32, BF16), True)
    return (q, k, v, sga, u, sgc), (qs, ks, vs, sgas, us, sgcs)


def _topk_mask(g, idx, n_valid, axis):
    sel = jnp.zeros(g.shape, F32)
    big = g.shape[axis]
    for t in range(MOBA_TOPK):
        m = jnp.max(g, axis=axis, keepdims=True)
        first = jnp.min(jnp.where(g == m, idx, big), axis=axis, keepdims=True)
        hit = idx == first
        counts = jnp.where(t < n_valid, 1.0, 0.0)
        sel = jnp.maximum(sel, jnp.where(hit, counts, 0.0))
        g = jnp.where(hit, -jnp.inf, g)
    return sel


def _moba_prompt_kernel(pt_ref, qa_ref, qb_ref, qna_ref, qnb_ref, k_ref, v_ref, sgaa_ref, sgab_ref, *refs,
                        n_blocks, pages_per_step):
    del pt_ref
    page_refs = refs[:pages_per_step]
    o_ref, pmean_ref, kb_ref, vt_ref, kmean_ref, acc_ref = refs[pages_per_step:pages_per_step + 6]
    bufs = refs[pages_per_step + 6:]
    stage = (bufs[0:4], bufs[4:8])
    i = pl.program_id(2)
    blk = MOBA_BLOCK
    last = n_blocks - 1
    pairs = n_blocks // 2
    qscale = HEAD_DIM ** -0.5 * LOG2_E
    sga_refs = (sgaa_ref, sgab_ref)

    def past_slot(t, pair):
        first = t < pair
        return jnp.where(first, 1.0, 0.0), jnp.where(first, 0, 1), jnp.where(first, t, t - pair)

    def cache_block_mean(a):
        tot = jnp.sum(page_refs[PAGES_PER_BLOCK * a][0], axis=0)
        for r in range(1, PAGES_PER_BLOCK):
            tot = tot + jnp.sum(page_refs[PAGES_PER_BLOCK * a + r][0], axis=0)
        pmean_ref[0, a] = tot * (1.0 / MOBA_BLOCK)

    def score_pass(pair, q_refs, buf, with_cache_stream):
        qt_ref, bias_ref, m_ref, s_ref = buf
        own = (pair, last - pair)
        for w in range(2):
            qf_t = q_refs[w][...].T
            qt_ref[w] = (qf_t * qscale).astype(BF16)
            gate = jnp.dot(kmean_ref[...], qf_t, preferred_element_type=F32, precision=lax.Precision.HIGHEST)
            row = lax.broadcasted_iota(jnp.int32, gate.shape, 0)
            gate = jnp.where(row < own[w], gate, -jnp.inf)
            sel = _topk_mask(gate, row, own[w], axis=0)
            bias_ref[w] = jnp.where(sel > 0.5, 0.0, NEG)
        k_id = lax.broadcasted_iota(jnp.int32, (blk, blk), 0)
        q_id = lax.broadcasted_iota(jnp.int32, (blk, blk), 1)
        m = []
        for w in range(2):
            s = jnp.dot(kb_ref[own[w]], qt_ref[w], preferred_element_type=F32)
            s = jnp.where(k_id <= q_id, s, NEG)
            s_ref[w] = s
            m.append(jnp.max(s, axis=0, keepdims=True))
        for t in range(last):
            fa, w, kblk = past_slot(t, pair)
            s = jnp.dot(kb_ref[kblk], qt_ref[w], preferred_element_type=F32)
            s_ref[2 + t] = s
            c = jnp.max(s, axis=0, keepdims=True) + bias_ref[w, pl.ds(kblk, 1), :]
            m[0] = jnp.maximum(m[0], c + (fa - 1.0) * -NEG)
            m[1] = jnp.maximum(m[1], c + fa * NEG)
            if with_cache_stream and t < pages_per_step // PAGES_PER_BLOCK:
                cache_block_mean(t)
        for w in range(2):
            m_ref[w] = m[w]

    def value_pass(pair, buf):
        _, bias_ref, m_ref, s_ref = buf
        own = (pair, last - pair)
        m = [m_ref[0], m_ref[1]]
        for w in range(2):
            p = jnp.exp2((s_ref[w] - m[w]).astype(BF16))
            acc_ref[w] = jnp.dot(vt_ref[own[w]], p, preferred_element_type=F32)
        for t in range(last):
            fa, w, kblk = past_slot(t, pair)
            shift = bias_ref[w, pl.ds(kblk, 1), :] - (m[1] + fa * (m[0] - m[1]))
            p = jnp.exp2((s_ref[2 + t] + shift).astype(BF16))
            acc_ref[2 + t] = jnp.dot(vt_ref[kblk], p, preferred_element_type=F32)
        bounds = ((0, pair), (pair, last))
        for w in range(2):
            acc = lax.fori_loop(bounds[w][0], bounds[w][1], lambda t, a: a + acc_ref[2 + t], acc_ref[w])
            out_t = acc[0:HEAD_DIM, :] / acc[HEAD_DIM:HEAD_DIM + 1, :]
            o_ref[0, 0, w] = (out_t.T * sga_refs[w][...]).astype(o_ref.dtype)

    @pl.when(i == 0)
    def _():
        kmean_ref[...] = jnp.mean(k_ref[...].reshape(n_blocks, blk, HEAD_DIM), axis=1)
        for jb in range(n_blocks):
            rows = slice(jb * blk, (jb + 1) * blk)
            kb_ref[jb] = k_ref[rows, :].astype(BF16)
            vt_ref[jb, 0:HEAD_DIM, :] = v_ref[rows, :].T.astype(BF16)
            vt_ref[jb, HEAD_DIM:, :] = jnp.ones((SUM_ROWS, blk), BF16)
        score_pass(0, (qa_ref, qb_ref), stage[0], False)

    nxt = jnp.minimum(i + 1, pairs - 1)
    for parity in range(2):
        @pl.when(i % 2 == parity)
        def _():
            score_pass(nxt, (qna_ref, qnb_ref), stage[1 - parity], True)
            value_pass(i, stage[parity])


def _moba_prompt(q, k, v, sga, batch, seq, heads, page_table_flat, cache, n_seq, n_pages):
    n_blocks = seq // MOBA_BLOCK
    assert n_blocks % 2 == 0
    last = n_blocks - 1
    pairs = n_blocks // 2
    n_phys, page, c_heads, hd = cache.shape
    steps = batch * heads * pairs
    pages_per_step = n_seq * n_pages // steps
    parts = n_pages // pages_per_step
    assert pages_per_step * steps == n_seq * n_pages and parts * pages_per_step == n_pages
    blocks_per_step = pages_per_step // PAGES_PER_BLOCK
    assert blocks_per_step * PAGES_PER_BLOCK == pages_per_step and blocks_per_step <= last

    def step_id(b, h, i):
        return (b * heads + h) * pairs + i

    def page_spec(r):
        return pl.BlockSpec((1, page, c_heads, hd),
                            lambda b, h, i, pt: (pt[step_id(b, h, i) * pages_per_step + r], 0, 0, 0))

    qa = pl.BlockSpec((MOBA_BLOCK, HEAD_DIM), lambda b, h, i, pt: (b * n_blocks + i, h))
    qb = pl.BlockSpec((MOBA_BLOCK, HEAD_DIM), lambda b, h, i, pt: (b * n_blocks + last - i, h))
    qna = pl.BlockSpec((MOBA_BLOCK, HEAD_DIM), lambda b, h, i, pt: (b * n_blocks + jnp.minimum(i + 1, pairs - 1), h))
    qnb = pl.BlockSpec((MOBA_BLOCK, HEAD_DIM),
                       lambda b, h, i, pt: (b * n_blocks + last - jnp.minimum(i + 1, pairs - 1), h))
    score_stage = [pltpu.VMEM((2, HEAD_DIM, MOBA_BLOCK), BF16),
                   pltpu.VMEM((2, n_blocks, MOBA_BLOCK), F32),
                   pltpu.VMEM((2, 1, MOBA_BLOCK), F32),
                   pltpu.VMEM((n_blocks + 1, MOBA_BLOCK, MOBA_BLOCK), F32)]
    kvspec = pl.BlockSpec((seq, HEAD_DIM), lambda b, h, i, pt: (b, h))
    return pl.pallas_call(
        functools.partial(_moba_prompt_kernel, n_blocks=n_blocks, pages_per_step=pages_per_step),
        grid_spec=pltpu.PrefetchScalarGridSpec(
            num_scalar_prefetch=1,
            grid=(batch, heads, pairs),
            in_specs=[qa, qb, qna, qnb, kvspec, kvspec, qa, qb] + [page_spec(r) for r in range(pages_per_step)],
            out_specs=[
                pl.BlockSpec((1, 1, 2, MOBA_BLOCK, HEAD_DIM), lambda b, h, i, pt: (b, i, 0, 0, h)),
                pl.BlockSpec((1, blocks_per_step, c_heads, hd),
                             lambda b, h, i, pt: (step_id(b, h, i) // parts, step_id(b, h, i) % parts, 0, 0)),
            ],
            scratch_shapes=[pltpu.VMEM((n_blocks, MOBA_BLOCK, HEAD_DIM), BF16),
                            pltpu.VMEM((n_blocks, HEAD_DIM + SUM_ROWS, MOBA_BLOCK), BF16),
                            pltpu.VMEM((n_blocks, HEAD_DIM), F32),
                            pltpu.VMEM((n_blocks + 1, HEAD_DIM + SUM_ROWS, MOBA_BLOCK), F32)] + score_stage * 2,
        ),
        out_shape=[jax.ShapeDtypeStruct((batch, pairs, 2, MOBA_BLOCK, q.shape[1]), BF16),
                   jax.ShapeDtypeStruct((n_seq, n_pages // PAGES_PER_BLOCK, c_heads, hd), F32)],
        compiler_params=pltpu.CompilerParams(
            dimension_semantics=("arbitrary", "arbitrary", "arbitrary"), vmem_limit_bytes=VMEM_LIMIT),
        name="moba_prompt",
    )(page_table_flat, q, q, q, q, k, v, sga, sga, *([cache] * pages_per_step))


def _conv_tail(c, sgc, gcn_ref, bcn_ref, wpw_ref, bpw_ref):
    mu = jnp.mean(c, axis=-1, keepdims=True)
    d = c - mu
    var = jnp.mean(d * d, axis=-1, keepdims=True)
    y = d * lax.rsqrt(var + LN_EPS) * gcn_ref[...] + bcn_ref[...]
    y = _silu(y)
    return (jnp.dot(y.astype(BF16), wpw_ref[...], preferred_element_type=F32) + bpw_ref[...]) * sgc


def _conv_prompt_kernel(u_ref, prev_ref, sgc_ref, wdw_ref, bdw_ref, gcn_ref, bcn_ref, wpw_ref, bpw_ref,
                        o_ref, ext_ref, sh_ref, c_ref, *, tiles_per_seq, rows):
    i = pl.program_id(0)
    t, w = u_ref.shape
    first = (i % tiles_per_seq) == 0
    ext_ref[0:CONV_HALO, :] = jnp.where(first, 0.0, prev_ref[...])
    ext_ref[CONV_HALO:, :] = u_ref[...]
    n_sh = sh_ref.shape[1]
    for r in range(1, SUBLANES):
        sh_ref[r - 1] = ext_ref[r:r + n_sh, :]
    off = CONV_HALO - (CONV_K - 1)
    for c in range(w // LANES):
        cs = slice(c * LANES, (c + 1) * LANES)
        for rc in range(t // rows):
            acc = jnp.broadcast_to(bdw_ref[:, cs], (rows, LANES))
            for j in range(CONV_K):
                a, r = divmod(off + j, SUBLANES)
                lo = rc * rows + a * SUBLANES
                win = ext_ref[lo:lo + rows, cs] if r == 0 else sh_ref[r - 1, lo:lo + rows, cs]
                acc = acc + win * wdw_ref[j:j + 1, cs]
            c_ref[rc * rows:(rc + 1) * rows, cs] = acc
    o_ref[...] = _conv_tail(c_ref[...], sgc_ref[...], gcn_ref, bcn_ref, wpw_ref, bpw_ref).astype(o_ref.dtype)


def _conv_prompt(u, sgc, w_dw, b_dw, g_cn, b_cn, w_pw_b, b_pw, seq, tile):
    n, w = u.shape
    halo_per_tile = tile // CONV_HALO
    full = lambda shape: pl.BlockSpec(shape, lambda i: (0, 0))
    rowspec = pl.BlockSpec((tile, w), lambda i: (i, 0))
    return pl.pallas_call(
        functools.partial(_conv_prompt_kernel, tiles_per_seq=seq // tile, rows=64),
        grid=(n // tile,),
        in_specs=[
            rowspec,
            pl.BlockSpec((CONV_HALO, w), lambda i: (jnp.maximum(i * halo_per_tile - 1, 0), 0)),
            rowspec,
            full((CONV_K, w)), full((1, w)), full((1, w)), full((1, w)), full((w, w)), full((1, w)),
        ],
        out_specs=rowspec,
        out_shape=jax.ShapeDtypeStruct((n, w), BF16),
        scratch_shapes=[pltpu.VMEM((tile + CONV_HALO, w), F32),
                        pltpu.VMEM((SUBLANES - 1, tile + CONV_HALO - SUBLANES, w), F32),
                        pltpu.VMEM((tile, w), F32)],
        compiler_params=pltpu.CompilerParams(dimension_semantics=("arbitrary",), vmem_limit_bytes=VMEM_LIMIT),
        name="conv_prompt",
    )(u, u, sgc, w_dw, b_dw, g_cn, b_cn, w_pw_b, b_pw)


def _conv_sample_kernel(state_ref, u_ref, sgc_ref, wdw_ref, bdw_ref, gcn_ref, bcn_ref, wpw_ref, bpw_ref,
                        o_ref, new_ref, c_ref):
    nb = state_ref.shape[0]
    hist = CONV_K - 1
    for b in range(nb):
        st = state_ref[b]
        u_row = u_ref[b:b + 1, :]
        c_ref[b:b + 1, :] = (jnp.sum(st * wdw_ref[0:hist, :], axis=0, keepdims=True)
                             + u_row * wdw_ref[hist:hist + 1, :] + bdw_ref[...])
        new_ref[b, 0:hist - 1, :] = state_ref[b, 1:hist, :]
        new_ref[b, hist - 1:hist, :] = u_row
    o_ref[...] = _conv_tail(c_ref[...], sgc_ref[...], gcn_ref, bcn_ref, wpw_ref, bpw_ref).astype(o_ref.dtype)


def _conv_sample(state, u, sgc, w_dw, b_dw, g_cn, b_cn, w_pw_b, b_pw):
    nb, hist, w = state.shape
    full2 = lambda shape: pl.BlockSpec(shape, lambda i: (0, 0))
    full3 = lambda shape: pl.BlockSpec(shape, lambda i: (0, 0, 0))
    return pl.pallas_call(
        _conv_sample_kernel,
        grid=(1,),
        in_specs=[full3((nb, hist, w)), full2((nb, w)), full2((nb, w)),
                  full2((CONV_K, w)), full2((1, w)), full2((1, w)), full2((1, w)), full2((w, w)), full2((1, w))],
        out_specs=[full2((nb, w)), full3((nb, hist, w))],
        out_shape=[jax.ShapeDtypeStruct((nb, w), BF16), jax.ShapeDtypeStruct((nb, hist, w), F32)],
        scratch_shapes=[pltpu.VMEM((nb, w), F32)],
        compiler_params=pltpu.CompilerParams(dimension_semantics=("arbitrary",), vmem_limit_bytes=VMEM_LIMIT),
        name="conv_sample",
    )(state, u, sgc, w_dw, b_dw, g_cn, b_cn, w_pw_b, b_pw)


def _out_kernel(x_ref, a_ref, c_ref, p_ref, woa_ref, woc_ref, bo_ref, g_ref, b_ref, wpg_ref, bpg_ref, wpe_ref,
                o_ref, *, alpha):
    attn = a_ref[...].reshape(c_ref.shape).astype(BF16)
    mix = (jnp.dot(attn, woa_ref[...], preferred_element_type=F32)
           + jnp.dot(c_ref[...].astype(BF16), woc_ref[...], preferred_element_type=F32) + bo_ref[...])
    t = alpha * x_ref[...] + mix
    mu = jnp.mean(t, axis=-1, keepdims=True)
    d = t - mu
    var = jnp.mean(d * d, axis=-1, keepdims=True)
    h = d * lax.rsqrt(var + LN_EPS) * g_ref[...] + b_ref[...]
    gate = _sigmoid(jnp.dot(h.astype(BF16), wpg_ref[...], preferred_element_type=F32) + bpg_ref[...])
    pe = jnp.dot(p_ref[...].astype(BF16), wpe_ref[...], preferred_element_type=F32)
    o_ref[...] = h + gate * pe


def _out_proj(x, attn, attn_spec, conv, p, w_out_b, b_out, g_ln, b_ln, w_pg_b, b_pg, w_pe_b, alpha, tm):
    n, d = x.shape
    half = conv.shape[1]
    pd = p.shape[1]
    const = lambda shape, r=0: pl.BlockSpec(shape, lambda i: (r, 0), pipeline_mode=pl.Buffered(1))
    row = lambda w: pl.BlockSpec((tm, w), lambda i: (i, 0))
    return pl.pallas_call(
        functools.partial(_out_kernel, alpha=alpha),
        grid=(n // tm,),
        in_specs=[row(d), attn_spec, row(half), row(pd),
                  const((half, d), 0), const((half, d), 1), const((1, d)), const((1, d)), const((1, d)),
                  const((d, d)), const((1, d)), const((pd, d))],
        out_specs=row(d),
        out_shape=jax.ShapeDtypeStruct((n, d), F32),
        compiler_params=pltpu.CompilerParams(dimension_semantics=("arbitrary",), vmem_limit_bytes=VMEM_LIMIT),
        name="out_proj",
    )(x, attn, conv, p, w_out_b, w_out_b, b_out, g_ln, b_ln, w_pg_b, b_pg, w_pe_b)


def _select_kernel(q_ref, kmean_ref, o_ref):
    g = jnp.sum(kmean_ref[0] * q_ref[...], axis=-1, keepdims=True)
    n_blocks = g.shape[0]
    idx = lax.broadcasted_iota(jnp.int32, g.shape, 0)
    for t in range(MOBA_TOPK):
        m = jnp.max(g, axis=0, keepdims=True)
        first = jnp.min(jnp.where(g == m, idx, n_blocks), axis=0, keepdims=True)
        o_ref[0, t] = jnp.broadcast_to(first[0], o_ref.shape[2:])
        g = jnp.where(idx == first, -jnp.inf, g)


def _select_sample(q4, kmean):
    n_seq, n_blocks, heads, hd = kmean.shape
    assert n_blocks >= MOBA_TOPK
    return pl.pallas_call(
        _select_kernel,
        grid=(n_seq,),
        in_specs=[pl.BlockSpec((1, heads, hd), lambda b: (b, 0, 0)),
                  pl.BlockSpec((1, n_blocks, heads, hd), lambda b: (b, 0, 0, 0))],
        out_specs=pl.BlockSpec((1, MOBA_TOPK, heads, LANES), lambda b: (b, 0, 0, 0)),
        out_shape=jax.ShapeDtypeStruct((n_seq, MOBA_TOPK, heads, LANES), jnp.int32),
        compiler_params=pltpu.CompilerParams(dimension_semantics=("arbitrary",), vmem_limit_bytes=VMEM_LIMIT),
        name="select_sample",
    )(q4, kmean)


def _moba_sample_kernel(sel_ref, pt_ref, q_ref, kn_ref, vn_ref, sga_ref, ck_ref, cv_ref, o_ref, kbuf, vbuf, sems,
                        *, n_seq, n_pages, heads):
    b = pl.program_id(0)
    n_sel = MOBA_TOPK * PAGES_PER_BLOCK
    scale = HEAD_DIM ** -0.5

    def slab_copies(seq, slot):
        copies = []
        for h in range(heads):
            for t in range(MOBA_TOPK):
                blk = sel_ref[(seq * MOBA_TOPK + t) * heads + h]
                for r in range(PAGES_PER_BLOCK):
                    page = pt_ref[seq * n_pages + blk * PAGES_PER_BLOCK + r]
                    dst = t * PAGES_PER_BLOCK + r
                    copies.append(pltpu.make_async_copy(
                        ck_ref.at[page, :, h, :], kbuf.at[slot, h, dst], sems.at[slot, 0, h]))
                    copies.append(pltpu.make_async_copy(
                        cv_ref.at[page, :, h, :], vbuf.at[slot, h, dst], sems.at[slot, 1, h]))
        return copies

    slot = b % 2

    @pl.when(b == 0)
    def _():
        for c in slab_copies(b, slot):
            c.start()

    @pl.when(b + 1 < n_seq)
    def _():
        for c in slab_copies(b + 1, 1 - slot):
            c.start()

    for c in slab_copies(b, slot):
        c.wait()

    for h in range(heads):
        hs = slice(h, h + 1)
        qh = q_ref[0, hs, :]
        kh = kbuf[slot, h].reshape(n_sel * PAGE_SIZE, HEAD_DIM).astype(BF16)
        vh = vbuf[slot, h].reshape(n_sel * PAGE_SIZE, HEAD_DIM).astype(BF16)
        s = _dot_nt(qh.astype(BF16), kh) * scale
        s_new = jnp.sum(qh * kn_ref[0, hs, :], axis=1, keepdims=True) * scale
        m = jnp.maximum(jnp.max(s, axis=1, keepdims=True), s_new)
        p = jnp.exp(s - m)
        p_new = jnp.exp(s_new - m)
        l = jnp.sum(p, axis=1, keepdims=True) + p_new
        acc = jnp.dot(p.astype(BF16), vh, preferred_element_type=F32) + p_new * vn_ref[0, hs, :]
        o_ref[0, hs, :] = acc / l * sga_ref[0, hs, :]


def _moba_sample(sel_flat, page_table_flat, q4, kn4, vn4, sga4, cache_k, cache_v, n_pages):
    n_seq, heads, hd = q4.shape
    n_sel = MOBA_TOPK * PAGES_PER_BLOCK
    vec = pl.BlockSpec((1, heads, hd), lambda b, sel, pt: (b, 0, 0))
    hbm = pl.BlockSpec(memory_space=pl.ANY)
    return pl.pallas_call(
        functools.partial(_moba_sample_kernel, n_seq=n_seq, n_pages=n_pages, heads=heads),
        grid_spec=pltpu.PrefetchScalarGridSpec(
            num_scalar_prefetch=2,
            grid=(n_seq,),
            in_specs=[vec, vec, vec, vec, hbm, hbm],
            out_specs=vec,
            scratch_shapes=[pltpu.VMEM((2, heads, n_sel, PAGE_SIZE, hd), F32),
                            pltpu.VMEM((2, heads, n_sel, PAGE_SIZE, hd), F32),
                            pltpu.SemaphoreType.DMA((2, 2, heads))],
        ),
        out_shape=jax.ShapeDtypeStruct(q4.shape, F32),
        compiler_params=pltpu.CompilerParams(dimension_semantics=("arbitrary",), vmem_limit_bytes=VMEM_LIMIT),
        name="moba_sample",
    )(sel_flat, page_table_flat, q4, kn4, vn4, sga4, cache_k, cache_v)


def kernel(x_prompt, x_sample, p_prompt, p_sample, cache_k, cache_v, state_conv, page_table, w_in, b_in, w_dw,
           b_dw, g_cn, b_cn, w_pw, b_pw, w_out, b_out, g_ln, b_ln, w_pe, w_pg, b_pg):
    depth = w_in.shape[0]
    assert depth == 1
    batch, seq, d_model = x_prompt.shape
    n_seq, dec_seq, _ = x_sample.shape
    assert dec_seq == 1
    n_phys, page, heads, head_dim = cache_k.shape[1:]
    assert head_dim == HEAD_DIM and page == PAGE_SIZE
    attn_w = heads * head_dim
    conv_w = w_pw.shape[1]
    assert conv_w == attn_w and w_in.shape[2] == 7 * attn_w
    n_pages = page_table.shape[1]
    hist = CONV_K - 1
    alpha = (2 * depth) ** 0.25
    n = batch * seq

    row2 = lambda a: a.reshape(1, -1)
    w_in_b = w_in[0].astype(BF16)
    w_pw_b = w_pw[0].astype(BF16)
    w_out_b = w_out[0].astype(BF16)
    w_pg_b = w_pg[0].astype(BF16)
    w_pe_b = w_pe[0].astype(BF16)
    b_in2, b_dw2, g_cn2, b_cn2, b_pw2 = row2(b_in[0]), row2(b_dw[0]), row2(g_cn[0]), row2(b_cn[0]), row2(b_pw[0])
    b_out2, g_ln2, b_ln2, b_pg2 = row2(b_out[0]), row2(g_ln[0]), row2(b_ln[0]), row2(b_pg[0])
    conv_w_args = (w_dw[0], b_dw2, g_cn2, b_cn2, w_pw_b, b_pw2)
    out_w_args = (w_out_b, b_out2, g_ln2, b_ln2, w_pg_b, b_pg2, w_pe_b, alpha)

    xp = x_prompt.reshape(n, d_model)
    xs = x_sample.reshape(n_seq, d_model)
    (q, k, v, sga, u, sgc), (qs, ks, vs, sgas, us, sgcs) = _in_proj_all(xp, xs, w_in_b, b_in2, tm=1024)
    pt_flat = page_table.reshape(-1)
    ck = cache_k.reshape(n_phys, page, heads, head_dim)
    cv = cache_v.reshape(n_phys, page, heads, head_dim)
    attn, kmean = _moba_prompt(q, k, v, sga, batch, seq, heads, pt_flat, ck, n_seq, n_pages)
    conv = _conv_prompt(u, sgc, *conv_w_args, seq=seq, tile=MOBA_BLOCK)
    n_blocks = seq // MOBA_BLOCK

    def paired_rows(r):
        blk = r % n_blocks
        mirror = n_blocks - 1 - blk
        return (r // n_blocks, jnp.minimum(blk, mirror), jnp.where(blk > mirror, 1, 0), 0, 0)

    y_prompt = _out_proj(xp, attn, pl.BlockSpec((1, 1, 1, MOBA_BLOCK, attn_w), paired_rows), conv,
                         p_prompt[0].reshape(n, -1), *out_w_args, tm=MOBA_BLOCK)
    conv_prompt_new = u.reshape(batch, seq, conv_w)[:, seq - hist:, :]

    conv_s, conv_sample_new = _conv_sample(state_conv[0], us, sgcs, *conv_w_args)
    as4 = lambda a: a.reshape(n_seq, heads, head_dim)
    sel = _select_sample(as4(qs), kmean)[:, :, :, 0]
    attn_s = _moba_sample(sel.reshape(-1), pt_flat, as4(qs), as4(ks), as4(vs), as4(sgas), ck, cv, n_pages)
    y_sample = _out_proj(xs, attn_s.reshape(n_seq, attn_w), pl.BlockSpec((n_seq, attn_w), lambda r: (r, 0)), conv_s,
                         p_sample[0].reshape(n_seq, -1), *out_w_args, tm=n_seq)

    kv_shape = (depth, batch, seq, heads, head_dim)
    kvs_shape = (depth, n_seq, dec_seq, heads, head_dim)
    return (y_prompt.reshape(batch, seq, d_model), y_sample.reshape(n_seq, dec_seq, d_model),
            k.reshape(kv_shape), v.reshape(kv_shape), conv_prompt_new.reshape(depth, batch, hist, conv_w),
            ks.reshape(kvs_shape), vs.reshape(kvs_shape), conv_sample_new.reshape(depth, n_seq, hist, conv_w))
```

```python
import functools

import jax
import jax.numpy as jnp
from jax import lax
from jax.experimental import pallas as pl
from jax.experimental.pallas import tpu as pltpu

F32 = jnp.float32
BF16 = jnp.bfloat16

HEAD_DIM = 128
CONV_K = 31
MOBA_BLOCK = 256
MOBA_TOPK = 3
PAGE_SIZE = 128
PAGES_PER_BLOCK = MOBA_BLOCK // PAGE_SIZE
LN_EPS = 1e-5
NEG = -1e30
LOG2_E = 1.4426950408889634
LANES = 128
SUBLANES = 8
SUM_ROWS = 16
CONV_HALO = 32
OUT_PROJ_PAGES_PER_STEP = 16
VMEM_LIMIT = 60 * 1024 * 1024


def _sigmoid(z):
    return 0.5 * jnp.tanh(0.5 * z) + 0.5


def _silu(z):
    return z * _sigmoid(z)


def _dot_nt(a, b, **kw):
    return lax.dot_general(a, b, (((1,), (1,)), ((), ())), preferred_element_type=F32, **kw)


def _in_proj_kernel(x_ref, xs_ref, w_ref, b_ref, *refs, finishers, n_out):
    outs_p, outs_s, (xb_ref, *hold) = refs[:n_out], refs[n_out:2 * n_out], refs[2 * n_out:]
    j = pl.program_id(1)
    tm = x_ref.shape[0]

    @pl.when(j == 0)
    def _():
        xb_ref[0:tm, :] = x_ref[...].astype(BF16)
        xb_ref[tm:, :] = xs_ref[...].astype(BF16)

    def write(k, value):
        outs_p[k][...] = value[0:tm].astype(outs_p[k].dtype)
        outs_s[k][...] = value[tm:].astype(outs_s[k].dtype)

    for g, finish in enumerate(finishers):
        @pl.when(j == g)
        def _(finish=finish):
            finish(jnp.dot(xb_ref[...], w_ref[...], preferred_element_type=F32) + b_ref[...], write, *hold)


def _emit(k, fn=lambda z: z):
    return lambda z, write, *hold: write(k, fn(z))


def _hold(z, write, hold_ref):
    hold_ref[...] = z


def _glu_into(k):
    return lambda z, write, hold_ref: write(k, hold_ref[...] * _sigmoid(z))


QKV_GROUPS = (_emit(0), _emit(1), _emit(2))
GATE_GROUPS = (_emit(0, _silu), _hold, _glu_into(1), _emit(2, _silu))


def _in_proj(x, xs, w_b, b, tm, first_group, finishers, out_dtypes, needs_hold):
    n, d = x.shape
    ns = xs.shape[0]
    gw = w_b.shape[1] // 7
    row = pl.BlockSpec((tm, gw), lambda i, j: (i, 0))
    fixed = pl.BlockSpec((ns, gw), lambda i, j: (0, 0))
    return pl.pallas_call(
        functools.partial(_in_proj_kernel, finishers=finishers, n_out=len(out_dtypes)),
        grid=(n // tm, len(finishers)),
        in_specs=[
            pl.BlockSpec((tm, d), lambda i, j: (i, 0)),
            pl.BlockSpec((ns, d), lambda i, j: (0, 0)),
            pl.BlockSpec((d, gw), lambda i, j: (0, first_group + j)),
            pl.BlockSpec((1, gw), lambda i, j: (0, first_group + j)),
        ],
        out_specs=[row] * len(out_dtypes) + [fixed] * len(out_dtypes),
        out_shape=([jax.ShapeDtypeStruct((n, gw), dt) for dt in out_dtypes]
                   + [jax.ShapeDtypeStruct((ns, gw), dt) for dt in out_dtypes]),
        scratch_shapes=([pltpu.VMEM((tm + ns, d), BF16)]
                        + ([pltpu.VMEM((tm + ns, gw), F32)] if needs_hold else [])),
        compiler_params=pltpu.CompilerParams(
            dimension_semantics=("arbitrary", "arbitrary"), vmem_limit_bytes=VMEM_LIMIT),
        name="in_proj",
    )(x, xs, w_b, b)


def _in_proj_all(x, xs, w_b, b, tm):
    q, k, v, qs, ks, vs = _in_proj(x, xs, w_b, b, tm, 0, QKV_GROUPS, (F32, F32, F32), False)
    sga, u, sgc, sgas, us, sgcs = _in_proj(x, xs, w_b, b, tm, len(QKV_GROUPS), GATE_GROUPS, (BF16, F32, BF16), True)
    return (q, k, v, sga, u, sgc), (qs, ks, vs, sgas, us, sgcs)


def _topk_mask(g, idx, n_valid, axis):
    sel = jnp.zeros(g.shape, F32)
    big = g.shape[axis]
    for t in range(MOBA_TOPK):
        m = jnp.max(g, axis=axis, keepdims=True)
        first = jnp.min(jnp.where(g == m, idx, big), axis=axis, keepdims=True)
        hit = idx == first
        counts = jnp.where(t < n_valid, 1.0, 0.0)
        sel = jnp.maximum(sel, jnp.where(hit, counts, 0.0))
        g = jnp.where(hit, -jnp.inf, g)
    return sel


def _cache_block_mean(page_refs, pmean_ref, a):
    tot = jnp.sum(page_refs[PAGES_PER_BLOCK * a][0], axis=0)
    for r in range(1, PAGES_PER_BLOCK):
        tot = tot + jnp.sum(page_refs[PAGES_PER_BLOCK * a + r][0], axis=0)
    pmean_ref[a] = tot * (1.0 / MOBA_BLOCK)


def _page_stream_specs(cache, step_of, first_page, pages_per_step):
    n_phys, page, heads, hd = cache.shape
    assert pages_per_step % PAGES_PER_BLOCK == 0
    blocks_per_step = pages_per_step // PAGES_PER_BLOCK

    def page_spec(r):
        return pl.BlockSpec((1, page, heads, hd),
                            lambda *ids: (ids[-1][first_page + step_of(*ids[:-1]) * pages_per_step + r], 0, 0, 0))

    in_specs = [page_spec(r) for r in range(pages_per_step)]
    out_spec = pl.BlockSpec((blocks_per_step, heads, hd), lambda *ids: (step_of(*ids[:-1]), 0, 0))
    return in_specs, out_spec, blocks_per_step


def _moba_prompt_kernel(pt_ref, qa_ref, qb_ref, qna_ref, qnb_ref, k_ref, v_ref, sgaa_ref, sgab_ref, *refs,
                        n_blocks, pages_per_step):
    del pt_ref
    page_refs = refs[:pages_per_step]
    o_ref, pmean_ref, kb_ref, vt_ref, kmean_ref, acc_ref = refs[pages_per_step:pages_per_step + 6]
    bufs = refs[pages_per_step + 6:]
    stage = (bufs[0:4], bufs[4:8])
    i = pl.program_id(2)
    blk = MOBA_BLOCK
    last = n_blocks - 1
    pairs = n_blocks // 2
    qscale = HEAD_DIM ** -0.5 * LOG2_E
    sga_refs = (sgaa_ref, sgab_ref)

    def past_slot(t, pair):
        first = t < pair
        return jnp.where(first, 1.0, 0.0), jnp.where(first, 0, 1), jnp.where(first, t, t - pair)

    def score_pass(pair, q_refs, buf, with_cache_stream):
        qt_ref, bias_ref, m_ref, s_ref = buf
        own = (pair, last - pair)
        for w in range(2):
            qf_t = q_refs[w][...].T
            qt_ref[w] = (qf_t * qscale).astype(BF16)
            gate = jnp.dot(kmean_ref[...], qf_t, preferred_element_type=F32, precision=lax.Precision.HIGHEST)
            row = lax.broadcasted_iota(jnp.int32, gate.shape, 0)
            gate = jnp.where(row < own[w], gate, -jnp.inf)
            sel = _topk_mask(gate, row, own[w], axis=0)
            bias_ref[w] = jnp.where(sel > 0.5, 0.0, NEG)
        k_id = lax.broadcasted_iota(jnp.int32, (blk, blk), 0)
        q_id = lax.broadcasted_iota(jnp.int32, (blk, blk), 1)
        m = []
        for w in range(2):
            s = jnp.dot(kb_ref[own[w]], qt_ref[w], preferred_element_type=F32)
            s = jnp.where(k_id <= q_id, s, NEG)
            s_ref[w] = s
            m.append(jnp.max(s, axis=0, keepdims=True))
        for t in range(last):
            fa, w, kblk = past_slot(t, pair)
            s = jnp.dot(kb_ref[kblk], qt_ref[w], preferred_element_type=F32)
            s_ref[2 + t] = s
            c = jnp.max(s, axis=0, keepdims=True) + bias_ref[w, pl.ds(kblk, 1), :]
            m[0] = jnp.maximum(m[0], c + (fa - 1.0) * -NEG)
            m[1] = jnp.maximum(m[1], c + fa * NEG)
            if with_cache_stream and t < pages_per_step // PAGES_PER_BLOCK:
                _cache_block_mean(page_refs, pmean_ref, t)
        for w in range(2):
            m_ref[w] = m[w]

    def value_pass(pair, buf):
        _, bias_ref, m_ref, s_ref = buf
        own = (pair, last - pair)
        m = [m_ref[0], m_ref[1]]
        for w in range(2):
            p = jnp.exp2((s_ref[w] - m[w]).astype(BF16))
            acc_ref[w] = jnp.dot(vt_ref[own[w]], p, preferred_element_type=F32)
        for t in range(last):
            fa, w, kblk = past_slot(t, pair)
            shift = bias_ref[w, pl.ds(kblk, 1), :] - (m[1] + fa * (m[0] - m[1]))
            p = jnp.exp2((s_ref[2 + t] + shift).astype(BF16))
            acc_ref[2 + t] = jnp.dot(vt_ref[kblk], p, preferred_element_type=F32)
        bounds = ((0, pair), (pair, last))
        for w in range(2):
            acc = lax.fori_loop(bounds[w][0], bounds[w][1], lambda t, a: a + acc_ref[2 + t], acc_ref[w])
            out_t = acc[0:HEAD_DIM, :] / acc[HEAD_DIM:HEAD_DIM + 1, :]
            o_ref[0, 0, w] = (out_t.T * sga_refs[w][...]).astype(o_ref.dtype)

    @pl.when(i == 0)
    def _():
        kmean_ref[...] = jnp.mean(k_ref[...].reshape(n_blocks, blk, HEAD_DIM), axis=1)
        for jb in range(n_blocks):
            rows = slice(jb * blk, (jb + 1) * blk)
            kb_ref[jb] = k_ref[rows, :].astype(BF16)
            vt_ref[jb, 0:HEAD_DIM, :] = v_ref[rows, :].T.astype(BF16)
            vt_ref[jb, HEAD_DIM:, :] = jnp.ones((SUM_ROWS, blk), BF16)
        score_pass(0, (qa_ref, qb_ref), stage[0], False)

    nxt = jnp.minimum(i + 1, pairs - 1)
    for parity in range(2):
        @pl.when(i % 2 == parity)
        def _():
            score_pass(nxt, (qna_ref, qnb_ref), stage[1 - parity], True)
            value_pass(i, stage[parity])


def _moba_prompt(q, k, v, sga, batch, seq, heads, page_table_flat, cache, n_stream_pages):
    n_blocks = seq // MOBA_BLOCK
    assert n_blocks % 2 == 0
    last = n_blocks - 1
    pairs = n_blocks // 2
    steps = batch * heads * pairs
    pages_per_step = n_stream_pages // steps
    assert pages_per_step * steps == n_stream_pages

    def step_id(b, h, i):
        return (b * heads + h) * pairs + i

    page_specs, pmean_spec, blocks_per_step = _page_stream_specs(cache, step_id, 0, pages_per_step)
    assert blocks_per_step <= last
    qa = pl.BlockSpec((MOBA_BLOCK, HEAD_DIM), lambda b, h, i, pt: (b * n_blocks + i, h))
    qb = pl.BlockSpec((MOBA_BLOCK, HEAD_DIM), lambda b, h, i, pt: (b * n_blocks + last - i, h))
    qna = pl.BlockSpec((MOBA_BLOCK, HEAD_DIM), lambda b, h, i, pt: (b * n_blocks + jnp.minimum(i + 1, pairs - 1), h))
    qnb = pl.BlockSpec((MOBA_BLOCK, HEAD_DIM),
                       lambda b, h, i, pt: (b * n_blocks + last - jnp.minimum(i + 1, pairs - 1), h))
    score_stage = [pltpu.VMEM((2, HEAD_DIM, MOBA_BLOCK), BF16),
                   pltpu.VMEM((2, n_blocks, MOBA_BLOCK), F32),
                   pltpu.VMEM((2, 1, MOBA_BLOCK), F32),
                   pltpu.VMEM((n_blocks + 1, MOBA_BLOCK, MOBA_BLOCK), F32)]
    kvspec = pl.BlockSpec((seq, HEAD_DIM), lambda b, h, i, pt: (b, h))
    return pl.pallas_call(
        functools.partial(_moba_prompt_kernel, n_blocks=n_blocks, pages_per_step=pages_per_step),
        grid_spec=pltpu.PrefetchScalarGridSpec(
            num_scalar_prefetch=1,
            grid=(batch, heads, pairs),
            in_specs=[qa, qb, qna, qnb, kvspec, kvspec, qa, qb] + page_specs,
            out_specs=[pl.BlockSpec((1, 1, 2, MOBA_BLOCK, HEAD_DIM), lambda b, h, i, pt: (b, i, 0, 0, h)), pmean_spec],
            scratch_shapes=[pltpu.VMEM((n_blocks, MOBA_BLOCK, HEAD_DIM), BF16),
                            pltpu.VMEM((n_blocks, HEAD_DIM + SUM_ROWS, MOBA_BLOCK), BF16),
                            pltpu.VMEM((n_blocks, HEAD_DIM), F32),
                            pltpu.VMEM((n_blocks + 1, HEAD_DIM + SUM_ROWS, MOBA_BLOCK), F32)] + score_stage * 2,
        ),
        out_shape=[jax.ShapeDtypeStruct((batch, pairs, 2, MOBA_BLOCK, q.shape[1]), BF16),
                   jax.ShapeDtypeStruct((steps * blocks_per_step,) + cache.shape[2:], F32)],
        compiler_params=pltpu.CompilerParams(
            dimension_semantics=("arbitrary", "arbitrary", "arbitrary"), vmem_limit_bytes=VMEM_LIMIT),
        name="moba_prompt",
    )(page_table_flat, q, q, q, q, k, v, sga, sga, *([cache] * pages_per_step))


def _conv_tail(c, sgc, gcn_ref, bcn_ref, wpw_ref, bpw_ref):
    mu = jnp.mean(c, axis=-1, keepdims=True)
    d = c - mu
    var = jnp.mean(d * d, axis=-1, keepdims=True)
    y = d * lax.rsqrt(var + LN_EPS) * gcn_ref[...] + bcn_ref[...]
    y = _silu(y)
    return (jnp.dot(y.astype(BF16), wpw_ref[...], preferred_element_type=F32) + bpw_ref[...]) * sgc


def _conv_prompt_kernel(u_ref, prev_ref, sgc_ref, wdw_ref, bdw_ref, gcn_ref, bcn_ref, wpw_ref, bpw_ref,
                        o_ref, ext_ref, sh_ref, c_ref, *, tiles_per_seq, rows):
    i = pl.program_id(0)
    t, w = u_ref.shape
    first = (i % tiles_per_seq) == 0
    ext_ref[0:CONV_HALO, :] = jnp.where(first, 0.0, prev_ref[...])
    ext_ref[CONV_HALO:, :] = u_ref[...]
    n_sh = sh_ref.shape[1]
    for r in range(1, SUBLANES):
        sh_ref[r - 1] = ext_ref[r:r + n_sh, :]
    off = CONV_HALO - (CONV_K - 1)
    for c in range(w // LANES):
        cs = slice(c * LANES, (c + 1) * LANES)
        for rc in range(t // rows):
            acc = jnp.broadcast_to(bdw_ref[:, cs], (rows, LANES))
            for j in range(CONV_K):
                a, r = divmod(off + j, SUBLANES)
                lo = rc * rows + a * SUBLANES
                win = ext_ref[lo:lo + rows, cs] if r == 0 else sh_ref[r - 1, lo:lo + rows, cs]
                acc = acc + win * wdw_ref[j:j + 1, cs]
            c_ref[rc * rows:(rc + 1) * rows, cs] = acc
    o_ref[...] = _conv_tail(c_ref[...], sgc_ref[...], gcn_ref, bcn_ref, wpw_ref, bpw_ref).astype(o_ref.dtype)


def _conv_prompt(u, sgc, w_dw, b_dw, g_cn, b_cn, w_pw_b, b_pw, seq, tile):
    n, w = u.shape
    halo_per_tile = tile // CONV_HALO
    full = lambda shape: pl.BlockSpec(shape, lambda i: (0, 0))
    rowspec = pl.BlockSpec((tile, w), lambda i: (i, 0))
    return pl.pallas_call(
        functools.partial(_conv_prompt_kernel, tiles_per_seq=seq // tile, rows=64),
        grid=(n // tile,),
        in_specs=[
            rowspec,
            pl.BlockSpec((CONV_HALO, w), lambda i: (jnp.maximum(i * halo_per_tile - 1, 0), 0)),
            rowspec,
            full((CONV_K, w)), full((1, w)), full((1, w)), full((1, w)), full((w, w)), full((1, w)),
        ],
        out_specs=rowspec,
        out_shape=jax.ShapeDtypeStruct((n, w), BF16),
        scratch_shapes=[pltpu.VMEM((tile + CONV_HALO, w), F32),
                        pltpu.VMEM((SUBLANES - 1, tile + CONV_HALO - SUBLANES, w), F32),
                        pltpu.VMEM((tile, w), F32)],
        compiler_params=pltpu.CompilerParams(dimension_semantics=("arbitrary",), vmem_limit_bytes=VMEM_LIMIT),
        name="conv_prompt",
    )(u, u, sgc, w_dw, b_dw, g_cn, b_cn, w_pw_b, b_pw)


def _conv_sample_kernel(state_ref, u_ref, sgc_ref, wdw_ref, bdw_ref, gcn_ref, bcn_ref, wpw_ref, bpw_ref,
                        o_ref, new_ref, c_ref):
    nb = state_ref.shape[0]
    hist = CONV_K - 1
    for b in range(nb):
        st = state_ref[b]
        u_row = u_ref[b:b + 1, :]
        c_ref[b:b + 1, :] = (jnp.sum(st * wdw_ref[0:hist, :], axis=0, keepdims=True)
                             + u_row * wdw_ref[hist:hist + 1, :] + bdw_ref[...])
        new_ref[b, 0:hist - 1, :] = state_ref[b, 1:hist, :]
        new_ref[b, hist - 1:hist, :] = u_row
    o_ref[...] = _conv_tail(c_ref[...], sgc_ref[...], gcn_ref, bcn_ref, wpw_ref, bpw_ref).astype(o_ref.dtype)


def _conv_sample(state, u, sgc, w_dw, b_dw, g_cn, b_cn, w_pw_b, b_pw):
    nb, hist, w = state.shape
    full2 = lambda shape: pl.BlockSpec(shape, lambda i: (0, 0))
    full3 = lambda shape: pl.BlockSpec(shape, lambda i: (0, 0, 0))
    return pl.pallas_call(
        _conv_sample_kernel,
        grid=(1,),
        in_specs=[full3((nb, hist, w)), full2((nb, w)), full2((nb, w)),
                  full2((CONV_K, w)), full2((1, w)), full2((1, w)), full2((1, w)), full2((w, w)), full2((1, w))],
        out_specs=[full2((nb, w)), full3((nb, hist, w))],
        out_shape=[jax.ShapeDtypeStruct((nb, w), BF16), jax.ShapeDtypeStruct((nb, hist, w), F32)],
        scratch_shapes=[pltpu.VMEM((nb, w), F32)],
        compiler_params=pltpu.CompilerParams(dimension_semantics=("arbitrary",), vmem_limit_bytes=VMEM_LIMIT),
        name="conv_sample",
    )(state, u, sgc, w_dw, b_dw, g_cn, b_cn, w_pw_b, b_pw)


def _out_kernel(pt_ref, x_ref, a_ref, c_ref, p_ref, woa_ref, woc_ref, bo_ref, g_ref, b_ref, wpg_ref, bpg_ref, wpe_ref,
                *refs, alpha, pages_per_step):
    del pt_ref
    page_refs, o_ref = refs[:pages_per_step], refs[pages_per_step]
    for a in range(pages_per_step // PAGES_PER_BLOCK):
        _cache_block_mean(page_refs, refs[pages_per_step + 1], a)
    attn = a_ref[...].reshape(c_ref.shape).astype(BF16)
    mix = (jnp.dot(attn, woa_ref[...], preferred_element_type=F32)
           + jnp.dot(c_ref[...].astype(BF16), woc_ref[...], preferred_element_type=F32) + bo_ref[...])
    t = alpha * x_ref[...] + mix
    mu = jnp.mean(t, axis=-1, keepdims=True)
    d = t - mu
    var = jnp.mean(d * d, axis=-1, keepdims=True)
    h = d * lax.rsqrt(var + LN_EPS) * g_ref[...] + b_ref[...]
    gate = _sigmoid(jnp.dot(h.astype(BF16), wpg_ref[...], preferred_element_type=F32) + bpg_ref[...])
    pe = jnp.dot(p_ref[...].astype(BF16), wpe_ref[...], preferred_element_type=F32)
    o_ref[...] = h + gate * pe


def _out_proj(x, attn, attn_spec, conv, p, w_out_b, b_out, g_ln, b_ln, w_pg_b, b_pg, w_pe_b, alpha, tm,
              page_table_flat, cache, first_page, n_stream_pages):
    n, d = x.shape
    half = conv.shape[1]
    pd = p.shape[1]
    steps = n // tm
    pages_per_step = n_stream_pages // steps
    assert pages_per_step * steps == n_stream_pages
    const = lambda shape, r=0: pl.BlockSpec(shape, lambda i, pt: (r, 0), pipeline_mode=pl.Buffered(1))
    row = lambda w: pl.BlockSpec((tm, w), lambda i, pt: (i, 0))
    out_specs, out_shape, pages = [row(d)], [jax.ShapeDtypeStruct((n, d), F32)], []
    if pages_per_step:
        pages, pmean_spec, blocks_per_step = _page_stream_specs(cache, lambda i: i, first_page, pages_per_step)
        out_specs.append(pmean_spec)
        out_shape.append(jax.ShapeDtypeStruct((steps * blocks_per_step,) + cache.shape[2:], F32))
    return pl.pallas_call(
        functools.partial(_out_kernel, alpha=alpha, pages_per_step=pages_per_step),
        grid_spec=pltpu.PrefetchScalarGridSpec(
            num_scalar_prefetch=1,
            grid=(steps,),
            in_specs=[row(d), attn_spec, row(half), row(pd),
                      const((half, d), 0), const((half, d), 1), const((1, d)), const((1, d)), const((1, d)),
                      const((d, d)), const((1, d)), const((pd, d))] + pages,
            out_specs=out_specs,
        ),
        out_shape=out_shape,
        compiler_params=pltpu.CompilerParams(dimension_semantics=("arbitrary",), vmem_limit_bytes=VMEM_LIMIT),
        name="out_proj",
    )(page_table_flat, x, attn, conv, p, w_out_b, w_out_b, b_out, g_ln, b_ln, w_pg_b, b_pg, w_pe_b,
      *([cache] * pages_per_step))


def _select_kernel(q_ref, kmean_ref, o_ref):
    g = jnp.sum(kmean_ref[0] * q_ref[...], axis=-1, keepdims=True)
    n_blocks = g.shape[0]
    idx = lax.broadcasted_iota(jnp.int32, g.shape, 0)
    for t in range(MOBA_TOPK):
        m = jnp.max(g, axis=0, keepdims=True)
        first = jnp.min(jnp.where(g == m, idx, n_blocks), axis=0, keepdims=True)
        o_ref[0, t] = jnp.broadcast_to(first[0], o_ref.shape[2:])
        g = jnp.where(idx == first, -jnp.inf, g)


def _select_sample(q4, kmean):
    n_seq, n_blocks, heads, hd = kmean.shape
    assert n_blocks >= MOBA_TOPK
    return pl.pallas_call(
        _select_kernel,
        grid=(n_seq,),
        in_specs=[pl.BlockSpec((1, heads, hd), lambda b: (b, 0, 0)),
                  pl.BlockSpec((1, n_blocks, heads, hd), lambda b: (b, 0, 0, 0))],
        out_specs=pl.BlockSpec((1, MOBA_TOPK, heads, LANES), lambda b: (b, 0, 0, 0)),
        out_shape=jax.ShapeDtypeStruct((n_seq, MOBA_TOPK, heads, LANES), jnp.int32),
        compiler_params=pltpu.CompilerParams(dimension_semantics=("arbitrary",), vmem_limit_bytes=VMEM_LIMIT),
        name="select_sample",
    )(q4, kmean)


def _moba_sample_kernel(sel_ref, pt_ref, q_ref, kn_ref, vn_ref, sga_ref, ck_ref, cv_ref, o_ref, kbuf, vbuf, sems,
                        *, n_seq, n_pages, heads):
    b = pl.program_id(0)
    n_sel = MOBA_TOPK * PAGES_PER_BLOCK
    scale = HEAD_DIM ** -0.5

    def slab_copies(seq, slot):
        copies = []
        for h in range(heads):
            for t in range(MOBA_TOPK):
                blk = sel_ref[(seq * MOBA_TOPK + t) * heads + h]
                for r in range(PAGES_PER_BLOCK):
                    page = pt_ref[seq * n_pages + blk * PAGES_PER_BLOCK + r]
                    dst = t * PAGES_PER_BLOCK + r
                    copies.append(pltpu.make_async_copy(
                        ck_ref.at[page, :, h, :], kbuf.at[slot, h, dst], sems.at[slot, 0, h]))
                    copies.append(pltpu.make_async_copy(
                        cv_ref.at[page, :, h, :], vbuf.at[slot, h, dst], sems.at[slot, 1, h]))
        return copies

    slot = b % 2

    @pl.when(b == 0)
    def _():
        for c in slab_copies(b, slot):
            c.start()

    @pl.when(b + 1 < n_seq)
    def _():
        for c in slab_copies(b + 1, 1 - slot):
            c.start()

    for c in slab_copies(b, slot):
        c.wait()

    for h in range(heads):
        hs = slice(h, h + 1)
        qh = q_ref[0, hs, :]
        kh = kbuf[slot, h].reshape(n_sel * PAGE_SIZE, HEAD_DIM).astype(BF16)
        vh = vbuf[slot, h].reshape(n_sel * PAGE_SIZE, HEAD_DIM).astype(BF16)
        s = _dot_nt(qh.astype(BF16), kh) * scale
        s_new = jnp.sum(qh * kn_ref[0, hs, :], axis=1, keepdims=True) * scale
        m = jnp.maximum(jnp.max(s, axis=1, keepdims=True), s_new)
        p = jnp.exp(s - m)
        p_new = jnp.exp(s_new - m)
        l = jnp.sum(p, axis=1, keepdims=True) + p_new
        acc = jnp.dot(p.astype(BF16), vh, preferred_element_type=F32) + p_new * vn_ref[0, hs, :]
        o_ref[0, hs, :] = acc / l * sga_ref[0, hs, :]


def _moba_sample(sel_flat, page_table_flat, q4, kn4, vn4, sga4, cache_k, cache_v, n_pages):
    n_seq, heads, hd = q4.shape
    n_sel = MOBA_TOPK * PAGES_PER_BLOCK
    vec = pl.BlockSpec((1, heads, hd), lambda b, sel, pt: (b, 0, 0))
    hbm = pl.BlockSpec(memory_space=pl.ANY)
    return pl.pallas_call(
        functools.partial(_moba_sample_kernel, n_seq=n_seq, n_pages=n_pages, heads=heads),
        grid_spec=pltpu.PrefetchScalarGridSpec(
            num_scalar_prefetch=2,
            grid=(n_seq,),
            in_specs=[vec, vec, vec, vec, hbm, hbm],
            out_specs=vec,
            scratch_shapes=[pltpu.VMEM((2, heads, n_sel, PAGE_SIZE, hd), F32),
                            pltpu.VMEM((2, heads, n_sel, PAGE_SIZE, hd), F32),
                            pltpu.SemaphoreType.DMA((2, 2, heads))],
        ),
        out_shape=jax.ShapeDtypeStruct(q4.shape, F32),
        compiler_params=pltpu.CompilerParams(dimension_semantics=("arbitrary",), vmem_limit_bytes=VMEM_LIMIT),
        name="moba_sample",
    )(sel_flat, page_table_flat, q4, kn4, vn4, sga4, cache_k, cache_v)


def kernel(x_prompt, x_sample, p_prompt, p_sample, cache_k, cache_v, state_conv, page_table, w_in, b_in, w_dw,
           b_dw, g_cn, b_cn, w_pw, b_pw, w_out, b_out, g_ln, b_ln, w_pe, w_pg, b_pg):
    depth = w_in.shape[0]
    assert depth == 1
    batch, seq, d_model = x_prompt.shape
    n_seq, dec_seq, _ = x_sample.shape
    assert dec_seq == 1
    n_phys, page, heads, head_dim = cache_k.shape[1:]
    assert head_dim == HEAD_DIM and page == PAGE_SIZE
    attn_w = heads * head_dim
    conv_w = w_pw.shape[1]
    assert conv_w == attn_w and w_in.shape[2] == 7 * attn_w
    n_pages = page_table.shape[1]
    hist = CONV_K - 1
    alpha = (2 * depth) ** 0.25
    n = batch * seq

    row2 = lambda a: a.reshape(1, -1)
    w_in_b = w_in[0].astype(BF16)
    w_pw_b = w_pw[0].astype(BF16)
    w_out_b = w_out[0].astype(BF16)
    w_pg_b = w_pg[0].astype(BF16)
    w_pe_b = w_pe[0].astype(BF16)
    b_in2, b_dw2, g_cn2, b_cn2, b_pw2 = row2(b_in[0]), row2(b_dw[0]), row2(g_cn[0]), row2(b_cn[0]), row2(b_pw[0])
    b_out2, g_ln2, b_ln2, b_pg2 = row2(b_out[0]), row2(g_ln[0]), row2(b_ln[0]), row2(b_pg[0])
    conv_w_args = (w_dw[0], b_dw2, g_cn2, b_cn2, w_pw_b, b_pw2)
    out_w_args = (w_out_b, b_out2, g_ln2, b_ln2, w_pg_b, b_pg2, w_pe_b, alpha)

    xp = x_prompt.reshape(n, d_model)
    xs = x_sample.reshape(n_seq, d_model)
    (q, k, v, sga, u, sgc), (qs, ks, vs, sgas, us, sgcs) = _in_proj_all(xp, xs, w_in_b, b_in2, tm=1024)
    pt_flat = page_table.reshape(-1)
    ck = cache_k.reshape(n_phys, page, heads, head_dim)
    cv = cache_v.reshape(n_phys, page, heads, head_dim)
    total_pages = n_seq * n_pages
    out_pages = (n // MOBA_BLOCK) * OUT_PROJ_PAGES_PER_STEP
    attn, kmean_a = _moba_prompt(q, k, v, sga, batch, seq, heads, pt_flat, ck, total_pages - out_pages)
    conv = _conv_prompt(u, sgc, *conv_w_args, seq=seq, tile=MOBA_BLOCK)
    n_blocks = seq // MOBA_BLOCK

    def paired_rows(r, pt):
        blk = r % n_blocks
        mirror = n_blocks - 1 - blk
        return (r // n_blocks, jnp.minimum(blk, mirror), jnp.where(blk > mirror, 1, 0), 0, 0)

    y_prompt, kmean_o = _out_proj(xp, attn, pl.BlockSpec((1, 1, 1, MOBA_BLOCK, attn_w), paired_rows), conv,
                                  p_prompt[0].reshape(n, -1), *out_w_args, tm=MOBA_BLOCK, page_table_flat=pt_flat,
                                  cache=ck, first_page=total_pages - out_pages, n_stream_pages=out_pages)
    kmean = jnp.concatenate([kmean_a, kmean_o]).reshape(n_seq, n_pages // PAGES_PER_BLOCK, heads, head_dim)
    conv_prompt_new = u.reshape(batch, seq, conv_w)[:, seq - hist:, :]

    conv_s, conv_sample_new = _conv_sample(state_conv[0], us, sgcs, *conv_w_args)
    as4 = lambda a: a.reshape(n_seq, heads, head_dim)
    sel = _select_sample(as4(qs), kmean)[:, :, :, 0]
    attn_s = _moba_sample(sel.reshape(-1), pt_flat, as4(qs), as4(ks), as4(vs), as4(sgas), ck, cv, n_pages)
    (y_sample,) = _out_proj(xs, attn_s.reshape(n_seq, attn_w), pl.BlockSpec((n_seq, attn_w), lambda r, pt: (r, 0)),
                            conv_s, p_sample[0].reshape(n_seq, -1), *out_w_args, tm=n_seq, page_table_flat=pt_flat,
                            cache=ck, first_page=0, n_stream_pages=0)

    kv_shape = (depth, batch, seq, heads, head_dim)
    kvs_shape = (depth, n_seq, dec_seq, heads, head_dim)
    return (y_prompt.reshape(batch, seq, d_model), y_sample.reshape(n_seq, dec_seq, d_model),
            k.reshape(kv_shape), v.reshape(kv_shape), conv_prompt_new.reshape(depth, batch, hist, conv_w),
            ks.reshape(kvs_shape), vs.reshape(kvs_shape), conv_sample_new.reshape(depth, n_seq, hist, conv_w))
```

```python
import functools

import jax
import jax.numpy as jnp
from jax import lax
from jax.experimental import pallas as pl
from jax.experimental.pallas import tpu as pltpu

F32 = jnp.float32
BF16 = jnp.bfloat16

HEAD_DIM = 128
CONV_K = 31
MOBA_BLOCK = 256
MOBA_TOPK = 3
PAGE_SIZE = 128
PAGES_PER_BLOCK = MOBA_BLOCK // PAGE_SIZE
LN_EPS = 1e-5
NEG = -1e30
LOG2_E = 1.4426950408889634
LANES = 128
SUBLANES = 8
SUM_ROWS = 16
CONV_HALO = 32
OUT_PROJ_PAGES_PER_STEP = 16
CONV_PAGES_PER_STEP = 16
VMEM_LIMIT = 60 * 1024 * 1024


def _sigmoid(z):
    return 0.5 * jnp.tanh(0.5 * z) + 0.5


def _silu(z):
    return z * _sigmoid(z)


def _dot_nt(a, b, **kw):
    return lax.dot_general(a, b, (((1,), (1,)), ((), ())), preferred_element_type=F32, **kw)


def _in_proj_kernel(x_ref, xs_ref, w_ref, b_ref, *refs, finishers, n_out):
    outs_p, outs_s, (xb_ref, *hold) = refs[:n_out], refs[n_out:2 * n_out], refs[2 * n_out:]
    j = pl.program_id(1)
    tm = x_ref.shape[0]

    @pl.when(j == 0)
    def _():
        xb_ref[0:tm, :] = x_ref[...].astype(BF16)
        xb_ref[tm:, :] = xs_ref[...].astype(BF16)

    def write(k, value):
        outs_p[k][...] = value[0:tm].astype(outs_p[k].dtype)
        outs_s[k][...] = value[tm:].astype(outs_s[k].dtype)

    for g, finish in enumerate(finishers):
        @pl.when(j == g)
        def _(finish=finish):
            finish(jnp.dot(xb_ref[...], w_ref[...], preferred_element_type=F32) + b_ref[...], write, *hold)


def _emit(k, fn=lambda z: z):
    return lambda z, write, *hold: write(k, fn(z))


def _hold(z, write, hold_ref):
    hold_ref[...] = z


def _glu_into(k):
    return lambda z, write, hold_ref: write(k, hold_ref[...] * _sigmoid(z))


QKV_GROUPS = (_emit(0), _emit(1), _emit(2))
GATE_GROUPS = (_emit(0, _silu), _hold, _glu_into(1), _emit(2, _silu))


def _in_proj(x, xs, w_b, b, tm, first_group, finishers, out_dtypes, needs_hold):
    n, d = x.shape
    ns = xs.shape[0]
    gw = w_b.shape[1] // 7
    row = pl.BlockSpec((tm, gw), lambda i, j: (i, 0))
    fixed = pl.BlockSpec((ns, gw), lambda i, j: (0, 0))
    return pl.pallas_call(
        functools.partial(_in_proj_kernel, finishers=finishers, n_out=len(out_dtypes)),
        grid=(n // tm, len(finishers)),
        in_specs=[
            pl.BlockSpec((tm, d), lambda i, j: (i, 0)),
            pl.BlockSpec((ns, d), lambda i, j: (0, 0)),
            pl.BlockSpec((d, gw), lambda i, j: (0, first_group + j)),
            pl.BlockSpec((1, gw), lambda i, j: (0, first_group + j)),
        ],
        out_specs=[row] * len(out_dtypes) + [fixed] * len(out_dtypes),
        out_shape=([jax.ShapeDtypeStruct((n, gw), dt) for dt in out_dtypes]
                   + [jax.ShapeDtypeStruct((ns, gw), dt) for dt in out_dtypes]),
        scratch_shapes=([pltpu.VMEM((tm + ns, d), BF16)]
                        + ([pltpu.VMEM((tm + ns, gw), F32)] if needs_hold else [])),
        compiler_params=pltpu.CompilerParams(
            dimension_semantics=("arbitrary", "arbitrary"), vmem_limit_bytes=VMEM_LIMIT),
        name="in_proj",
    )(x, xs, w_b, b)


def _in_proj_all(x, xs, w_b, b, tm):
    q, k, v, qs, ks, vs = _in_proj(x, xs, w_b, b, tm, 0, QKV_GROUPS, (F32, F32, F32), False)
    sga, u, sgc, sgas, us, sgcs = _in_proj(x, xs, w_b, b, tm, len(QKV_GROUPS), GATE_GROUPS, (BF16, F32, BF16), True)
    return (q, k, v, sga, u, sgc), (qs, ks, vs, sgas, us, sgcs)


def _topk_mask(g, idx, n_valid, axis):
    sel = jnp.zeros(g.shape, F32)
    big = g.shape[axis]
    for t in range(MOBA_TOPK):
        m = jnp.max(g, axis=axis, keepdims=True)
        first = jnp.min(jnp.where(g == m, idx, big), axis=axis, keepdims=True)
        hit = idx == first
        counts = jnp.where(t < n_valid, 1.0, 0.0)
        sel = jnp.maximum(sel, jnp.where(hit, counts, 0.0))
        g = jnp.where(hit, -jnp.inf, g)
    return sel


def _cache_block_mean(page_refs, pmean_ref, a):
    tot = jnp.sum(page_refs[PAGES_PER_BLOCK * a][0], axis=0)
    for r in range(1, PAGES_PER_BLOCK):
        tot = tot + jnp.sum(page_refs[PAGES_PER_BLOCK * a + r][0], axis=0)
    pmean_ref[a] = tot * (1.0 / MOBA_BLOCK)


def _page_stream_specs(cache, step_of, first_page, pages_per_step):
    n_phys, page, heads, hd = cache.shape
    assert pages_per_step % PAGES_PER_BLOCK == 0
    blocks_per_step = pages_per_step // PAGES_PER_BLOCK

    def page_spec(r):
        return pl.BlockSpec((1, page, heads, hd),
                            lambda *ids: (ids[-1][first_page + step_of(*ids[:-1]) * pages_per_step + r], 0, 0, 0))

    in_specs = [page_spec(r) for r in range(pages_per_step)]
    out_spec = pl.BlockSpec((blocks_per_step, heads, hd), lambda *ids: (step_of(*ids[:-1]), 0, 0))
    return in_specs, out_spec, blocks_per_step


def _moba_prompt_kernel(pt_ref, qa_ref, qb_ref, qna_ref, qnb_ref, k_ref, v_ref, sgaa_ref, sgab_ref, *refs,
                        n_blocks, pages_per_step):
    del pt_ref
    page_refs = refs[:pages_per_step]
    o_ref, pmean_ref, kb_ref, vt_ref, kmean_ref, acc_ref = refs[pages_per_step:pages_per_step + 6]
    bufs = refs[pages_per_step + 6:]
    stage = (bufs[0:4], bufs[4:8])
    i = pl.program_id(2)
    blk = MOBA_BLOCK
    last = n_blocks - 1
    pairs = n_blocks // 2
    qscale = HEAD_DIM ** -0.5 * LOG2_E
    sga_refs = (sgaa_ref, sgab_ref)

    def past_slot(t, pair):
        first = t < pair
        return jnp.where(first, 1.0, 0.0), jnp.where(first, 0, 1), jnp.where(first, t, t - pair)

    def score_pass(pair, q_refs, buf, with_cache_stream):
        qt_ref, bias_ref, m_ref, s_ref = buf
        own = (pair, last - pair)
        for w in range(2):
            qf_t = q_refs[w][...].T
            qt_ref[w] = (qf_t * qscale).astype(BF16)
            gate = jnp.dot(kmean_ref[...], qf_t, preferred_element_type=F32, precision=lax.Precision.HIGHEST)
            row = lax.broadcasted_iota(jnp.int32, gate.shape, 0)
            gate = jnp.where(row < own[w], gate, -jnp.inf)
            sel = _topk_mask(gate, row, own[w], axis=0)
            bias_ref[w] = jnp.where(sel > 0.5, 0.0, NEG)
        k_id = lax.broadcasted_iota(jnp.int32, (blk, blk), 0)
        q_id = lax.broadcasted_iota(jnp.int32, (blk, blk), 1)
        m = []
        for w in range(2):
            s = jnp.dot(kb_ref[own[w]], qt_ref[w], preferred_element_type=F32)
            s = jnp.where(k_id <= q_id, s, NEG)
            s_ref[w] = s
            m.append(jnp.max(s, axis=0, keepdims=True))
        for t in range(last):
            fa, w, kblk = past_slot(t, pair)
            s = jnp.dot(kb_ref[kblk], qt_ref[w], preferred_element_type=F32)
            s_ref[2 + t] = s
            c = jnp.max(s, axis=0, keepdims=True) + bias_ref[w, pl.ds(kblk, 1), :]
            m[0] = jnp.maximum(m[0], c + (fa - 1.0) * -NEG)
            m[1] = jnp.maximum(m[1], c + fa * NEG)
            if with_cache_stream and t < pages_per_step // PAGES_PER_BLOCK:
                _cache_block_mean(page_refs, pmean_ref, t)
        for w in range(2):
            m_ref[w] = m[w]

    def value_pass(pair, buf):
        _, bias_ref, m_ref, s_ref = buf
        own = (pair, last - pair)
        m = [m_ref[0], m_ref[1]]
        for w in range(2):
            p = jnp.exp2((s_ref[w] - m[w]).astype(BF16))
            acc_ref[w] = jnp.dot(vt_ref[own[w]], p, preferred_element_type=F32)
        for t in range(last):
            fa, w, kblk = past_slot(t, pair)
            shift = bias_ref[w, pl.ds(kblk, 1), :] - (m[1] + fa * (m[0] - m[1]))
            p = jnp.exp2((s_ref[2 + t] + shift).astype(BF16))
            acc_ref[2 + t] = jnp.dot(vt_ref[kblk], p, preferred_element_type=F32)
        bounds = ((0, pair), (pair, last))
        for w in range(2):
            acc = lax.fori_loop(bounds[w][0], bounds[w][1], lambda t, a: a + acc_ref[2 + t], acc_ref[w])
            out_t = acc[0:HEAD_DIM, :] / acc[HEAD_DIM:HEAD_DIM + 1, :]
            o_ref[0, 0, w] = (out_t.T * sga_refs[w][...]).astype(o_ref.dtype)

    @pl.when(i == 0)
    def _():
        kmean_ref[...] = jnp.mean(k_ref[...].reshape(n_blocks, blk, HEAD_DIM), axis=1)
        for jb in range(n_blocks):
            rows = slice(jb * blk, (jb + 1) * blk)
            kb_ref[jb] = k_ref[rows, :].astype(BF16)
            vt_ref[jb, 0:HEAD_DIM, :] = v_ref[rows, :].T.astype(BF16)
            vt_ref[jb, HEAD_DIM:, :] = jnp.ones((SUM_ROWS, blk), BF16)
        score_pass(0, (qa_ref, qb_ref), stage[0], False)

    nxt = jnp.minimum(i + 1, pairs - 1)
    for parity in range(2):
        @pl.when(i % 2 == parity)
        def _():
            score_pass(nxt, (qna_ref, qnb_ref), stage[1 - parity], True)
            value_pass(i, stage[parity])


def _moba_prompt(q, k, v, sga, batch, seq, heads, page_table_flat, cache, n_stream_pages):
    n_blocks = seq // MOBA_BLOCK
    assert n_blocks % 2 == 0
    last = n_blocks - 1
    pairs = n_blocks // 2
    steps = batch * heads * pairs
    pages_per_step = n_stream_pages // steps
    assert pages_per_step * steps == n_stream_pages

    def step_id(b, h, i):
        return (b * heads + h) * pairs + i

    page_specs, pmean_spec, blocks_per_step = _page_stream_specs(cache, step_id, 0, pages_per_step)
    assert blocks_per_step <= last
    qa = pl.BlockSpec((MOBA_BLOCK, HEAD_DIM), lambda b, h, i, pt: (b * n_blocks + i, h))
    qb = pl.BlockSpec((MOBA_BLOCK, HEAD_DIM), lambda b, h, i, pt: (b * n_blocks + last - i, h))
    qna = pl.BlockSpec((MOBA_BLOCK, HEAD_DIM), lambda b, h, i, pt: (b * n_blocks + jnp.minimum(i + 1, pairs - 1), h))
    qnb = pl.BlockSpec((MOBA_BLOCK, HEAD_DIM),
                       lambda b, h, i, pt: (b * n_blocks + last - jnp.minimum(i + 1, pairs - 1), h))
    score_stage = [pltpu.VMEM((2, HEAD_DIM, MOBA_BLOCK), BF16),
                   pltpu.VMEM((2, n_blocks, MOBA_BLOCK), F32),
                   pltpu.VMEM((2, 1, MOBA_BLOCK), F32),
                   pltpu.VMEM((n_blocks + 1, MOBA_BLOCK, MOBA_BLOCK), F32)]
    kvspec = pl.BlockSpec((seq, HEAD_DIM), lambda b, h, i, pt: (b, h))
    return pl.pallas_call(
        functools.partial(_moba_prompt_kernel, n_blocks=n_blocks, pages_per_step=pages_per_step),
        grid_spec=pltpu.PrefetchScalarGridSpec(
            num_scalar_prefetch=1,
            grid=(batch, heads, pairs),
            in_specs=[qa, qb, qna, qnb, kvspec, kvspec, qa, qb] + page_specs,
            out_specs=[pl.BlockSpec((1, 1, 2, MOBA_BLOCK, HEAD_DIM), lambda b, h, i, pt: (b, i, 0, 0, h)), pmean_spec],
            scratch_shapes=[pltpu.VMEM((n_blocks, MOBA_BLOCK, HEAD_DIM), BF16),
                            pltpu.VMEM((n_blocks, HEAD_DIM + SUM_ROWS, MOBA_BLOCK), BF16),
                            pltpu.VMEM((n_blocks, HEAD_DIM), F32),
                            pltpu.VMEM((n_blocks + 1, HEAD_DIM + SUM_ROWS, MOBA_BLOCK), F32)] + score_stage * 2,
        ),
        out_shape=[jax.ShapeDtypeStruct((batch, pairs, 2, MOBA_BLOCK, q.shape[1]), BF16),
                   jax.ShapeDtypeStruct((steps * blocks_per_step,) + cache.shape[2:], F32)],
        compiler_params=pltpu.CompilerParams(
            dimension_semantics=("arbitrary", "arbitrary", "arbitrary"), vmem_limit_bytes=VMEM_LIMIT),
        name="moba_prompt",
    )(page_table_flat, q, q, q, q, k, v, sga, sga, *([cache] * pages_per_step))


def _conv_tail(c, sgc, gcn_ref, bcn_ref, wpw_ref, bpw_ref):
    mu = jnp.mean(c, axis=-1, keepdims=True)
    d = c - mu
    var = jnp.mean(d * d, axis=-1, keepdims=True)
    y = d * lax.rsqrt(var + LN_EPS) * gcn_ref[...] + bcn_ref[...]
    y = _silu(y)
    return (jnp.dot(y.astype(BF16), wpw_ref[...], preferred_element_type=F32) + bpw_ref[...]) * sgc


def _conv_prompt_kernel(pt_ref, u_ref, prev_ref, sgc_ref, wdw_ref, bdw_ref, gcn_ref, bcn_ref, wpw_ref, bpw_ref,
                        *refs, tiles_per_seq, rows, pages_per_step):
    del pt_ref
    page_refs = refs[:pages_per_step]
    o_ref, pmean_ref, ext_ref, sh_ref, c_ref = refs[pages_per_step:]
    for a in range(pages_per_step // PAGES_PER_BLOCK):
        _cache_block_mean(page_refs, pmean_ref, a)
    i = pl.program_id(0)
    t, w = u_ref.shape
    first = (i % tiles_per_seq) == 0
    ext_ref[0:CONV_HALO, :] = jnp.where(first, 0.0, prev_ref[...])
    ext_ref[CONV_HALO:, :] = u_ref[...]
    n_sh = sh_ref.shape[1]
    for r in range(1, SUBLANES):
        sh_ref[r - 1] = ext_ref[r:r + n_sh, :]
    off = CONV_HALO - (CONV_K - 1)
    for c in range(w // LANES):
        cs = slice(c * LANES, (c + 1) * LANES)
        for rc in range(t // rows):
            acc = jnp.broadcast_to(bdw_ref[:, cs], (rows, LANES))
            for j in range(CONV_K):
                a, r = divmod(off + j, SUBLANES)
                lo = rc * rows + a * SUBLANES
                win = ext_ref[lo:lo + rows, cs] if r == 0 else sh_ref[r - 1, lo:lo + rows, cs]
                acc = acc + win * wdw_ref[j:j + 1, cs]
            c_ref[rc * rows:(rc + 1) * rows, cs] = acc
    o_ref[...] = _conv_tail(c_ref[...], sgc_ref[...], gcn_ref, bcn_ref, wpw_ref, bpw_ref).astype(o_ref.dtype)


def _conv_prompt(u, sgc, w_dw, b_dw, g_cn, b_cn, w_pw_b, b_pw, seq, tile, page_table_flat, cache, first_page,
                 n_stream_pages):
    n, w = u.shape
    steps = n // tile
    pages_per_step = n_stream_pages // steps
    assert pages_per_step * steps == n_stream_pages and pages_per_step > 0
    halo_per_tile = tile // CONV_HALO
    full = lambda shape: pl.BlockSpec(shape, lambda i, pt: (0, 0))
    rowspec = pl.BlockSpec((tile, w), lambda i, pt: (i, 0))
    pages, pmean_spec, blocks_per_step = _page_stream_specs(cache, lambda i: i, first_page, pages_per_step)
    return pl.pallas_call(
        functools.partial(_conv_prompt_kernel, tiles_per_seq=seq // tile, rows=64, pages_per_step=pages_per_step),
        grid_spec=pltpu.PrefetchScalarGridSpec(
            num_scalar_prefetch=1,
            grid=(steps,),
            in_specs=[
                rowspec,
                pl.BlockSpec((CONV_HALO, w), lambda i, pt: (jnp.maximum(i * halo_per_tile - 1, 0), 0)),
                rowspec,
                full((CONV_K, w)), full((1, w)), full((1, w)), full((1, w)), full((w, w)), full((1, w)),
            ] + pages,
            out_specs=[rowspec, pmean_spec],
            scratch_shapes=[pltpu.VMEM((tile + CONV_HALO, w), F32),
                            pltpu.VMEM((SUBLANES - 1, tile + CONV_HALO - SUBLANES, w), F32),
                            pltpu.VMEM((tile, w), F32)],
        ),
        out_shape=[jax.ShapeDtypeStruct((n, w), BF16),
                   jax.ShapeDtypeStruct((steps * blocks_per_step,) + cache.shape[2:], F32)],
        compiler_params=pltpu.CompilerParams(dimension_semantics=("arbitrary",), vmem_limit_bytes=VMEM_LIMIT),
        name="conv_prompt",
    )(page_table_flat, u, u, sgc, w_dw, b_dw, g_cn, b_cn, w_pw_b, b_pw, *([cache] * pages_per_step))


def _conv_sample_kernel(state_ref, u_ref, sgc_ref, wdw_ref, bdw_ref, gcn_ref, bcn_ref, wpw_ref, bpw_ref,
                        o_ref, new_ref, c_ref):
    nb = state_ref.shape[0]
    hist = CONV_K - 1
    for b in range(nb):
        st = state_ref[b]
        u_row = u_ref[b:b + 1, :]
        c_ref[b:b + 1, :] = (jnp.sum(st * wdw_ref[0:hist, :], axis=0, keepdims=True)
                             + u_row * wdw_ref[hist:hist + 1, :] + bdw_ref[...])
        new_ref[b, 0:hist - 1, :] = state_ref[b, 1:hist, :]
        new_ref[b, hist - 1:hist, :] = u_row
    o_ref[...] = _conv_tail(c_ref[...], sgc_ref[...], gcn_ref, bcn_ref, wpw_ref, bpw_ref).astype(o_ref.dtype)


def _conv_sample(state, u, sgc, w_dw, b_dw, g_cn, b_cn, w_pw_b, b_pw):
    nb, hist, w = state.shape
    full2 = lambda shape: pl.BlockSpec(shape, lambda i: (0, 0))
    full3 = lambda shape: pl.BlockSpec(shape, lambda i: (0, 0, 0))
    return pl.pallas_call(
        _conv_sample_kernel,
        grid=(1,),
        in_specs=[full3((nb, hist, w)), full2((nb, w)), full2((nb, w)),
                  full2((CONV_K, w)), full2((1, w)), full2((1, w)), full2((1, w)), full2((w, w)), full2((1, w))],
        out_specs=[full2((nb, w)), full3((nb, hist, w))],
        out_shape=[jax.ShapeDtypeStruct((nb, w), BF16), jax.ShapeDtypeStruct((nb, hist, w), F32)],
        scratch_shapes=[pltpu.VMEM((nb, w), F32)],
        compiler_params=pltpu.CompilerParams(dimension_semantics=("arbitrary",), vmem_limit_bytes=VMEM_LIMIT),
        name="conv_sample",
    )(state, u, sgc, w_dw, b_dw, g_cn, b_cn, w_pw_b, b_pw)


def _out_kernel(pt_ref, x_ref, a_ref, c_ref, p_ref, woa_ref, woc_ref, bo_ref, g_ref, b_ref, wpg_ref, bpg_ref, wpe_ref,
                *refs, alpha, pages_per_step):
    del pt_ref
    page_refs, o_ref = refs[:pages_per_step], refs[pages_per_step]
    for a in range(pages_per_step // PAGES_PER_BLOCK):
        _cache_block_mean(page_refs, refs[pages_per_step + 1], a)
    attn = a_ref[...].reshape(c_ref.shape).astype(BF16)
    mix = (jnp.dot(attn, woa_ref[...], preferred_element_type=F32)
           + jnp.dot(c_ref[...].astype(BF16), woc_ref[...], preferred_element_type=F32) + bo_ref[...])
    t = alpha * x_ref[...] + mix
    mu = jnp.mean(t, axis=-1, keepdims=True)
    d = t - mu
    var = jnp.mean(d * d, axis=-1, keepdims=True)
    h = d * lax.rsqrt(var + LN_EPS) * g_ref[...] + b_ref[...]
    gate = _sigmoid(jnp.dot(h.astype(BF16), wpg_ref[...], preferred_element_type=F32) + bpg_ref[...])
    pe = jnp.dot(p_ref[...].astype(BF16), wpe_ref[...], preferred_element_type=F32)
    o_ref[...] = h + gate * pe


def _out_proj(x, attn, attn_spec, conv, p, w_out_b, b_out, g_ln, b_ln, w_pg_b, b_pg, w_pe_b, alpha, tm,
              page_table_flat, cache, first_page, n_stream_pages):
    n, d = x.shape
    half = conv.shape[1]
    pd = p.shape[1]
    steps = n // tm
    pages_per_step = n_stream_pages // steps
    assert pages_per_step * steps == n_stream_pages
    const = lambda shape, r=0: pl.BlockSpec(shape, lambda i, pt: (r, 0), pipeline_mode=pl.Buffered(1))
    row = lambda w: pl.BlockSpec((tm, w), lambda i, pt: (i, 0))
    out_specs, out_shape, pages = [row(d)], [jax.ShapeDtypeStruct((n, d), F32)], []
    if pages_per_step:
        pages, pmean_spec, blocks_per_step = _page_stream_specs(cache, lambda i: i, first_page, pages_per_step)
        out_specs.append(pmean_spec)
        out_shape.append(jax.ShapeDtypeStruct((steps * blocks_per_step,) + cache.shape[2:], F32))
    return pl.pallas_call(
        functools.partial(_out_kernel, alpha=alpha, pages_per_step=pages_per_step),
        grid_spec=pltpu.PrefetchScalarGridSpec(
            num_scalar_prefetch=1,
            grid=(steps,),
            in_specs=[row(d), attn_spec, row(half), row(pd),
                      const((half, d), 0), const((half, d), 1), const((1, d)), const((1, d)), const((1, d)),
                      const((d, d)), const((1, d)), const((pd, d))] + pages,
            out_specs=out_specs,
        ),
        out_shape=out_shape,
        compiler_params=pltpu.CompilerParams(dimension_semantics=("arbitrary",), vmem_limit_bytes=VMEM_LIMIT),
        name="out_proj",
    )(page_table_flat, x, attn, conv, p, w_out_b, w_out_b, b_out, g_ln, b_ln, w_pg_b, b_pg, w_pe_b,
      *([cache] * pages_per_step))


def _select_kernel(q_ref, kmean_ref, o_ref):
    g = jnp.sum(kmean_ref[0] * q_ref[...], axis=-1, keepdims=True)
    n_blocks = g.shape[0]
    idx = lax.broadcasted_iota(jnp.int32, g.shape, 0)
    for t in range(MOBA_TOPK):
        m = jnp.max(g, axis=0, keepdims=True)
        first = jnp.min(jnp.where(g == m, idx, n_blocks), axis=0, keepdims=True)
        o_ref[0, t] = jnp.broadcast_to(first[0], o_ref.shape[2:])
        g = jnp.where(idx == first, -jnp.inf, g)


def _select_sample(q4, kmean):
    n_seq, n_blocks, heads, hd = kmean.shape
    assert n_blocks >= MOBA_TOPK
    return pl.pallas_call(
        _select_kernel,
        grid=(n_seq,),
        in_specs=[pl.BlockSpec((1, heads, hd), lambda b: (b, 0, 0)),
                  pl.BlockSpec((1, n_blocks, heads, hd), lambda b: (b, 0, 0, 0))],
        out_specs=pl.BlockSpec((1, MOBA_TOPK, heads, LANES), lambda b: (b, 0, 0, 0)),
        out_shape=jax.ShapeDtypeStruct((n_seq, MOBA_TOPK, heads, LANES), jnp.int32),
        compiler_params=pltpu.CompilerParams(dimension_semantics=("arbitrary",), vmem_limit_bytes=VMEM_LIMIT),
        name="select_sample",
    )(q4, kmean)


def _moba_sample_kernel(sel_ref, pt_ref, q_ref, kn_ref, vn_ref, sga_ref, ck_ref, cv_ref, o_ref, kbuf, vbuf, sems,
                        *, n_seq, n_pages, heads):
    b = pl.program_id(0)
    n_sel = MOBA_TOPK * PAGES_PER_BLOCK
    scale = HEAD_DIM ** -0.5

    def slab_copies(seq, slot):
        copies = []
        for h in range(heads):
            for t in range(MOBA_TOPK):
                blk = sel_ref[(seq * MOBA_TOPK + t) * heads + h]
                for r in range(PAGES_PER_BLOCK):
                    page = pt_ref[seq * n_pages + blk * PAGES_PER_BLOCK + r]
                    dst = t * PAGES_PER_BLOCK + r
                    copies.append(pltpu.make_async_copy(
                        ck_ref.at[page, :, h, :], kbuf.at[slot, h, dst], sems.at[slot, 0, h]))
                    copies.append(pltpu.make_async_copy(
                        cv_ref.at[page, :, h, :], vbuf.at[slot, h, dst], sems.at[slot, 1, h]))
        return copies

    slot = b % 2

    @pl.when(b == 0)
    def _():
        for c in slab_copies(b, slot):
            c.start()

    @pl.when(b + 1 < n_seq)
    def _():
        for c in slab_copies(b + 1, 1 - slot):
            c.start()

    for c in slab_copies(b, slot):
        c.wait()

    for h in range(heads):
        hs = slice(h, h + 1)
        qh = q_ref[0, hs, :]
        kh = kbuf[slot, h].reshape(n_sel * PAGE_SIZE, HEAD_DIM).astype(BF16)
        vh = vbuf[slot, h].reshape(n_sel * PAGE_SIZE, HEAD_DIM).astype(BF16)
        s = _dot_nt(qh.astype(BF16), kh) * scale
        s_new = jnp.sum(qh * kn_ref[0, hs, :], axis=1, keepdims=True) * scale
        m = jnp.maximum(jnp.max(s, axis=1, keepdims=True), s_new)
        p = jnp.exp(s - m)
        p_new = jnp.exp(s_new - m)
        l = jnp.sum(p, axis=1, keepdims=True) + p_new
        acc = jnp.dot(p.astype(BF16), vh, preferred_element_type=F32) + p_new * vn_ref[0, hs, :]
        o_ref[0, hs, :] = acc / l * sga_ref[0, hs, :]


def _moba_sample(sel_flat, page_table_flat, q4, kn4, vn4, sga4, cache_k, cache_v, n_pages):
    n_seq, heads, hd = q4.shape
    n_sel = MOBA_TOPK * PAGES_PER_BLOCK
    vec = pl.BlockSpec((1, heads, hd), lambda b, sel, pt: (b, 0, 0))
    hbm = pl.BlockSpec(memory_space=pl.ANY)
    return pl.pallas_call(
        functools.partial(_moba_sample_kernel, n_seq=n_seq, n_pages=n_pages, heads=heads),
        grid_spec=pltpu.PrefetchScalarGridSpec(
            num_scalar_prefetch=2,
            grid=(n_seq,),
            in_specs=[vec, vec, vec, vec, hbm, hbm],
            out_specs=vec,
            scratch_shapes=[pltpu.VMEM((2, heads, n_sel, PAGE_SIZE, hd), F32),
                            pltpu.VMEM((2, heads, n_sel, PAGE_SIZE, hd), F32),
                            pltpu.SemaphoreType.DMA((2, 2, heads))],
        ),
        out_shape=jax.ShapeDtypeStruct(q4.shape, F32),
        compiler_params=pltpu.CompilerParams(dimension_semantics=("arbitrary",), vmem_limit_bytes=VMEM_LIMIT),
        name="moba_sample",
    )(sel_flat, page_table_flat, q4, kn4, vn4, sga4, cache_k, cache_v)


def kernel(x_prompt, x_sample, p_prompt, p_sample, cache_k, cache_v, state_conv, page_table, w_in, b_in, w_dw,
           b_dw, g_cn, b_cn, w_pw, b_pw, w_out, b_out, g_ln, b_ln, w_pe, w_pg, b_pg):
    depth = w_in.shape[0]
    assert depth == 1
    batch, seq, d_model = x_prompt.shape
    n_seq, dec_seq, _ = x_sample.shape
    assert dec_seq == 1
    n_phys, page, heads, head_dim = cache_k.shape[1:]
    assert head_dim == HEAD_DIM and page == PAGE_SIZE
    attn_w = heads * head_dim
    conv_w = w_pw.shape[1]
    assert conv_w == attn_w and w_in.shape[2] == 7 * attn_w
    n_pages = page_table.shape[1]
    hist = CONV_K - 1
    alpha = (2 * depth) ** 0.25
    n = batch * seq

    row2 = lambda a: a.reshape(1, -1)
    w_in_b = w_in[0].astype(BF16)
    w_pw_b = w_pw[0].astype(BF16)
    w_out_b = w_out[0].astype(BF16)
    w_pg_b = w_pg[0].astype(BF16)
    w_pe_b = w_pe[0].astype(BF16)
    b_in2, b_dw2, g_cn2, b_cn2, b_pw2 = row2(b_in[0]), row2(b_dw[0]), row2(g_cn[0]), row2(b_cn[0]), row2(b_pw[0])
    b_out2, g_ln2, b_ln2, b_pg2 = row2(b_out[0]), row2(g_ln[0]), row2(b_ln[0]), row2(b_pg[0])
    conv_w_args = (w_dw[0], b_dw2, g_cn2, b_cn2, w_pw_b, b_pw2)
    out_w_args = (w_out_b, b_out2, g_ln2, b_ln2, w_pg_b, b_pg2, w_pe_b, alpha)

    xp = x_prompt.reshape(n, d_model)
    xs = x_sample.reshape(n_seq, d_model)
    (q, k, v, sga, u, sgc), (qs, ks, vs, sgas, us, sgcs) = _in_proj_all(xp, xs, w_in_b, b_in2, tm=1024)
    pt_flat = page_table.reshape(-1)
    ck = cache_k.reshape(n_phys, page, heads, head_dim)
    cv = cache_v.reshape(n_phys, page, heads, head_dim)
    total_pages = n_seq * n_pages
    out_pages = (n // MOBA_BLOCK) * OUT_PROJ_PAGES_PER_STEP
    conv_pages = (n // MOBA_BLOCK) * CONV_PAGES_PER_STEP
    attn_pages = total_pages - out_pages - conv_pages
    attn, kmean_a = _moba_prompt(q, k, v, sga, batch, seq, heads, pt_flat, ck, attn_pages)
    conv, kmean_c = _conv_prompt(u, sgc, *conv_w_args, seq=seq, tile=MOBA_BLOCK, page_table_flat=pt_flat, cache=ck,
                                 first_page=attn_pages, n_stream_pages=conv_pages)
    n_blocks = seq // MOBA_BLOCK

    def paired_rows(r, pt):
        blk = r % n_blocks
        mirror = n_blocks - 1 - blk
        return (r // n_blocks, jnp.minimum(blk, mirror), jnp.where(blk > mirror, 1, 0), 0, 0)

    y_prompt, kmean_o = _out_proj(xp, attn, pl.BlockSpec((1, 1, 1, MOBA_BLOCK, attn_w), paired_rows), conv,
                                  p_prompt[0].reshape(n, -1), *out_w_args, tm=MOBA_BLOCK, page_table_flat=pt_flat,
                                  cache=ck, first_page=total_pages - out_pages, n_stream_pages=out_pages)
    kmean = jnp.concatenate([kmean_a, kmean_c, kmean_o]).reshape(n_seq, n_pages // PAGES_PER_BLOCK, heads, head_dim)
    conv_prompt_new = u.reshape(batch, seq, conv_w)[:, seq - hist:, :]

    conv_s, conv_sample_new = _conv_sample(state_conv[0], us, sgcs, *conv_w_args)
    as4 = lambda a: a.reshape(n_seq, heads, head_dim)
    sel = _select_sample(as4(qs), kmean)[:, :, :, 0]
    attn_s = _moba_sample(sel.reshape(-1), pt_flat, as4(qs), as4(ks), as4(vs), as4(sgas), ck, cv, n_pages)
    (y_sample,) = _out_proj(xs, attn_s.reshape(n_seq, attn_w), pl.BlockSpec((n_seq, attn_w), lambda r, pt: (r, 0)),
                            conv_s, p_sample[0].reshape(n_seq, -1), *out_w_args, tm=n_seq, page_table_flat=pt_flat,
                            cache=ck, first_page=0, n_stream_pages=0)

    kv_shape = (depth, batch, seq, heads, head_dim)
    kvs_shape = (depth, n_seq, dec_seq, heads, head_dim)
    return (y_prompt.reshape(batch, seq, d_model), y_sample.reshape(n_seq, dec_seq, d_model),
            k.reshape(kv_shape), v.reshape(kv_shape), conv_prompt_new.reshape(depth, batch, hist, conv_w),
            ks.reshape(kvs_shape), vs.reshape(kvs_shape), conv_sample_new.reshape(depth, n_seq, hist, conv_w))
```

```python
import functools

import jax
import jax.numpy as jnp
from jax import lax
from jax.experimental import pallas as pl
from jax.experimental.pallas import tpu as pltpu

F32 = jnp.float32
BF16 = jnp.bfloat16

HEAD_DIM = 128
CONV_K = 31
MOBA_BLOCK = 256
MOBA_TOPK = 3
PAGE_SIZE = 128
PAGES_PER_BLOCK = MOBA_BLOCK // PAGE_SIZE
LN_EPS = 1e-5
NEG = -1e30
LOG2_E = 1.4426950408889634
LANES = 128
SUBLANES = 8
SUM_ROWS = 16
CONV_HALO = 32
OUT_PROJ_PAGES_PER_STEP = 16
CONV_PAGES_PER_STEP = 16
VMEM_LIMIT = 60 * 1024 * 1024


def _sigmoid(z):
    return 0.5 * jnp.tanh(0.5 * z) + 0.5


def _silu(z):
    return z * _sigmoid(z)


def _dot_nt(a, b, **kw):
    return lax.dot_general(a, b, (((1,), (1,)), ((), ())), preferred_element_type=F32, **kw)


def _in_proj_kernel(x_ref, xs_ref, w_ref, b_ref, *refs, finishers, n_out):
    outs_p, outs_s, (xb_ref, *hold) = refs[:n_out], refs[n_out:2 * n_out], refs[2 * n_out:]
    j = pl.program_id(1)
    tm = x_ref.shape[0]

    @pl.when(j == 0)
    def _():
        xb_ref[0:tm, :] = x_ref[...].astype(BF16)
        xb_ref[tm:, :] = xs_ref[...].astype(BF16)

    def write(k, value):
        outs_p[k][...] = value[0:tm].astype(outs_p[k].dtype)
        outs_s[k][...] = value[tm:].astype(outs_s[k].dtype)

    for g, finish in enumerate(finishers):
        @pl.when(j == g)
        def _(finish=finish):
            finish(jnp.dot(xb_ref[...], w_ref[...], preferred_element_type=F32) + b_ref[...], write, *hold)


def _emit(k, fn=lambda z: z):
    return lambda z, write, *hold: write(k, fn(z))


def _hold(z, write, hold_ref):
    hold_ref[...] = z


def _glu_into(k):
    return lambda z, write, hold_ref: write(k, hold_ref[...] * _sigmoid(z))


QKV_GROUPS = (_emit(0), _emit(1), _emit(2))
GATE_GROUPS = (_emit(0, _silu), _hold, _glu_into(1), _emit(2, _silu))


def _in_proj(x, xs, w_b, b, tm, first_group, finishers, out_dtypes, needs_hold):
    n, d = x.shape
    ns = xs.shape[0]
    gw = w_b.shape[1] // 7
    row = pl.BlockSpec((tm, gw), lambda i, j: (i, 0))
    fixed = pl.BlockSpec((ns, gw), lambda i, j: (0, 0))
    return pl.pallas_call(
        functools.partial(_in_proj_kernel, finishers=finishers, n_out=len(out_dtypes)),
        grid=(n // tm, len(finishers)),
        in_specs=[
            pl.BlockSpec((tm, d), lambda i, j: (i, 0)),
            pl.BlockSpec((ns, d), lambda i, j: (0, 0)),
            pl.BlockSpec((d, gw), lambda i, j: (0, first_group + j)),
            pl.BlockSpec((1, gw), lambda i, j: (0, first_group + j)),
        ],
        out_specs=[row] * len(out_dtypes) + [fixed] * len(out_dtypes),
        out_shape=([jax.ShapeDtypeStruct((n, gw), dt) for dt in out_dtypes]
                   + [jax.ShapeDtypeStruct((ns, gw), dt) for dt in out_dtypes]),
        scratch_shapes=([pltpu.VMEM((tm + ns, d), BF16)]
                        + ([pltpu.VMEM((tm + ns, gw), F32)] if needs_hold else [])),
        compiler_params=pltpu.CompilerParams(
            dimension_semantics=("arbitrary", "arbitrary"), vmem_limit_bytes=VMEM_LIMIT),
        name="in_proj",
    )(x, xs, w_b, b)


def _in_proj_all(x, xs, w_b, b, tm):
    q, k, v, qs, ks, vs = _in_proj(x, xs, w_b, b, tm, 0, QKV_GROUPS, (F32, F32, F32), False)
    sga, u, sgc, sgas, us, sgcs = _in_proj(x, xs, w_b, b, tm, len(QKV_GROUPS), GATE_GROUPS, (BF16, F32, BF16), True)
    return (q, k, v, sga, u, sgc), (qs, ks, vs, sgas, us, sgcs)


def _topk_mask(g, idx, n_valid, axis):
    sel = jnp.zeros(g.shape, F32)
    big = g.shape[axis]
    for t in range(MOBA_TOPK):
        m = jnp.max(g, axis=axis, keepdims=True)
        first = jnp.min(jnp.where(g == m, idx, big), axis=axis, keepdims=True)
        hit = idx == first
        counts = jnp.where(t < n_valid, 1.0, 0.0)
        sel = jnp.maximum(sel, jnp.where(hit, counts, 0.0))
        g = jnp.where(hit, -jnp.inf, g)
    return sel


def _cache_block_mean(page_refs, pmean_ref, a):
    tot = jnp.sum(page_refs[PAGES_PER_BLOCK * a][0], axis=0)
    for r in range(1, PAGES_PER_BLOCK):
        tot = tot + jnp.sum(page_refs[PAGES_PER_BLOCK * a + r][0], axis=0)
    pmean_ref[a] = tot * (1.0 / MOBA_BLOCK)


def _page_stream_specs(cache, step_of, first_page, pages_per_step):
    n_phys, page, heads, hd = cache.shape
    assert pages_per_step % PAGES_PER_BLOCK == 0
    blocks_per_step = pages_per_step // PAGES_PER_BLOCK

    def page_spec(r):
        return pl.BlockSpec((1, page, heads, hd),
                            lambda *ids: (ids[-1][first_page + step_of(*ids[:-1]) * pages_per_step + r], 0, 0, 0))

    in_specs = [page_spec(r) for r in range(pages_per_step)]
    out_spec = pl.BlockSpec((blocks_per_step, heads, hd), lambda *ids: (step_of(*ids[:-1]), 0, 0))
    return in_specs, out_spec, blocks_per_step


def _moba_prompt_kernel(pt_ref, qa_ref, qb_ref, qna_ref, qnb_ref, k_ref, v_ref, sgaa_ref, sgab_ref, *refs,
                        n_blocks, pages_per_step):
    del pt_ref
    page_refs = refs[:pages_per_step]
    o_ref, pmean_ref, kb_ref, vt_ref, kmean_ref, acc_ref = refs[pages_per_step:pages_per_step + 6]
    bufs = refs[pages_per_step + 6:]
    stage = (bufs[0:4], bufs[4:8])
    i = pl.program_id(2)
    blk = MOBA_BLOCK
    last = n_blocks - 1
    pairs = n_blocks // 2
    qscale = HEAD_DIM ** -0.5 * LOG2_E
    sga_refs = (sgaa_ref, sgab_ref)

    def past_slot(t, pair):
        first = t < pair
        return jnp.where(first, 1.0, 0.0), jnp.where(first, 0, 1), jnp.where(first, t, t - pair)

    def score_pass(pair, q_refs, buf, with_cache_stream):
        qt_ref, bias_ref, m_ref, s_ref = buf
        own = (pair, last - pair)
        for w in range(2):
            qf_t = q_refs[w][...].T
            qt_ref[w] = (qf_t * qscale).astype(BF16)
            gate = jnp.dot(kmean_ref[...], qf_t, preferred_element_type=F32, precision=lax.Precision.HIGHEST)
            row = lax.broadcasted_iota(jnp.int32, gate.shape, 0)
            gate = jnp.where(row < own[w], gate, -jnp.inf)
            sel = _topk_mask(gate, row, own[w], axis=0)
            bias_ref[w] = jnp.where(sel > 0.5, 0.0, NEG)
        k_id = lax.broadcasted_iota(jnp.int32, (blk, blk), 0)
        q_id = lax.broadcasted_iota(jnp.int32, (blk, blk), 1)
        m = []
        for w in range(2):
            s = jnp.dot(kb_ref[own[w]], qt_ref[w], preferred_element_type=F32)
            s = jnp.where(k_id <= q_id, s, NEG)
            s_ref[w] = s
            m.append(jnp.max(s, axis=0, keepdims=True))
        for t in range(last):
            fa, w, kblk = past_slot(t, pair)
            s = jnp.dot(kb_ref[kblk], qt_ref[w], preferred_element_type=F32)
            s_ref[2 + t] = s
            c = jnp.max(s, axis=0, keepdims=True) + bias_ref[w, pl.ds(kblk, 1), :]
            m[0] = jnp.maximum(m[0], c + (fa - 1.0) * -NEG)
            m[1] = jnp.maximum(m[1], c + fa * NEG)
            if with_cache_stream and t < pages_per_step // PAGES_PER_BLOCK:
                _cache_block_mean(page_refs, pmean_ref, t)
        for w in range(2):
            m_ref[w] = m[w]

    def value_pass(pair, buf):
        _, bias_ref, m_ref, s_ref = buf
        own = (pair, last - pair)
        m = [m_ref[0], m_ref[1]]
        for w in range(2):
            p = jnp.exp2((s_ref[w] - m[w]).astype(BF16))
            acc_ref[w] = jnp.dot(vt_ref[own[w]], p, preferred_element_type=F32)
        for t in range(last):
            fa, w, kblk = past_slot(t, pair)
            shift = bias_ref[w, pl.ds(kblk, 1), :] - (m[1] + fa * (m[0] - m[1]))
            p = jnp.exp2((s_ref[2 + t] + shift).astype(BF16))
            acc_ref[2 + t] = jnp.dot(vt_ref[kblk], p, preferred_element_type=F32)
        bounds = ((0, pair), (pair, last))
        for w in range(2):
            acc = lax.fori_loop(bounds[w][0], bounds[w][1], lambda t, a: a + acc_ref[2 + t], acc_ref[w])
            out_t = acc[0:HEAD_DIM, :] / acc[HEAD_DIM:HEAD_DIM + 1, :]
            o_ref[0, 0, w] = (out_t.T * sga_refs[w][...]).astype(o_ref.dtype)

    @pl.when(i == 0)
    def _():
        kmean_ref[...] = jnp.mean(k_ref[...].reshape(n_blocks, blk, HEAD_DIM), axis=1)
        for jb in range(n_blocks):
            rows = slice(jb * blk, (jb + 1) * blk)
            kb_ref[jb] = k_ref[rows, :].astype(BF16)
            vt_ref[jb, 0:HEAD_DIM, :] = v_ref[rows, :].T.astype(BF16)
            vt_ref[jb, HEAD_DIM:, :] = jnp.ones((SUM_ROWS, blk), BF16)
        score_pass(0, (qa_ref, qb_ref), stage[0], False)

    nxt = jnp.minimum(i + 1, pairs - 1)
    for parity in range(2):
        @pl.when(i % 2 == parity)
        def _():
            score_pass(nxt, (qna_ref, qnb_ref), stage[1 - parity], True)
            value_pass(i, stage[parity])


def _moba_prompt(q, k, v, sga, batch, seq, heads, page_table_flat, cache, n_stream_pages):
    n_blocks = seq // MOBA_BLOCK
    assert n_blocks % 2 == 0
    last = n_blocks - 1
    pairs = n_blocks // 2
    steps = batch * heads * pairs
    pages_per_step = n_stream_pages // steps
    assert pages_per_step * steps == n_stream_pages

    def step_id(b, h, i):
        return (b * heads + h) * pairs + i

    page_specs, pmean_spec, blocks_per_step = _page_stream_specs(cache, step_id, 0, pages_per_step)
    assert blocks_per_step <= last
    qa = pl.BlockSpec((MOBA_BLOCK, HEAD_DIM), lambda b, h, i, pt: (b * n_blocks + i, h))
    qb = pl.BlockSpec((MOBA_BLOCK, HEAD_DIM), lambda b, h, i, pt: (b * n_blocks + last - i, h))
    qna = pl.BlockSpec((MOBA_BLOCK, HEAD_DIM), lambda b, h, i, pt: (b * n_blocks + jnp.minimum(i + 1, pairs - 1), h))
    qnb = pl.BlockSpec((MOBA_BLOCK, HEAD_DIM),
                       lambda b, h, i, pt: (b * n_blocks + last - jnp.minimum(i + 1, pairs - 1), h))
    score_stage = [pltpu.VMEM((2, HEAD_DIM, MOBA_BLOCK), BF16),
                   pltpu.VMEM((2, n_blocks, MOBA_BLOCK), F32),
                   pltpu.VMEM((2, 1, MOBA_BLOCK), F32),
                   pltpu.VMEM((n_blocks + 1, MOBA_BLOCK, MOBA_BLOCK), F32)]
    kvspec = pl.BlockSpec((seq, HEAD_DIM), lambda b, h, i, pt: (b, h))
    return pl.pallas_call(
        functools.partial(_moba_prompt_kernel, n_blocks=n_blocks, pages_per_step=pages_per_step),
        grid_spec=pltpu.PrefetchScalarGridSpec(
            num_scalar_prefetch=1,
            grid=(batch, heads, pairs),
            in_specs=[qa, qb, qna, qnb, kvspec, kvspec, qa, qb] + page_specs,
            out_specs=[pl.BlockSpec((1, 1, 2, MOBA_BLOCK, HEAD_DIM), lambda b, h, i, pt: (b, i, 0, 0, h)), pmean_spec],
            scratch_shapes=[pltpu.VMEM((n_blocks, MOBA_BLOCK, HEAD_DIM), BF16),
                            pltpu.VMEM((n_blocks, HEAD_DIM + SUM_ROWS, MOBA_BLOCK), BF16),
                            pltpu.VMEM((n_blocks, HEAD_DIM), F32),
                            pltpu.VMEM((n_blocks + 1, HEAD_DIM + SUM_ROWS, MOBA_BLOCK), F32)] + score_stage * 2,
        ),
        out_shape=[jax.ShapeDtypeStruct((batch, pairs, 2, MOBA_BLOCK, q.shape[1]), BF16),
                   jax.ShapeDtypeStruct((steps * blocks_per_step,) + cache.shape[2:], F32)],
        compiler_params=pltpu.CompilerParams(
            dimension_semantics=("arbitrary", "arbitrary", "arbitrary"), vmem_limit_bytes=VMEM_LIMIT),
        name="moba_prompt",
    )(page_table_flat, q, q, q, q, k, v, sga, sga, *([cache] * pages_per_step))


def _conv_tail(c, sgc, gcn_ref, bcn_ref, wpw_ref, bpw_ref):
    mu = jnp.mean(c, axis=-1, keepdims=True)
    d = c - mu
    var = jnp.mean(d * d, axis=-1, keepdims=True)
    y = d * lax.rsqrt(var + LN_EPS) * gcn_ref[...] + bcn_ref[...]
    y = _silu(y)
    return (jnp.dot(y.astype(BF16), wpw_ref[...], preferred_element_type=F32) + bpw_ref[...]) * sgc


def _conv_prompt_kernel(pt_ref, u_ref, prev_ref, sgc_ref, wdw_ref, bdw_ref, gcn_ref, bcn_ref, wpw_ref, bpw_ref,
                        *refs, tiles_per_seq, rows, pages_per_step):
    del pt_ref
    page_refs = refs[:pages_per_step]
    o_ref, pmean_ref, ext_ref, sh_ref, c_ref = refs[pages_per_step:]
    for a in range(pages_per_step // PAGES_PER_BLOCK):
        _cache_block_mean(page_refs, pmean_ref, a)
    i = pl.program_id(0)
    t, w = u_ref.shape
    first = (i % tiles_per_seq) == 0
    ext_ref[0:CONV_HALO, :] = jnp.where(first, 0.0, prev_ref[...])
    ext_ref[CONV_HALO:, :] = u_ref[...]
    n_sh = sh_ref.shape[1]
    for r in range(1, SUBLANES):
        sh_ref[r - 1] = ext_ref[r:r + n_sh, :]
    off = CONV_HALO - (CONV_K - 1)
    for c in range(w // LANES):
        cs = slice(c * LANES, (c + 1) * LANES)
        for rc in range(t // rows):
            acc = jnp.broadcast_to(bdw_ref[:, cs], (rows, LANES))
            for j in range(CONV_K):
                a, r = divmod(off + j, SUBLANES)
                lo = rc * rows + a * SUBLANES
                win = ext_ref[lo:lo + rows, cs] if r == 0 else sh_ref[r - 1, lo:lo + rows, cs]
                acc = acc + win * wdw_ref[j:j + 1, cs]
            c_ref[rc * rows:(rc + 1) * rows, cs] = acc
    o_ref[...] = _conv_tail(c_ref[...], sgc_ref[...], gcn_ref, bcn_ref, wpw_ref, bpw_ref).astype(o_ref.dtype)


def _conv_prompt(u, sgc, w_dw, b_dw, g_cn, b_cn, w_pw_b, b_pw, seq, tile, page_table_flat, cache, first_page,
                 n_stream_pages):
    n, w = u.shape
    steps = n // tile
    pages_per_step = n_stream_pages // steps
    assert pages_per_step * steps == n_stream_pages and pages_per_step > 0
    halo_per_tile = tile // CONV_HALO
    full = lambda shape: pl.BlockSpec(shape, lambda i, pt: (0, 0))
    rowspec = pl.BlockSpec((tile, w), lambda i, pt: (i, 0))
    pages, pmean_spec, blocks_per_step = _page_stream_specs(cache, lambda i: i, first_page, pages_per_step)
    return pl.pallas_call(
        functools.partial(_conv_prompt_kernel, tiles_per_seq=seq // tile, rows=64, pages_per_step=pages_per_step),
        grid_spec=pltpu.PrefetchScalarGridSpec(
            num_scalar_prefetch=1,
            grid=(steps,),
            in_specs=[
                rowspec,
                pl.BlockSpec((CONV_HALO, w), lambda i, pt: (jnp.maximum(i * halo_per_tile - 1, 0), 0)),
                rowspec,
                full((CONV_K, w)), full((1, w)), full((1, w)), full((1, w)), full((w, w)), full((1, w)),
            ] + pages,
            out_specs=[rowspec, pmean_spec],
            scratch_shapes=[pltpu.VMEM((tile + CONV_HALO, w), F32),
                            pltpu.VMEM((SUBLANES - 1, tile + CONV_HALO - SUBLANES, w), F32),
                            pltpu.VMEM((tile, w), F32)],
        ),
        out_shape=[jax.ShapeDtypeStruct((n, w), BF16),
                   jax.ShapeDtypeStruct((steps * blocks_per_step,) + cache.shape[2:], F32)],
        compiler_params=pltpu.CompilerParams(dimension_semantics=("arbitrary",), vmem_limit_bytes=VMEM_LIMIT),
        name="conv_prompt",
    )(page_table_flat, u, u, sgc, w_dw, b_dw, g_cn, b_cn, w_pw_b, b_pw, *([cache] * pages_per_step))


def _conv_sample_kernel(state_ref, u_ref, sgc_ref, wdw_ref, bdw_ref, gcn_ref, bcn_ref, wpw_ref, bpw_ref,
                        o_ref, new_ref, c_ref):
    nb = state_ref.shape[0]
    hist = CONV_K - 1
    for b in range(nb):
        st = state_ref[b]
        u_row = u_ref[b:b + 1, :]
        c_ref[b:b + 1, :] = (jnp.sum(st * wdw_ref[0:hist, :], axis=0, keepdims=True)
                             + u_row * wdw_ref[hist:hist + 1, :] + bdw_ref[...])
        new_ref[b, 0:hist - 1, :] = state_ref[b, 1:hist, :]
        new_ref[b, hist - 1:hist, :] = u_row
    o_ref[...] = _conv_tail(c_ref[...], sgc_ref[...], gcn_ref, bcn_ref, wpw_ref, bpw_ref).astype(o_ref.dtype)


def _conv_sample(state, u, sgc, w_dw, b_dw, g_cn, b_cn, w_pw_b, b_pw):
    nb, hist, w = state.shape
    full2 = lambda shape: pl.BlockSpec(shape, lambda i: (0, 0))
    full3 = lambda shape: pl.BlockSpec(shape, lambda i: (0, 0, 0))
    return pl.pallas_call(
        _conv_sample_kernel,
        grid=(1,),
        in_specs=[full3((nb, hist, w)), full2((nb, w)), full2((nb, w)),
                  full2((CONV_K, w)), full2((1, w)), full2((1, w)), full2((1, w)), full2((w, w)), full2((1, w))],
        out_specs=[full2((nb, w)), full3((nb, hist, w))],
        out_shape=[jax.ShapeDtypeStruct((nb, w), BF16), jax.ShapeDtypeStruct((nb, hist, w), F32)],
        scratch_shapes=[pltpu.VMEM((nb, w), F32)],
        compiler_params=pltpu.CompilerParams(dimension_semantics=("arbitrary",), vmem_limit_bytes=VMEM_LIMIT),
        name="conv_sample",
    )(state, u, sgc, w_dw, b_dw, g_cn, b_cn, w_pw_b, b_pw)


def _out_kernel(pt_ref, x_ref, a_ref, c_ref, p_ref, woa_ref, woc_ref, bo_ref, g_ref, b_ref, wpg_ref, bpg_ref, wpe_ref,
                *refs, alpha, pages_per_step):
    del pt_ref
    page_refs, o_ref = refs[:pages_per_step], refs[pages_per_step]
    for a in range(pages_per_step // PAGES_PER_BLOCK):
        _cache_block_mean(page_refs, refs[pages_per_step + 1], a)
    attn = a_ref[...].reshape(c_ref.shape).astype(BF16)
    mix = (jnp.dot(attn, woa_ref[...], preferred_element_type=F32)
           + jnp.dot(c_ref[...].astype(BF16), woc_ref[...], preferred_element_type=F32) + bo_ref[...])
    t = alpha * x_ref[...] + mix
    mu = jnp.mean(t, axis=-1, keepdims=True)
    d = t - mu
    var = jnp.mean(d * d, axis=-1, keepdims=True)
    h = d * lax.rsqrt(var + LN_EPS) * g_ref[...] + b_ref[...]
    gate = _sigmoid(jnp.dot(h.astype(BF16), wpg_ref[...], preferred_element_type=F32) + bpg_ref[...])
    pe = jnp.dot(p_ref[...].astype(BF16), wpe_ref[...], preferred_element_type=F32)
    o_ref[...] = h + gate * pe


def _out_proj(x, attn, attn_spec, conv, p, w_out_b, b_out, g_ln, b_ln, w_pg_b, b_pg, w_pe_b, alpha, tm,
              page_table_flat, cache, first_page, n_stream_pages):
    n, d = x.shape
    half = conv.shape[1]
    pd = p.shape[1]
    steps = n // tm
    pages_per_step = n_stream_pages // steps
    assert pages_per_step * steps == n_stream_pages
    const = lambda shape, r=0: pl.BlockSpec(shape, lambda i, pt: (r, 0), pipeline_mode=pl.Buffered(1))
    row = lambda w: pl.BlockSpec((tm, w), lambda i, pt: (i, 0))
    out_specs, out_shape, pages = [row(d)], [jax.ShapeDtypeStruct((n, d), F32)], []
    if pages_per_step:
        pages, pmean_spec, blocks_per_step = _page_stream_specs(cache, lambda i: i, first_page, pages_per_step)
        out_specs.append(pmean_spec)
        out_shape.append(jax.ShapeDtypeStruct((steps * blocks_per_step,) + cache.shape[2:], F32))
    return pl.pallas_call(
        functools.partial(_out_kernel, alpha=alpha, pages_per_step=pages_per_step),
        grid_spec=pltpu.PrefetchScalarGridSpec(
            num_scalar_prefetch=1,
            grid=(steps,),
            in_specs=[row(d), attn_spec, row(half), row(pd),
                      const((half, d), 0), const((half, d), 1), const((1, d)), const((1, d)), const((1, d)),
                      const((d, d)), const((1, d)), const((pd, d))] + pages,
            out_specs=out_specs,
        ),
        out_shape=out_shape,
        compiler_params=pltpu.CompilerParams(dimension_semantics=("arbitrary",), vmem_limit_bytes=VMEM_LIMIT),
        name="out_proj",
    )(page_table_flat, x, attn, conv, p, w_out_b, w_out_b, b_out, g_ln, b_ln, w_pg_b, b_pg, w_pe_b,
      *([cache] * pages_per_step))


def _select_kernel(q_ref, kmean_ref, o_ref):
    g = jnp.sum(kmean_ref[...] * q_ref[...][:, None], axis=-1, keepdims=True)
    n_blocks = g.shape[1]
    idx = lax.broadcasted_iota(jnp.int32, g.shape, 1)
    for t in range(MOBA_TOPK):
        m = jnp.max(g, axis=1, keepdims=True)
        first = jnp.min(jnp.where(g == m, idx, n_blocks), axis=1, keepdims=True)
        o_ref[:, t] = jnp.broadcast_to(first[:, 0], o_ref.shape[:1] + o_ref.shape[2:])
        g = jnp.where(idx == first, -jnp.inf, g)


def _select_sample(q4, kmean):
    n_seq, n_blocks, heads, hd = kmean.shape
    assert n_blocks >= MOBA_TOPK
    return pl.pallas_call(
        _select_kernel,
        grid=(1,),
        in_specs=[pl.BlockSpec((n_seq, heads, hd), lambda b: (0, 0, 0)),
                  pl.BlockSpec((n_seq, n_blocks, heads, hd), lambda b: (0, 0, 0, 0))],
        out_specs=pl.BlockSpec((n_seq, MOBA_TOPK, heads, LANES), lambda b: (0, 0, 0, 0)),
        out_shape=jax.ShapeDtypeStruct((n_seq, MOBA_TOPK, heads, LANES), jnp.int32),
        compiler_params=pltpu.CompilerParams(dimension_semantics=("arbitrary",), vmem_limit_bytes=VMEM_LIMIT),
        name="select_sample",
    )(q4, kmean)


def _moba_sample_kernel(sel_ref, pt_ref, q_ref, kn_ref, vn_ref, sga_ref, ck_ref, cv_ref, o_ref, kbuf, vbuf, sems,
                        *, n_seq, n_pages, heads):
    b = pl.program_id(0)
    n_sel = MOBA_TOPK * PAGES_PER_BLOCK
    scale = HEAD_DIM ** -0.5

    def slab_copies(seq, slot):
        copies = []
        for h in range(heads):
            for t in range(MOBA_TOPK):
                blk = sel_ref[(seq * MOBA_TOPK + t) * heads + h]
                for r in range(PAGES_PER_BLOCK):
                    page = pt_ref[seq * n_pages + blk * PAGES_PER_BLOCK + r]
                    dst = t * PAGES_PER_BLOCK + r
                    copies.append(pltpu.make_async_copy(
                        ck_ref.at[page, :, h, :], kbuf.at[slot, h, dst], sems.at[slot, 0, h]))
                    copies.append(pltpu.make_async_copy(
                        cv_ref.at[page, :, h, :], vbuf.at[slot, h, dst], sems.at[slot, 1, h]))
        return copies

    slot = b % 2

    @pl.when(b == 0)
    def _():
        for c in slab_copies(b, slot):
            c.start()

    @pl.when(b + 1 < n_seq)
    def _():
        for c in slab_copies(b + 1, 1 - slot):
            c.start()

    for c in slab_copies(b, slot):
        c.wait()

    for h in range(heads):
        hs = slice(h, h + 1)
        qh = q_ref[0, hs, :]
        kh = kbuf[slot, h].reshape(n_sel * PAGE_SIZE, HEAD_DIM).astype(BF16)
        vh = vbuf[slot, h].reshape(n_sel * PAGE_SIZE, HEAD_DIM).astype(BF16)
        s = _dot_nt(qh.astype(BF16), kh) * scale
        s_new = jnp.sum(qh * kn_ref[0, hs, :], axis=1, keepdims=True) * scale
        m = jnp.maximum(jnp.max(s, axis=1, keepdims=True), s_new)
        p = jnp.exp(s - m)
        p_new = jnp.exp(s_new - m)
        l = jnp.sum(p, axis=1, keepdims=True) + p_new
        acc = jnp.dot(p.astype(BF16), vh, preferred_element_type=F32) + p_new * vn_ref[0, hs, :]
        o_ref[0, hs, :] = acc / l * sga_ref[0, hs, :]


def _moba_sample(sel_flat, page_table_flat, q4, kn4, vn4, sga4, cache_k, cache_v, n_pages):
    n_seq, heads, hd = q4.shape
    n_sel = MOBA_TOPK * PAGES_PER_BLOCK
    vec = pl.BlockSpec((1, heads, hd), lambda b, sel, pt: (b, 0, 0))
    hbm = pl.BlockSpec(memory_space=pl.ANY)
    return pl.pallas_call(
        functools.partial(_moba_sample_kernel, n_seq=n_seq, n_pages=n_pages, heads=heads),
        grid_spec=pltpu.PrefetchScalarGridSpec(
            num_scalar_prefetch=2,
            grid=(n_seq,),
            in_specs=[vec, vec, vec, vec, hbm, hbm],
            out_specs=vec,
            scratch_shapes=[pltpu.VMEM((2, heads, n_sel, PAGE_SIZE, hd), F32),
                            pltpu.VMEM((2, heads, n_sel, PAGE_SIZE, hd), F32),
                            pltpu.SemaphoreType.DMA((2, 2, heads))],
        ),
        out_shape=jax.ShapeDtypeStruct(q4.shape, F32),
        compiler_params=pltpu.CompilerParams(dimension_semantics=("arbitrary",), vmem_limit_bytes=VMEM_LIMIT),
        name="moba_sample",
    )(sel_flat, page_table_flat, q4, kn4, vn4, sga4, cache_k, cache_v)


def kernel(x_prompt, x_sample, p_prompt, p_sample, cache_k, cache_v, state_conv, page_table, w_in, b_in, w_dw,
           b_dw, g_cn, b_cn, w_pw, b_pw, w_out, b_out, g_ln, b_ln, w_pe, w_pg, b_pg):
    depth = w_in.shape[0]
    assert depth == 1
    batch, seq, d_model = x_prompt.shape
    n_seq, dec_seq, _ = x_sample.shape
    assert dec_seq == 1
    n_phys, page, heads, head_dim = cache_k.shape[1:]
    assert head_dim == HEAD_DIM and page == PAGE_SIZE
    attn_w = heads * head_dim
    conv_w = w_pw.shape[1]
    assert conv_w == attn_w and w_in.shape[2] == 7 * attn_w
    n_pages = page_table.shape[1]
    hist = CONV_K - 1
    alpha = (2 * depth) ** 0.25
    n = batch * seq

    row2 = lambda a: a.reshape(1, -1)
    w_in_b = w_in[0].astype(BF16)
    w_pw_b = w_pw[0].astype(BF16)
    w_out_b = w_out[0].astype(BF16)
    w_pg_b = w_pg[0].astype(BF16)
    w_pe_b = w_pe[0].astype(BF16)
    b_in2, b_dw2, g_cn2, b_cn2, b_pw2 = row2(b_in[0]), row2(b_dw[0]), row2(g_cn[0]), row2(b_cn[0]), row2(b_pw[0])
    b_out2, g_ln2, b_ln2, b_pg2 = row2(b_out[0]), row2(g_ln[0]), row2(b_ln[0]), row2(b_pg[0])
    conv_w_args = (w_dw[0], b_dw2, g_cn2, b_cn2, w_pw_b, b_pw2)
    out_w_args = (w_out_b, b_out2, g_ln2, b_ln2, w_pg_b, b_pg2, w_pe_b, alpha)

    xp = x_prompt.reshape(n, d_model)
    xs = x_sample.reshape(n_seq, d_model)
    (q, k, v, sga, u, sgc), (qs, ks, vs, sgas, us, sgcs) = _in_proj_all(xp, xs, w_in_b, b_in2, tm=1024)
    pt_flat = page_table.reshape(-1)
    ck = cache_k.reshape(n_phys, page, heads, head_dim)
    cv = cache_v.reshape(n_phys, page, heads, head_dim)
    total_pages = n_seq * n_pages
    out_pages = (n // MOBA_BLOCK) * OUT_PROJ_PAGES_PER_STEP
    conv_pages = (n // MOBA_BLOCK) * CONV_PAGES_PER_STEP
    attn_pages = total_pages - out_pages - conv_pages
    attn, kmean_a = _moba_prompt(q, k, v, sga, batch, seq, heads, pt_flat, ck, attn_pages)
    conv, kmean_c = _conv_prompt(u, sgc, *conv_w_args, seq=seq, tile=MOBA_BLOCK, page_table_flat=pt_flat, cache=ck,
                                 first_page=attn_pages, n_stream_pages=conv_pages)
    n_blocks = seq // MOBA_BLOCK

    def paired_rows(r, pt):
        blk = r % n_blocks
        mirror = n_blocks - 1 - blk
        return (r // n_blocks, jnp.minimum(blk, mirror), jnp.where(blk > mirror, 1, 0), 0, 0)

    y_prompt, kmean_o = _out_proj(xp, attn, pl.BlockSpec((1, 1, 1, MOBA_BLOCK, attn_w), paired_rows), conv,
                                  p_prompt[0].reshape(n, -1), *out_w_args, tm=MOBA_BLOCK, page_table_flat=pt_flat,
                                  cache=ck, first_page=total_pages - out_pages, n_stream_pages=out_pages)
    kmean = jnp.concatenate([kmean_a, kmean_c, kmean_o]).reshape(n_seq, n_pages // PAGES_PER_BLOCK, heads, head_dim)
    conv_prompt_new = u.reshape(batch, seq, conv_w)[:, seq - hist:, :]

    conv_s, conv_sample_new = _conv_sample(state_conv[0], us, sgcs, *conv_w_args)
    as4 = lambda a: a.reshape(n_seq, heads, head_dim)
    sel = _select_sample(as4(qs), kmean)[:, :, :, 0]
    attn_s = _moba_sample(sel.reshape(-1), pt_flat, as4(qs), as4(ks), as4(vs), as4(sgas), ck, cv, n_pages)
    (y_sample,) = _out_proj(xs, attn_s.reshape(n_seq, attn_w), pl.BlockSpec((n_seq, attn_w), lambda r, pt: (r, 0)),
                            conv_s, p_sample[0].reshape(n_seq, -1), *out_w_args, tm=n_seq, page_table_flat=pt_flat,
                            cache=ck, first_page=0, n_stream_pages=0)

    kv_shape = (depth, batch, seq, heads, head_dim)
    kvs_shape = (depth, n_seq, dec_seq, heads, head_dim)
    return (y_prompt.reshape(batch, seq, d_model), y_sample.reshape(n_seq, dec_seq, d_model),
            k.reshape(kv_shape), v.reshape(kv_shape), conv_prompt_new.reshape(depth, batch, hist, conv_w),
            ks.reshape(kvs_shape), vs.reshape(kvs_shape), conv_sample_new.reshape(depth, n_seq, hist, conv_w))
```

```python
import functools

import jax
import jax.numpy as jnp
from jax import lax
from jax.experimental import pallas as pl
from jax.experimental.pallas import tpu as pltpu

F32 = jnp.float32
BF16 = jnp.bfloat16

HEAD_DIM = 128
CONV_K = 31
MOBA_BLOCK = 256
MOBA_TOPK = 3
PAGE_SIZE = 128
PAGES_PER_BLOCK = MOBA_BLOCK // PAGE_SIZE
LN_EPS = 1e-5
NEG = -1e30
LOG2_E = 1.4426950408889634
LANES = 128
SUBLANES = 8
SUM_ROWS = 16
CONV_HALO = 32
OUT_PROJ_PAGES_PER_STEP = 16
CONV_PAGES_PER_STEP = 8
VMEM_LIMIT = 60 * 1024 * 1024


def _sigmoid(z):
    return 0.5 * jnp.tanh(0.5 * z) + 0.5


def _silu(z):
    return z * _sigmoid(z)


def _dot_nt(a, b, **kw):
    return lax.dot_general(a, b, (((1,), (1,)), ((), ())), preferred_element_type=F32, **kw)


def _in_proj_kernel(x_ref, xs_ref, w_ref, b_ref, *refs, finishers, n_out):
    outs_p, outs_s, (xb_ref, *hold) = refs[:n_out], refs[n_out:2 * n_out], refs[2 * n_out:]
    j = pl.program_id(1)
    tm = x_ref.shape[0]

    @pl.when(j == 0)
    def _():
        xb_ref[0:tm, :] = x_ref[...].astype(BF16)
        xb_ref[tm:, :] = xs_ref[...].astype(BF16)

    def write(k, value):
        outs_p[k][...] = value[0:tm].astype(outs_p[k].dtype)
        outs_s[k][...] = value[tm:].astype(outs_s[k].dtype)

    for g, finish in enumerate(finishers):
        @pl.when(j == g)
        def _(finish=finish):
            finish(jnp.dot(xb_ref[...], w_ref[...], preferred_element_type=F32) + b_ref[...], write, *hold)


def _emit(k, fn=lambda z: z):
    return lambda z, write, *hold: write(k, fn(z))


def _hold(z, write, hold_ref):
    hold_ref[...] = z


def _glu_into(k):
    return lambda z, write, hold_ref: write(k, hold_ref[...] * _sigmoid(z))


QKV_GROUPS = (_emit(0), _emit(1), _emit(2))
GATE_GROUPS = (_emit(0, _silu), _hold, _glu_into(1), _emit(2, _silu))


def _in_proj(x, xs, w_b, b, tm, first_group, finishers, out_dtypes, needs_hold):
    n, d = x.shape
    ns = xs.shape[0]
    gw = w_b.shape[1] // 7
    row = pl.BlockSpec((tm, gw), lambda i, j: (i, 0))
    fixed = pl.BlockSpec((ns, gw), lambda i, j: (0, 0))
    return pl.pallas_call(
        functools.partial(_in_proj_kernel, finishers=finishers, n_out=len(out_dtypes)),
        grid=(n // tm, len(finishers)),
        in_specs=[
            pl.BlockSpec((tm, d), lambda i, j: (i, 0)),
            pl.BlockSpec((ns, d), lambda i, j: (0, 0)),
            pl.BlockSpec((d, gw), lambda i, j: (0, first_group + j)),
            pl.BlockSpec((1, gw), lambda i, j: (0, first_group + j)),
        ],
        out_specs=[row] * len(out_dtypes) + [fixed] * len(out_dtypes),
        out_shape=([jax.ShapeDtypeStruct((n, gw), dt) for dt in out_dtypes]
                   + [jax.ShapeDtypeStruct((ns, gw), dt) for dt in out_dtypes]),
        scratch_shapes=([pltpu.VMEM((tm + ns, d), BF16)]
                        + ([pltpu.VMEM((tm + ns, gw), F32)] if needs_hold else [])),
        compiler_params=pltpu.CompilerParams(
            dimension_semantics=("arbitrary", "arbitrary"), vmem_limit_bytes=VMEM_LIMIT),
        name="in_proj",
    )(x, xs, w_b, b)


def _in_proj_all(x, xs, w_b, b, tm):
    q, k, v, qs, ks, vs = _in_proj(x, xs, w_b, b, tm, 0, QKV_GROUPS, (F32, F32, F32), False)
    sga, u, sgc, sgas, us, sgcs = _in_proj(x, xs, w_b, b, tm, len(QKV_GROUPS), GATE_GROUPS, (BF16, F32, BF16), True)
    return (q, k, v, sga, u, sgc), (qs, ks, vs, sgas, us, sgcs)


def _topk_mask(g, idx, n_valid, axis):
    sel = jnp.zeros(g.shape, F32)
    big = g.shape[axis]
    for t in range(MOBA_TOPK):
        m = jnp.max(g, axis=axis, keepdims=True)
        first = jnp.min(jnp.where(g == m, idx, big), axis=axis, keepdims=True)
        hit = idx == first
        counts = jnp.where(t < n_valid, 1.0, 0.0)
        sel = jnp.maximum(sel, jnp.where(hit, counts, 0.0))
        g = jnp.where(hit, -jnp.inf, g)
    return sel


def _cache_block_mean(page_refs, pmean_ref, a):
    tot = jnp.sum(page_refs[PAGES_PER_BLOCK * a][0], axis=0)
    for r in range(1, PAGES_PER_BLOCK):
        tot = tot + jnp.sum(page_refs[PAGES_PER_BLOCK * a + r][0], axis=0)
    pmean_ref[a] = tot * (1.0 / MOBA_BLOCK)


def _page_stream_specs(cache, step_of, first_page, pages_per_step):
    n_phys, page, heads, hd = cache.shape
    assert pages_per_step % PAGES_PER_BLOCK == 0
    blocks_per_step = pages_per_step // PAGES_PER_BLOCK

    def page_spec(r):
        return pl.BlockSpec((1, page, heads, hd),
                            lambda *ids: (ids[-1][first_page + step_of(*ids[:-1]) * pages_per_step + r], 0, 0, 0))

    in_specs = [page_spec(r) for r in range(pages_per_step)]
    out_spec = pl.BlockSpec((blocks_per_step, heads, hd), lambda *ids: (step_of(*ids[:-1]), 0, 0))
    return in_specs, out_spec, blocks_per_step


def _moba_prompt_kernel(pt_ref, qa_ref, qb_ref, qna_ref, qnb_ref, k_ref, v_ref, sgaa_ref, sgab_ref, *refs,
                        n_blocks, pages_per_step):
    del pt_ref
    page_refs = refs[:pages_per_step]
    o_ref, pmean_ref, kb_ref, vt_ref, kmean_ref, acc_ref = refs[pages_per_step:pages_per_step + 6]
    bufs = refs[pages_per_step + 6:]
    stage = (bufs[0:4], bufs[4:8])
    i = pl.program_id(2)
    blk = MOBA_BLOCK
    last = n_blocks - 1
    pairs = n_blocks // 2
    qscale = HEAD_DIM ** -0.5 * LOG2_E
    sga_refs = (sgaa_ref, sgab_ref)

    def past_slot(t, pair):
        first = t < pair
        return jnp.where(first, 1.0, 0.0), jnp.where(first, 0, 1), jnp.where(first, t, t - pair)

    def score_pass(pair, q_refs, buf, with_cache_stream):
        qt_ref, bias_ref, m_ref, s_ref = buf
        own = (pair, last - pair)
        for w in range(2):
            qf_t = q_refs[w][...].T
            qt_ref[w] = (qf_t * qscale).astype(BF16)
            gate = jnp.dot(kmean_ref[...], qf_t, preferred_element_type=F32, precision=lax.Precision.HIGHEST)
            row = lax.broadcasted_iota(jnp.int32, gate.shape, 0)
            gate = jnp.where(row < own[w], gate, -jnp.inf)
            sel = _topk_mask(gate, row, own[w], axis=0)
            bias_ref[w] = jnp.where(sel > 0.5, 0.0, NEG)
        k_id = lax.broadcasted_iota(jnp.int32, (blk, blk), 0)
        q_id = lax.broadcasted_iota(jnp.int32, (blk, blk), 1)
        m = []
        for w in range(2):
            s = jnp.dot(kb_ref[own[w]], qt_ref[w], preferred_element_type=F32)
            s = jnp.where(k_id <= q_id, s, NEG)
            s_ref[w] = s
            m.append(jnp.max(s, axis=0, keepdims=True))
        for t in range(last):
            fa, w, kblk = past_slot(t, pair)
            s = jnp.dot(kb_ref[kblk], qt_ref[w], preferred_element_type=F32)
            s_ref[2 + t] = s
            c = jnp.max(s, axis=0, keepdims=True) + bias_ref[w, pl.ds(kblk, 1), :]
            m[0] = jnp.maximum(m[0], c + (fa - 1.0) * -NEG)
            m[1] = jnp.maximum(m[1], c + fa * NEG)
            if with_cache_stream and t < pages_per_step // PAGES_PER_BLOCK:
                _cache_block_mean(page_refs, pmean_ref, t)
        for w in range(2):
            m_ref[w] = m[w]

    def value_pass(pair, buf):
        _, bias_ref, m_ref, s_ref = buf
        own = (pair, last - pair)
        m = [m_ref[0], m_ref[1]]
        for w in range(2):
            p = jnp.exp2((s_ref[w] - m[w]).astype(BF16))
            acc_ref[w] = jnp.dot(vt_ref[own[w]], p, preferred_element_type=F32)
        for t in range(last):
            fa, w, kblk = past_slot(t, pair)
            shift = bias_ref[w, pl.ds(kblk, 1), :] - (m[1] + fa * (m[0] - m[1]))
            p = jnp.exp2((s_ref[2 + t] + shift).astype(BF16))
            acc_ref[2 + t] = jnp.dot(vt_ref[kblk], p, preferred_element_type=F32)
        bounds = ((0, pair), (pair, last))
        for w in range(2):
            acc = lax.fori_loop(bounds[w][0], bounds[w][1], lambda t, a: a + acc_ref[2 + t], acc_ref[w])
            out_t = acc[0:HEAD_DIM, :] / acc[HEAD_DIM:HEAD_DIM + 1, :]
            o_ref[0, 0, w] = (out_t.T * sga_refs[w][...]).astype(o_ref.dtype)

    @pl.when(i == 0)
    def _():
        kmean_ref[...] = jnp.mean(k_ref[...].reshape(n_blocks, blk, HEAD_DIM), axis=1)
        for jb in range(n_blocks):
            rows = slice(jb * blk, (jb + 1) * blk)
            kb_ref[jb] = k_ref[rows, :].astype(BF16)
            vt_ref[jb, 0:HEAD_DIM, :] = v_ref[rows, :].T.astype(BF16)
            vt_ref[jb, HEAD_DIM:, :] = jnp.ones((SUM_ROWS, blk), BF16)
        score_pass(0, (qa_ref, qb_ref), stage[0], False)

    nxt = jnp.minimum(i + 1, pairs - 1)
    for parity in range(2):
        @pl.when(i % 2 == parity)
        def _():
            score_pass(nxt, (qna_ref, qnb_ref), stage[1 - parity], True)
            value_pass(i, stage[parity])


def _moba_prompt(q, k, v, sga, batch, seq, heads, page_table_flat, cache, n_stream_pages):
    n_blocks = seq // MOBA_BLOCK
    assert n_blocks % 2 == 0
    last = n_blocks - 1
    pairs = n_blocks // 2
    steps = batch * heads * pairs
    pages_per_step = n_stream_pages // steps
    assert pages_per_step * steps == n_stream_pages

    def step_id(b, h, i):
        return (b * heads + h) * pairs + i

    page_specs, pmean_spec, blocks_per_step = _page_stream_specs(cache, step_id, 0, pages_per_step)
    assert blocks_per_step <= last
    qa = pl.BlockSpec((MOBA_BLOCK, HEAD_DIM), lambda b, h, i, pt: (b * n_blocks + i, h))
    qb = pl.BlockSpec((MOBA_BLOCK, HEAD_DIM), lambda b, h, i, pt: (b * n_blocks + last - i, h))
    qna = pl.BlockSpec((MOBA_BLOCK, HEAD_DIM), lambda b, h, i, pt: (b * n_blocks + jnp.minimum(i + 1, pairs - 1), h))
    qnb = pl.BlockSpec((MOBA_BLOCK, HEAD_DIM),
                       lambda b, h, i, pt: (b * n_blocks + last - jnp.minimum(i + 1, pairs - 1), h))
    score_stage = [pltpu.VMEM((2, HEAD_DIM, MOBA_BLOCK), BF16),
                   pltpu.VMEM((2, n_blocks, MOBA_BLOCK), F32),
                   pltpu.VMEM((2, 1, MOBA_BLOCK), F32),
                   pltpu.VMEM((n_blocks + 1, MOBA_BLOCK, MOBA_BLOCK), F32)]
    kvspec = pl.BlockSpec((seq, HEAD_DIM), lambda b, h, i, pt: (b, h))
    return pl.pallas_call(
        functools.partial(_moba_prompt_kernel, n_blocks=n_blocks, pages_per_step=pages_per_step),
        grid_spec=pltpu.PrefetchScalarGridSpec(
            num_scalar_prefetch=1,
            grid=(batch, heads, pairs),
            in_specs=[qa, qb, qna, qnb, kvspec, kvspec, qa, qb] + page_specs,
            out_specs=[pl.BlockSpec((1, 1, 2, MOBA_BLOCK, HEAD_DIM), lambda b, h, i, pt: (b, i, 0, 0, h)), pmean_spec],
            scratch_shapes=[pltpu.VMEM((n_blocks, MOBA_BLOCK, HEAD_DIM), BF16),
                            pltpu.VMEM((n_blocks, HEAD_DIM + SUM_ROWS, MOBA_BLOCK), BF16),
                            pltpu.VMEM((n_blocks, HEAD_DIM), F32),
                            pltpu.VMEM((n_blocks + 1, HEAD_DIM + SUM_ROWS, MOBA_BLOCK), F32)] + score_stage * 2,
        ),
        out_shape=[jax.ShapeDtypeStruct((batch, pairs, 2, MOBA_BLOCK, q.shape[1]), BF16),
                   jax.ShapeDtypeStruct((steps * blocks_per_step,) + cache.shape[2:], F32)],
        compiler_params=pltpu.CompilerParams(
            dimension_semantics=("arbitrary", "arbitrary", "arbitrary"), vmem_limit_bytes=VMEM_LIMIT),
        name="moba_prompt",
    )(page_table_flat, q, q, q, q, k, v, sga, sga, *([cache] * pages_per_step))


def _conv_tail(c, sgc, gcn_ref, bcn_ref, wpw_ref, bpw_ref):
    mu = jnp.mean(c, axis=-1, keepdims=True)
    d = c - mu
    var = jnp.mean(d * d, axis=-1, keepdims=True)
    y = d * lax.rsqrt(var + LN_EPS) * gcn_ref[...] + bcn_ref[...]
    y = _silu(y)
    return (jnp.dot(y.astype(BF16), wpw_ref[...], preferred_element_type=F32) + bpw_ref[...]) * sgc


def _conv_prompt_kernel(pt_ref, u_ref, prev_ref, sgc_ref, wdw_ref, bdw_ref, gcn_ref, bcn_ref, wpw_ref, bpw_ref,
                        *refs, tiles_per_seq, rows, pages_per_step):
    del pt_ref
    page_refs = refs[:pages_per_step]
    o_ref, pmean_ref, ext_ref, sh_ref, c_ref = refs[pages_per_step:]
    for a in range(pages_per_step // PAGES_PER_BLOCK):
        _cache_block_mean(page_refs, pmean_ref, a)
    i = pl.program_id(0)
    t, w = u_ref.shape
    first = (i % tiles_per_seq) == 0
    ext_ref[0:CONV_HALO, :] = jnp.where(first, 0.0, prev_ref[...])
    ext_ref[CONV_HALO:, :] = u_ref[...]
    n_sh = sh_ref.shape[1]
    for r in range(1, SUBLANES):
        sh_ref[r - 1] = ext_ref[r:r + n_sh, :]
    off = CONV_HALO - (CONV_K - 1)
    for c in range(w // LANES):
        cs = slice(c * LANES, (c + 1) * LANES)
        for rc in range(t // rows):
            acc = jnp.broadcast_to(bdw_ref[:, cs], (rows, LANES))
            for j in range(CONV_K):
                a, r = divmod(off + j, SUBLANES)
                lo = rc * rows + a * SUBLANES
                win = ext_ref[lo:lo + rows, cs] if r == 0 else sh_ref[r - 1, lo:lo + rows, cs]
                acc = acc + win * wdw_ref[j:j + 1, cs]
            c_ref[rc * rows:(rc + 1) * rows, cs] = acc
    o_ref[...] = _conv_tail(c_ref[...], sgc_ref[...], gcn_ref, bcn_ref, wpw_ref, bpw_ref).astype(o_ref.dtype)


def _conv_prompt(u, sgc, w_dw, b_dw, g_cn, b_cn, w_pw_b, b_pw, seq, tile, page_table_flat, cache, first_page,
                 n_stream_pages):
    n, w = u.shape
    steps = n // tile
    pages_per_step = n_stream_pages // steps
    assert pages_per_step * steps == n_stream_pages and pages_per_step > 0
    halo_per_tile = tile // CONV_HALO
    full = lambda shape: pl.BlockSpec(shape, lambda i, pt: (0, 0))
    rowspec = pl.BlockSpec((tile, w), lambda i, pt: (i, 0))
    pages, pmean_spec, blocks_per_step = _page_stream_specs(cache, lambda i: i, first_page, pages_per_step)
    return pl.pallas_call(
        functools.partial(_conv_prompt_kernel, tiles_per_seq=seq // tile, rows=64, pages_per_step=pages_per_step),
        grid_spec=pltpu.PrefetchScalarGridSpec(
            num_scalar_prefetch=1,
            grid=(steps,),
            in_specs=[
                rowspec,
                pl.BlockSpec((CONV_HALO, w), lambda i, pt: (jnp.maximum(i * halo_per_tile - 1, 0), 0)),
                rowspec,
                full((CONV_K, w)), full((1, w)), full((1, w)), full((1, w)), full((w, w)), full((1, w)),
            ] + pages,
            out_specs=[rowspec, pmean_spec],
            scratch_shapes=[pltpu.VMEM((tile + CONV_HALO, w), F32),
                            pltpu.VMEM((SUBLANES - 1, tile + CONV_HALO - SUBLANES, w), F32),
                            pltpu.VMEM((tile, w), F32)],
        ),
        out_shape=[jax.ShapeDtypeStruct((n, w), BF16),
                   jax.ShapeDtypeStruct((steps * blocks_per_step,) + cache.shape[2:], F32)],
        compiler_params=pltpu.CompilerParams(dimension_semantics=("arbitrary",), vmem_limit_bytes=VMEM_LIMIT),
        name="conv_prompt",
    )(page_table_flat, u, u, sgc, w_dw, b_dw, g_cn, b_cn, w_pw_b, b_pw, *([cache] * pages_per_step))


def _conv_sample_kernel(state_ref, u_ref, sgc_ref, wdw_ref, bdw_ref, gcn_ref, bcn_ref, wpw_ref, bpw_ref,
                        o_ref, new_ref, c_ref):
    nb = state_ref.shape[0]
    hist = CONV_K - 1
    for b in range(nb):
        st = state_ref[b]
        u_row = u_ref[b:b + 1, :]
        c_ref[b:b + 1, :] = (jnp.sum(st * wdw_ref[0:hist, :], axis=0, keepdims=True)
                             + u_row * wdw_ref[hist:hist + 1, :] + bdw_ref[...])
        new_ref[b, 0:hist - 1, :] = state_ref[b, 1:hist, :]
        new_ref[b, hist - 1:hist, :] = u_row
    o_ref[...] = _conv_tail(c_ref[...], sgc_ref[...], gcn_ref, bcn_ref, wpw_ref, bpw_ref).astype(o_ref.dtype)


def _conv_sample(state, u, sgc, w_dw, b_dw, g_cn, b_cn, w_pw_b, b_pw):
    nb, hist, w = state.shape
    full2 = lambda shape: pl.BlockSpec(shape, lambda i: (0, 0))
    full3 = lambda shape: pl.BlockSpec(shape, lambda i: (0, 0, 0))
    return pl.pallas_call(
        _conv_sample_kernel,
        grid=(1,),
        in_specs=[full3((nb, hist, w)), full2((nb, w)), full2((nb, w)),
                  full2((CONV_K, w)), full2((1, w)), full2((1, w)), full2((1, w)), full2((w, w)), full2((1, w))],
        out_specs=[full2((nb, w)), full3((nb, hist, w))],
        out_shape=[jax.ShapeDtypeStruct((nb, w), BF16), jax.ShapeDtypeStruct((nb, hist, w), F32)],
        scratch_shapes=[pltpu.VMEM((nb, w), F32)],
        compiler_params=pltpu.CompilerParams(dimension_semantics=("arbitrary",), vmem_limit_bytes=VMEM_LIMIT),
        name="conv_sample",
    )(state, u, sgc, w_dw, b_dw, g_cn, b_cn, w_pw_b, b_pw)


def _out_kernel(pt_ref, x_ref, a_ref, c_ref, p_ref, woa_ref, woc_ref, bo_ref, g_ref, b_ref, wpg_ref, bpg_ref, wpe_ref,
                *refs, alpha, pages_per_step):
    del pt_ref
    page_refs, o_ref = refs[:pages_per_step], refs[pages_per_step]
    for a in range(pages_per_step // PAGES_PER_BLOCK):
        _cache_block_mean(page_refs, refs[pages_per_step + 1], a)
    attn = a_ref[...].reshape(c_ref.shape).astype(BF16)
    mix = (jnp.dot(attn, woa_ref[...], preferred_element_type=F32)
           + jnp.dot(c_ref[...].astype(BF16), woc_ref[...], preferred_element_type=F32) + bo_ref[...])
    t = alpha * x_ref[...] + mix
    mu = jnp.mean(t, axis=-1, keepdims=True)
    d = t - mu
    var = jnp.mean(d * d, axis=-1, keepdims=True)
    h = d * lax.rsqrt(var + LN_EPS) * g_ref[...] + b_ref[...]
    gate = _sigmoid(jnp.dot(h.astype(BF16), wpg_ref[...], preferred_element_type=F32) + bpg_ref[...])
    pe = jnp.dot(p_ref[...].astype(BF16), wpe_ref[...], preferred_element_type=F32)
    o_ref[...] = h + gate * pe


def _out_proj(x, attn, attn_spec, conv, p, w_out_b, b_out, g_ln, b_ln, w_pg_b, b_pg, w_pe_b, alpha, tm,
              page_table_flat, cache, first_page, n_stream_pages):
    n, d = x.shape
    half = conv.shape[1]
    pd = p.shape[1]
    steps = n // tm
    pages_per_step = n_stream_pages // steps
    assert pages_per_step * steps == n_stream_pages
    const = lambda shape, r=0: pl.BlockSpec(shape, lambda i, pt: (r, 0), pipeline_mode=pl.Buffered(1))
    row = lambda w: pl.BlockSpec((tm, w), lambda i, pt: (i, 0))
    out_specs, out_shape, pages = [row(d)], [jax.ShapeDtypeStruct((n, d), F32)], []
    if pages_per_step:
        pages, pmean_spec, blocks_per_step = _page_stream_specs(cache, lambda i: i, first_page, pages_per_step)
        out_specs.append(pmean_spec)
        out_shape.append(jax.ShapeDtypeStruct((steps * blocks_per_step,) + cache.shape[2:], F32))
    return pl.pallas_call(
        functools.partial(_out_kernel, alpha=alpha, pages_per_step=pages_per_step),
        grid_spec=pltpu.PrefetchScalarGridSpec(
            num_scalar_prefetch=1,
            grid=(steps,),
            in_specs=[row(d), attn_spec, row(half), row(pd),
                      const((half, d), 0), const((half, d), 1), const((1, d)), const((1, d)), const((1, d)),
                      const((d, d)), const((1, d)), const((pd, d))] + pages,
            out_specs=out_specs,
        ),
        out_shape=out_shape,
        compiler_params=pltpu.CompilerParams(dimension_semantics=("arbitrary",), vmem_limit_bytes=VMEM_LIMIT),
        name="out_proj",
    )(page_table_flat, x, attn, conv, p, w_out_b, w_out_b, b_out, g_ln, b_ln, w_pg_b, b_pg, w_pe_b,
      *([cache] * pages_per_step))


def _select_kernel(q_ref, kmean_ref, o_ref):
    g = jnp.sum(kmean_ref[...] * q_ref[...][:, None], axis=-1, keepdims=True)
    n_blocks = g.shape[1]
    idx = lax.broadcasted_iota(jnp.int32, g.shape, 1)
    for t in range(MOBA_TOPK):
        m = jnp.max(g, axis=1, keepdims=True)
        first = jnp.min(jnp.where(g == m, idx, n_blocks), axis=1, keepdims=True)
        o_ref[:, t] = jnp.broadcast_to(first[:, 0], o_ref.shape[:1] + o_ref.shape[2:])
        g = jnp.where(idx == first, -jnp.inf, g)


def _select_sample(q4, kmean):
    n_seq, n_blocks, heads, hd = kmean.shape
    assert n_blocks >= MOBA_TOPK
    return pl.pallas_call(
        _select_kernel,
        grid=(1,),
        in_specs=[pl.BlockSpec((n_seq, heads, hd), lambda b: (0, 0, 0)),
                  pl.BlockSpec((n_seq, n_blocks, heads, hd), lambda b: (0, 0, 0, 0))],
        out_specs=pl.BlockSpec((n_seq, MOBA_TOPK, heads, LANES), lambda b: (0, 0, 0, 0)),
        out_shape=jax.ShapeDtypeStruct((n_seq, MOBA_TOPK, heads, LANES), jnp.int32),
        compiler_params=pltpu.CompilerParams(dimension_semantics=("arbitrary",), vmem_limit_bytes=VMEM_LIMIT),
        name="select_sample",
    )(q4, kmean)


def _moba_sample_kernel(sel_ref, pt_ref, q_ref, kn_ref, vn_ref, sga_ref, ck_ref, cv_ref, o_ref, kbuf, vbuf, sems,
                        *, n_seq, n_pages, heads):
    b = pl.program_id(0)
    n_sel = MOBA_TOPK * PAGES_PER_BLOCK
    scale = HEAD_DIM ** -0.5

    def slab_copies(seq, slot):
        copies = []
        for h in range(heads):
            for t in range(MOBA_TOPK):
                blk = sel_ref[(seq * MOBA_TOPK + t) * heads + h]
                for r in range(PAGES_PER_BLOCK):
                    page = pt_ref[seq * n_pages + blk * PAGES_PER_BLOCK + r]
                    dst = t * PAGES_PER_BLOCK + r
                    copies.append(pltpu.make_async_copy(
                        ck_ref.at[page, :, h, :], kbuf.at[slot, h, dst], sems.at[slot, 0, h]))
                    copies.append(pltpu.make_async_copy(
                        cv_ref.at[page, :, h, :], vbuf.at[slot, h, dst], sems.at[slot, 1, h]))
        return copies

    slot = b % 2

    @pl.when(b == 0)
    def _():
        for c in slab_copies(b, slot):
            c.start()

    @pl.when(b + 1 < n_seq)
    def _():
        for c in slab_copies(b + 1, 1 - slot):
            c.start()

    for c in slab_copies(b, slot):
        c.wait()

    for h in range(heads):
        hs = slice(h, h + 1)
        qh = q_ref[0, hs, :]
        kh = kbuf[slot, h].reshape(n_sel * PAGE_SIZE, HEAD_DIM).astype(BF16)
        vh = vbuf[slot, h].reshape(n_sel * PAGE_SIZE, HEAD_DIM).astype(BF16)
        s = _dot_nt(qh.astype(BF16), kh) * scale
        s_new = jnp.sum(qh * kn_ref[0, hs, :], axis=1, keepdims=True) * scale
        m = jnp.maximum(jnp.max(s, axis=1, keepdims=True), s_new)
        p = jnp.exp(s - m)
        p_new = jnp.exp(s_new - m)
        l = jnp.sum(p, axis=1, keepdims=True) + p_new
        acc = jnp.dot(p.astype(BF16), vh, preferred_element_type=F32) + p_new * vn_ref[0, hs, :]
        o_ref[0, hs, :] = acc / l * sga_ref[0, hs, :]


def _moba_sample(sel_flat, page_table_flat, q4, kn4, vn4, sga4, cache_k, cache_v, n_pages):
    n_seq, heads, hd = q4.shape
    n_sel = MOBA_TOPK * PAGES_PER_BLOCK
    vec = pl.BlockSpec((1, heads, hd), lambda b, sel, pt: (b, 0, 0))
    hbm = pl.BlockSpec(memory_space=pl.ANY)
    return pl.pallas_call(
        functools.partial(_moba_sample_kernel, n_seq=n_seq, n_pages=n_pages, heads=heads),
        grid_spec=pltpu.PrefetchScalarGridSpec(
            num_scalar_prefetch=2,
            grid=(n_seq,),
            in_specs=[vec, vec, vec, vec, hbm, hbm],
            out_specs=vec,
            scratch_shapes=[pltpu.VMEM((2, heads, n_sel, PAGE_SIZE, hd), F32),
                            pltpu.VMEM((2, heads, n_sel, PAGE_SIZE, hd), F32),
                            pltpu.SemaphoreType.DMA((2, 2, heads))],
        ),
        out_shape=jax.ShapeDtypeStruct(q4.shape, F32),
        compiler_params=pltpu.CompilerParams(dimension_semantics=("arbitrary",), vmem_limit_bytes=VMEM_LIMIT),
        name="moba_sample",
    )(sel_flat, page_table_flat, q4, kn4, vn4, sga4, cache_k, cache_v)


def kernel(x_prompt, x_sample, p_prompt, p_sample, cache_k, cache_v, state_conv, page_table, w_in, b_in, w_dw,
           b_dw, g_cn, b_cn, w_pw, b_pw, w_out, b_out, g_ln, b_ln, w_pe, w_pg, b_pg):
    depth = w_in.shape[0]
    assert depth == 1
    batch, seq, d_model = x_prompt.shape
    n_seq, dec_seq, _ = x_sample.shape
    assert dec_seq == 1
    n_phys, page, heads, head_dim = cache_k.shape[1:]
    assert head_dim == HEAD_DIM and page == PAGE_SIZE
    attn_w = heads * head_dim
    conv_w = w_pw.shape[1]
    assert conv_w == attn_w and w_in.shape[2] == 7 * attn_w
    n_pages = page_table.shape[1]
    hist = CONV_K - 1
    alpha = (2 * depth) ** 0.25
    n = batch * seq

    row2 = lambda a: a.reshape(1, -1)
    w_in_b = w_in[0].astype(BF16)
    w_pw_b = w_pw[0].astype(BF16)
    w_out_b = w_out[0].astype(BF16)
    w_pg_b = w_pg[0].astype(BF16)
    w_pe_b = w_pe[0].astype(BF16)
    b_in2, b_dw2, g_cn2, b_cn2, b_pw2 = row2(b_in[0]), row2(b_dw[0]), row2(g_cn[0]), row2(b_cn[0]), row2(b_pw[0])
    b_out2, g_ln2, b_ln2, b_pg2 = row2(b_out[0]), row2(g_ln[0]), row2(b_ln[0]), row2(b_pg[0])
    conv_w_args = (w_dw[0], b_dw2, g_cn2, b_cn2, w_pw_b, b_pw2)
    out_w_args = (w_out_b, b_out2, g_ln2, b_ln2, w_pg_b, b_pg2, w_pe_b, alpha)

    xp = x_prompt.reshape(n, d_model)
    xs = x_sample.reshape(n_seq, d_model)
    (q, k, v, sga, u, sgc), (qs, ks, vs, sgas, us, sgcs) = _in_proj_all(xp, xs, w_in_b, b_in2, tm=1024)
    pt_flat = page_table.reshape(-1)
    ck = cache_k.reshape(n_phys, page, heads, head_dim)
    cv = cache_v.reshape(n_phys, page, heads, head_dim)
    total_pages = n_seq * n_pages
    out_pages = (n // MOBA_BLOCK) * OUT_PROJ_PAGES_PER_STEP
    conv_pages = (n // MOBA_BLOCK) * CONV_PAGES_PER_STEP
    attn_pages = total_pages - out_pages - conv_pages
    attn, kmean_a = _moba_prompt(q, k, v, sga, batch, seq, heads, pt_flat, ck, attn_pages)
    conv, kmean_c = _conv_prompt(u, sgc, *conv_w_args, seq=seq, tile=MOBA_BLOCK, page_table_flat=pt_flat, cache=ck,
                                 first_page=attn_pages, n_stream_pages=conv_pages)
    n_blocks = seq // MOBA_BLOCK

    def paired_rows(r, pt):
        blk = r % n_blocks
        mirror = n_blocks - 1 - blk
        return (r // n_blocks, jnp.minimum(blk, mirror), jnp.where(blk > mirror, 1, 0), 0, 0)

    y_prompt, kmean_o = _out_proj(xp, attn, pl.BlockSpec((1, 1, 1, MOBA_BLOCK, attn_w), paired_rows), conv,
                                  p_prompt[0].reshape(n, -1), *out_w_args, tm=MOBA_BLOCK, page_table_flat=pt_flat,
                                  cache=ck, first_page=total_pages - out_pages, n_stream_pages=out_pages)
    kmean = jnp.concatenate([kmean_a, kmean_c, kmean_o]).reshape(n_seq, n_pages // PAGES_PER_BLOCK, heads, head_dim)
    conv_prompt_new = u.reshape(batch, seq, conv_w)[:, seq - hist:, :]

    conv_s, conv_sample_new = _conv_sample(state_conv[0], us, sgcs, *conv_w_args)
    as4 = lambda a: a.reshape(n_seq, heads, head_dim)
    sel = _select_sample(as4(qs), kmean)[:, :, :, 0]
    attn_s = _moba_sample(sel.reshape(-1), pt_flat, as4(qs), as4(ks), as4(vs), as4(sgas), ck, cv, n_pages)
    (y_sample,) = _out_proj(xs, attn_s.reshape(n_seq, attn_w), pl.BlockSpec((n_seq, attn_w), lambda r, pt: (r, 0)),
                            conv_s, p_sample[0].reshape(n_seq, -1), *out_w_args, tm=n_seq, page_table_flat=pt_flat,
                            cache=ck, first_page=0, n_stream_pages=0)

    kv_shape = (depth, batch, seq, heads, head_dim)
    kvs_shape = (depth, n_seq, dec_seq, heads, head_dim)
    return (y_prompt.reshape(batch, seq, d_model), y_sample.reshape(n_seq, dec_seq, d_model),
            k.reshape(kv_shape), v.reshape(kv_shape), conv_prompt_new.reshape(depth, batch, hist, conv_w),
            ks.reshape(kvs_shape), vs.reshape(kvs_shape), conv_sample_new.reshape(depth, n_seq, hist, conv_w))
```

```python
import functools

import jax
import jax.numpy as jnp
from jax import lax
from jax.experimental import pallas as pl
from jax.experimental.pallas import tpu as pltpu

F32 = jnp.float32
BF16 = jnp.bfloat16

HEAD_DIM = 128
CONV_K = 31
MOBA_BLOCK = 256
MOBA_TOPK = 3
PAGE_SIZE = 128
PAGES_PER_BLOCK = MOBA_BLOCK // PAGE_SIZE
LN_EPS = 1e-5
NEG = -1e30
LOG2_E = 1.4426950408889634
LANES = 128
SUBLANES = 8
SUM_ROWS = 16
CONV_HALO = 32
OUT_PROJ_PAGES_PER_STEP = 16
CONV_PAGES_PER_STEP = 16
VMEM_LIMIT = 60 * 1024 * 1024


def _sigmoid(z):
    return 0.5 * jnp.tanh(0.5 * z) + 0.5


def _silu(z):
    return z * _sigmoid(z)


def _dot_nt(a, b, **kw):
    return lax.dot_general(a, b, (((1,), (1,)), ((), ())), preferred_element_type=F32, **kw)


def _in_proj_kernel(x_ref, xs_ref, w_ref, b_ref, *refs, finishers, n_out):
    outs_p, outs_s, (xb_ref, *hold) = refs[:n_out], refs[n_out:2 * n_out], refs[2 * n_out:]
    j = pl.program_id(1)
    tm = x_ref.shape[0]

    @pl.when(j == 0)
    def _():
        xb_ref[0:tm, :] = x_ref[...].astype(BF16)
        xb_ref[tm:, :] = xs_ref[...].astype(BF16)

    def write(k, value):
        outs_p[k][...] = value[0:tm].astype(outs_p[k].dtype)
        outs_s[k][...] = value[tm:].astype(outs_s[k].dtype)

    for g, finish in enumerate(finishers):
        @pl.when(j == g)
        def _(finish=finish):
            finish(jnp.dot(xb_ref[...], w_ref[...], preferred_element_type=F32) + b_ref[...], write, *hold)


def _emit(k, fn=lambda z: z):
    return lambda z, write, *hold: write(k, fn(z))


def _hold(z, write, hold_ref):
    hold_ref[...] = z


def _glu_into(k):
    return lambda z, write, hold_ref: write(k, hold_ref[...] * _sigmoid(z))


QKV_GROUPS = (_emit(0), _emit(1), _emit(2))
GATE_GROUPS = (_emit(0, _silu), _hold, _glu_into(1), _emit(2, _silu))


def _in_proj(x, xs, w_b, b, tm, first_group, finishers, out_dtypes, needs_hold):
    n, d = x.shape
    ns = xs.shape[0]
    gw = w_b.shape[1] // 7
    row = pl.BlockSpec((tm, gw), lambda i, j: (i, 0))
    fixed = pl.BlockSpec((ns, gw), lambda i, j: (0, 0))
    return pl.pallas_call(
        functools.partial(_in_proj_kernel, finishers=finishers, n_out=len(out_dtypes)),
        grid=(n // tm, len(finishers)),
        in_specs=[
            pl.BlockSpec((tm, d), lambda i, j: (i, 0)),
            pl.BlockSpec((ns, d), lambda i, j: (0, 0)),
            pl.BlockSpec((d, gw), lambda i, j: (0, first_group + j)),
            pl.BlockSpec((1, gw), lambda i, j: (0, first_group + j)),
        ],
        out_specs=[row] * len(out_dtypes) + [fixed] * len(out_dtypes),
        out_shape=([jax.ShapeDtypeStruct((n, gw), dt) for dt in out_dtypes]
                   + [jax.ShapeDtypeStruct((ns, gw), dt) for dt in out_dtypes]),
        scratch_shapes=([pltpu.VMEM((tm + ns, d), BF16)]
                        + ([pltpu.VMEM((tm + ns, gw), F32)] if needs_hold else [])),
        compiler_params=pltpu.CompilerParams(
            dimension_semantics=("arbitrary", "arbitrary"), vmem_limit_bytes=VMEM_LIMIT),
        name="in_proj",
    )(x, xs, w_b, b)


def _in_proj_all(x, xs, w_b, b, tm):
    q, k, v, qs, ks, vs = _in_proj(x, xs, w_b, b, tm, 0, QKV_GROUPS, (F32, F32, F32), False)
    sga, u, sgc, sgas, us, sgcs = _in_proj(x, xs, w_b, b, tm, len(QKV_GROUPS), GATE_GROUPS, (BF16, F32, BF16), True)
    return (q, k, v, sga, u, sgc), (qs, ks, vs, sgas, us, sgcs)


def _topk_mask(g, idx, n_valid, axis):
    sel = jnp.zeros(g.shape, F32)
    big = g.shape[axis]
    for t in range(MOBA_TOPK):
        m = jnp.max(g, axis=axis, keepdims=True)
        first = jnp.min(jnp.where(g == m, idx, big), axis=axis, keepdims=True)
        hit = idx == first
        counts = jnp.where(t < n_valid, 1.0, 0.0)
        sel = jnp.maximum(sel, jnp.where(hit, counts, 0.0))
        g = jnp.where(hit, -jnp.inf, g)
    return sel


def _cache_block_mean(page_refs, pmean_ref, a):
    tot = jnp.sum(page_refs[PAGES_PER_BLOCK * a][0], axis=0)
    for r in range(1, PAGES_PER_BLOCK):
        tot = tot + jnp.sum(page_refs[PAGES_PER_BLOCK * a + r][0], axis=0)
    pmean_ref[a] = tot * (1.0 / MOBA_BLOCK)


def _page_stream_specs(cache, step_of, first_page, pages_per_step):
    n_phys, page, heads, hd = cache.shape
    assert pages_per_step % PAGES_PER_BLOCK == 0
    blocks_per_step = pages_per_step // PAGES_PER_BLOCK

    def page_spec(r):
        return pl.BlockSpec((1, page, heads, hd),
                            lambda *ids: (ids[-1][first_page + step_of(*ids[:-1]) * pages_per_step + r], 0, 0, 0))

    in_specs = [page_spec(r) for r in range(pages_per_step)]
    out_spec = pl.BlockSpec((blocks_per_step, heads, hd), lambda *ids: (step_of(*ids[:-1]), 0, 0))
    return in_specs, out_spec, blocks_per_step


def _moba_prompt_kernel(pt_ref, qa_ref, qb_ref, qna_ref, qnb_ref, k_ref, v_ref, sgaa_ref, sgab_ref, *refs,
                        n_blocks, pages_per_step):
    del pt_ref
    page_refs = refs[:pages_per_step]
    o_ref, pmean_ref, kb_ref, vt_ref, kmean_ref, acc_ref = refs[pages_per_step:pages_per_step + 6]
    bufs = refs[pages_per_step + 6:]
    stage = (bufs[0:4], bufs[4:8])
    i = pl.program_id(2)
    blk = MOBA_BLOCK
    last = n_blocks - 1
    pairs = n_blocks // 2
    qscale = HEAD_DIM ** -0.5 * LOG2_E
    sga_refs = (sgaa_ref, sgab_ref)

    def past_slot(t, pair):
        first = t < pair
        return jnp.where(first, 1.0, 0.0), jnp.where(first, 0, 1), jnp.where(first, t, t - pair)

    def score_pass(pair, q_refs, buf, with_cache_stream):
        qt_ref, bias_ref, m_ref, s_ref = buf
        own = (pair, last - pair)
        for w in range(2):
            qf_t = q_refs[w][...].T
            qt_ref[w] = (qf_t * qscale).astype(BF16)
            gate = jnp.dot(kmean_ref[...], qf_t, preferred_element_type=F32, precision=lax.Precision.HIGHEST)
            row = lax.broadcasted_iota(jnp.int32, gate.shape, 0)
            gate = jnp.where(row < own[w], gate, -jnp.inf)
            sel = _topk_mask(gate, row, own[w], axis=0)
            bias_ref[w] = jnp.where(sel > 0.5, 0.0, NEG)
        k_id = lax.broadcasted_iota(jnp.int32, (blk, blk), 0)
        q_id = lax.broadcasted_iota(jnp.int32, (blk, blk), 1)
        m = []
        for w in range(2):
            s = jnp.dot(kb_ref[own[w]], qt_ref[w], preferred_element_type=F32)
            s = jnp.where(k_id <= q_id, s, NEG)
            s_ref[w] = s
            m.append(jnp.max(s, axis=0, keepdims=True))
        for t in range(last):
            fa, w, kblk = past_slot(t, pair)
            s = jnp.dot(kb_ref[kblk], qt_ref[w], preferred_element_type=F32)
            s_ref[2 + t] = s
            c = jnp.max(s, axis=0, keepdims=True) + bias_ref[w, pl.ds(kblk, 1), :]
            m[0] = jnp.maximum(m[0], c + (fa - 1.0) * -NEG)
            m[1] = jnp.maximum(m[1], c + fa * NEG)
            if with_cache_stream and t < pages_per_step // PAGES_PER_BLOCK:
                _cache_block_mean(page_refs, pmean_ref, t)
        for w in range(2):
            m_ref[w] = m[w]

    def value_pass(pair, buf):
        _, bias_ref, m_ref, s_ref = buf
        own = (pair, last - pair)
        m = [m_ref[0], m_ref[1]]
        for w in range(2):
            p = jnp.exp2((s_ref[w] - m[w]).astype(BF16))
            acc_ref[w] = jnp.dot(vt_ref[own[w]], p, preferred_element_type=F32)
        for t in range(last):
            fa, w, kblk = past_slot(t, pair)
            shift = bias_ref[w, pl.ds(kblk, 1), :] - (m[1] + fa * (m[0] - m[1]))
            p = jnp.exp2((s_ref[2 + t] + shift).astype(BF16))
            acc_ref[2 + t] = jnp.dot(vt_ref[kblk], p, preferred_element_type=F32)
        bounds = ((0, pair), (pair, last))
        for w in range(2):
            acc = lax.fori_loop(bounds[w][0], bounds[w][1], lambda t, a: a + acc_ref[2 + t], acc_ref[w])
            out_t = acc[0:HEAD_DIM, :] / acc[HEAD_DIM:HEAD_DIM + 1, :]
            o_ref[0, 0, w] = (out_t.T * sga_refs[w][...]).astype(o_ref.dtype)

    @pl.when(i == 0)
    def _():
        kmean_ref[...] = jnp.mean(k_ref[...].reshape(n_blocks, blk, HEAD_DIM), axis=1)
        for jb in range(n_blocks):
            rows = slice(jb * blk, (jb + 1) * blk)
            kb_ref[jb] = k_ref[rows, :].astype(BF16)
            vt_ref[jb, 0:HEAD_DIM, :] = v_ref[rows, :].T.astype(BF16)
            vt_ref[jb, HEAD_DIM:, :] = jnp.ones((SUM_ROWS, blk), BF16)
        score_pass(0, (qa_ref, qb_ref), stage[0], False)

    nxt = jnp.minimum(i + 1, pairs - 1)
    for parity in range(2):
        @pl.when(i % 2 == parity)
        def _():
            score_pass(nxt, (qna_ref, qnb_ref), stage[1 - parity], True)
            value_pass(i, stage[parity])


def _moba_prompt(q, k, v, sga, batch, seq, heads, page_table_flat, cache, n_stream_pages):
    n_blocks = seq // MOBA_BLOCK
    assert n_blocks % 2 == 0
    last = n_blocks - 1
    pairs = n_blocks // 2
    steps = batch * heads * pairs
    pages_per_step = n_stream_pages // steps
    assert pages_per_step * steps == n_stream_pages

    def step_id(b, h, i):
        return (b * heads + h) * pairs + i

    page_specs, pmean_spec, blocks_per_step = _page_stream_specs(cache, step_id, 0, pages_per_step)
    assert blocks_per_step <= last
    qa = pl.BlockSpec((MOBA_BLOCK, HEAD_DIM), lambda b, h, i, pt: (b * n_blocks + i, h))
    qb = pl.BlockSpec((MOBA_BLOCK, HEAD_DIM), lambda b, h, i, pt: (b * n_blocks + last - i, h))
    qna = pl.BlockSpec((MOBA_BLOCK, HEAD_DIM), lambda b, h, i, pt: (b * n_blocks + jnp.minimum(i + 1, pairs - 1), h))
    qnb = pl.BlockSpec((MOBA_BLOCK, HEAD_DIM),
                       lambda b, h, i, pt: (b * n_blocks + last - jnp.minimum(i + 1, pairs - 1), h))
    score_stage = [pltpu.VMEM((2, HEAD_DIM, MOBA_BLOCK), BF16),
                   pltpu.VMEM((2, n_blocks, MOBA_BLOCK), F32),
                   pltpu.VMEM((2, 1, MOBA_BLOCK), F32),
                   pltpu.VMEM((n_blocks + 1, MOBA_BLOCK, MOBA_BLOCK), F32)]
    kvspec = pl.BlockSpec((seq, HEAD_DIM), lambda b, h, i, pt: (b, h))
    return pl.pallas_call(
        functools.partial(_moba_prompt_kernel, n_blocks=n_blocks, pages_per_step=pages_per_step),
        grid_spec=pltpu.PrefetchScalarGridSpec(
            num_scalar_prefetch=1,
            grid=(batch, heads, pairs),
            in_specs=[qa, qb, qna, qnb, kvspec, kvspec, qa, qb] + page_specs,
            out_specs=[pl.BlockSpec((1, 1, 2, MOBA_BLOCK, HEAD_DIM), lambda b, h, i, pt: (b, i, 0, 0, h)), pmean_spec],
            scratch_shapes=[pltpu.VMEM((n_blocks, MOBA_BLOCK, HEAD_DIM), BF16),
                            pltpu.VMEM((n_blocks, HEAD_DIM + SUM_ROWS, MOBA_BLOCK), BF16),
                            pltpu.VMEM((n_blocks, HEAD_DIM), F32),
                            pltpu.VMEM((n_blocks + 1, HEAD_DIM + SUM_ROWS, MOBA_BLOCK), F32)] + score_stage * 2,
        ),
        out_shape=[jax.ShapeDtypeStruct((batch, pairs, 2, MOBA_BLOCK, q.shape[1]), BF16),
                   jax.ShapeDtypeStruct((steps * blocks_per_step,) + cache.shape[2:], F32)],
        compiler_params=pltpu.CompilerParams(
            dimension_semantics=("arbitrary", "arbitrary", "arbitrary"), vmem_limit_bytes=VMEM_LIMIT),
        name="moba_prompt",
    )(page_table_flat, q, q, q, q, k, v, sga, sga, *([cache] * pages_per_step))


def _conv_tail(c, sgc, gcn_ref, bcn_ref, wpw_ref, bpw_ref):
    mu = jnp.mean(c, axis=-1, keepdims=True)
    d = c - mu
    var = jnp.mean(d * d, axis=-1, keepdims=True)
    y = d * lax.rsqrt(var + LN_EPS) * gcn_ref[...] + bcn_ref[...]
    y = _silu(y)
    return (jnp.dot(y.astype(BF16), wpw_ref[...], preferred_element_type=F32) + bpw_ref[...]) * sgc


def _conv_prompt_kernel(pt_ref, u_ref, prev_ref, sgc_ref, wdw_ref, bdw_ref, gcn_ref, bcn_ref, wpw_ref, bpw_ref,
                        *refs, tiles_per_seq, rows, pages_per_step):
    del pt_ref
    page_refs = refs[:pages_per_step]
    o_ref, pmean_ref, ext_ref, sh_ref, c_ref = refs[pages_per_step:]
    for a in range(pages_per_step // PAGES_PER_BLOCK):
        _cache_block_mean(page_refs, pmean_ref, a)
    i = pl.program_id(0)
    t, w = u_ref.shape
    first = (i % tiles_per_seq) == 0
    ext_ref[0:CONV_HALO, :] = jnp.where(first, 0.0, prev_ref[...])
    ext_ref[CONV_HALO:, :] = u_ref[...]
    n_sh = sh_ref.shape[1]
    for r in range(1, SUBLANES):
        sh_ref[r - 1] = ext_ref[r:r + n_sh, :]
    off = CONV_HALO - (CONV_K - 1)
    for c in range(w // LANES):
        cs = slice(c * LANES, (c + 1) * LANES)
        for rc in range(t // rows):
            acc = jnp.broadcast_to(bdw_ref[:, cs], (rows, LANES))
            for j in range(CONV_K):
                a, r = divmod(off + j, SUBLANES)
                lo = rc * rows + a * SUBLANES
                win = ext_ref[lo:lo + rows, cs] if r == 0 else sh_ref[r - 1, lo:lo + rows, cs]
                acc = acc + win * wdw_ref[j:j + 1, cs]
            c_ref[rc * rows:(rc + 1) * rows, cs] = acc
    o_ref[...] = _conv_tail(c_ref[...], sgc_ref[...], gcn_ref, bcn_ref, wpw_ref, bpw_ref).astype(o_ref.dtype)


def _conv_prompt(u, sgc, w_dw, b_dw, g_cn, b_cn, w_pw_b, b_pw, seq, tile, page_table_flat, cache, first_page,
                 n_stream_pages):
    n, w = u.shape
    steps = n // tile
    pages_per_step = n_stream_pages // steps
    assert pages_per_step * steps == n_stream_pages and pages_per_step > 0
    halo_per_tile = tile // CONV_HALO
    full = lambda shape: pl.BlockSpec(shape, lambda i, pt: (0, 0))
    rowspec = pl.BlockSpec((tile, w), lambda i, pt: (i, 0))
    pages, pmean_spec, blocks_per_step = _page_stream_specs(cache, lambda i: i, first_page, pages_per_step)
    return pl.pallas_call(
        functools.partial(_conv_prompt_kernel, tiles_per_seq=seq // tile, rows=64, pages_per_step=pages_per_step),
        grid_spec=pltpu.PrefetchScalarGridSpec(
            num_scalar_prefetch=1,
            grid=(steps,),
            in_specs=[
                rowspec,
                pl.BlockSpec((CONV_HALO, w), lambda i, pt: (jnp.maximum(i * halo_per_tile - 1, 0), 0)),
                rowspec,
                full((CONV_K, w)), full((1, w)), full((1, w)), full((1, w)), full((w, w)), full((1, w)),
            ] + pages,
            out_specs=[rowspec, pmean_spec],
            scratch_shapes=[pltpu.VMEM((tile + CONV_HALO, w), F32),
                            pltpu.VMEM((SUBLANES - 1, tile + CONV_HALO - SUBLANES, w), F32),
                            pltpu.VMEM((tile, w), F32)],
        ),
        out_shape=[jax.ShapeDtypeStruct((n, w), BF16),
                   jax.ShapeDtypeStruct((steps * blocks_per_step,) + cache.shape[2:], F32)],
        compiler_params=pltpu.CompilerParams(dimension_semantics=("arbitrary",), vmem_limit_bytes=VMEM_LIMIT),
        name="conv_prompt",
    )(page_table_flat, u, u, sgc, w_dw, b_dw, g_cn, b_cn, w_pw_b, b_pw, *([cache] * pages_per_step))


def _conv_sample_kernel(state_ref, u_ref, sgc_ref, wdw_ref, bdw_ref, gcn_ref, bcn_ref, wpw_ref, bpw_ref,
                        o_ref, new_ref, c_ref):
    nb = state_ref.shape[0]
    hist = CONV_K - 1
    for b in range(nb):
        st = state_ref[b]
        u_row = u_ref[b:b + 1, :]
        c_ref[b:b + 1, :] = (jnp.sum(st * wdw_ref[0:hist, :], axis=0, keepdims=True)
                             + u_row * wdw_ref[hist:hist + 1, :] + bdw_ref[...])
        new_ref[b, 0:hist - 1, :] = state_ref[b, 1:hist, :]
        new_ref[b, hist - 1:hist, :] = u_row
    o_ref[...] = _conv_tail(c_ref[...], sgc_ref[...], gcn_ref, bcn_ref, wpw_ref, bpw_ref).astype(o_ref.dtype)


def _conv_sample(state, u, sgc, w_dw, b_dw, g_cn, b_cn, w_pw_b, b_pw):
    nb, hist, w = state.shape
    full2 = lambda shape: pl.BlockSpec(shape, lambda i: (0, 0))
    full3 = lambda shape: pl.BlockSpec(shape, lambda i: (0, 0, 0))
    return pl.pallas_call(
        _conv_sample_kernel,
        grid=(1,),
        in_specs=[full3((nb, hist, w)), full2((nb, w)), full2((nb, w)),
                  full2((CONV_K, w)), full2((1, w)), full2((1, w)), full2((1, w)), full2((w, w)), full2((1, w))],
        out_specs=[full2((nb, w)), full3((nb, hist, w))],
        out_shape=[jax.ShapeDtypeStruct((nb, w), BF16), jax.ShapeDtypeStruct((nb, hist, w), F32)],
        scratch_shapes=[pltpu.VMEM((nb, w), F32)],
        compiler_params=pltpu.CompilerParams(dimension_semantics=("arbitrary",), vmem_limit_bytes=VMEM_LIMIT),
        name="conv_sample",
    )(state, u, sgc, w_dw, b_dw, g_cn, b_cn, w_pw_b, b_pw)


def _out_kernel(pt_ref, x_ref, a_ref, c_ref, p_ref, woa_ref, woc_ref, bo_ref, g_ref, b_ref, wpg_ref, bpg_ref, wpe_ref,
                *refs, alpha, pages_per_step):
    del pt_ref
    page_refs, o_ref = refs[:pages_per_step], refs[pages_per_step]
    for a in range(pages_per_step // PAGES_PER_BLOCK):
        _cache_block_mean(page_refs, refs[pages_per_step + 1], a)
    attn = a_ref[...].reshape(c_ref.shape).astype(BF16)
    mix = (jnp.dot(attn, woa_ref[...], preferred_element_type=F32)
           + jnp.dot(c_ref[...].astype(BF16), woc_ref[...], preferred_element_type=F32) + bo_ref[...])
    t = alpha * x_ref[...] + mix
    mu = jnp.mean(t, axis=-1, keepdims=True)
    d = t - mu
    var = jnp.mean(d * d, axis=-1, keepdims=True)
    h = d * lax.rsqrt(var + LN_EPS) * g_ref[...] + b_ref[...]
    gate = _sigmoid(jnp.dot(h.astype(BF16), wpg_ref[...], preferred_element_type=F32) + bpg_ref[...])
    pe = jnp.dot(p_ref[...].astype(BF16), wpe_ref[...], preferred_element_type=F32)
    o_ref[...] = h + gate * pe


def _out_proj(x, attn, attn_spec, conv, p, w_out_b, b_out, g_ln, b_ln, w_pg_b, b_pg, w_pe_b, alpha, tm,
              page_table_flat, cache, first_page, n_stream_pages):
    n, d = x.shape
    half = conv.shape[1]
    pd = p.shape[1]
    steps = n // tm
    pages_per_step = n_stream_pages // steps
    assert pages_per_step * steps == n_stream_pages
    const = lambda shape, r=0: pl.BlockSpec(shape, lambda i, pt: (r, 0), pipeline_mode=pl.Buffered(1))
    row = lambda w: pl.BlockSpec((tm, w), lambda i, pt: (i, 0))
    out_specs, out_shape, pages = [row(d)], [jax.ShapeDtypeStruct((n, d), F32)], []
    if pages_per_step:
        pages, pmean_spec, blocks_per_step = _page_stream_specs(cache, lambda i: i, first_page, pages_per_step)
        out_specs.append(pmean_spec)
        out_shape.append(jax.ShapeDtypeStruct((steps * blocks_per_step,) + cache.shape[2:], F32))
    return pl.pallas_call(
        functools.partial(_out_kernel, alpha=alpha, pages_per_step=pages_per_step),
        grid_spec=pltpu.PrefetchScalarGridSpec(
            num_scalar_prefetch=1,
            grid=(steps,),
            in_specs=[row(d), attn_spec, row(half), row(pd),
                      const((half, d), 0), const((half, d), 1), const((1, d)), const((1, d)), const((1, d)),
                      const((d, d)), const((1, d)), const((pd, d))] + pages,
            out_specs=out_specs,
        ),
        out_shape=out_shape,
        compiler_params=pltpu.CompilerParams(dimension_semantics=("arbitrary",), vmem_limit_bytes=VMEM_LIMIT),
        name="out_proj",
    )(page_table_flat, x, attn, conv, p, w_out_b, w_out_b, b_out, g_ln, b_ln, w_pg_b, b_pg, w_pe_b,
      *([cache] * pages_per_step))


def _select_kernel(q_ref, kmean_ref, o_ref):
    g = jnp.sum(kmean_ref[...] * q_ref[...][:, None], axis=-1, keepdims=True)
    n_blocks = g.shape[1]
    idx = lax.broadcasted_iota(jnp.int32, g.shape, 1)
    for t in range(MOBA_TOPK):
        m = jnp.max(g, axis=1, keepdims=True)
        first = jnp.min(jnp.where(g == m, idx, n_blocks), axis=1, keepdims=True)
        o_ref[:, t] = jnp.broadcast_to(first[:, 0], o_ref.shape[:1] + o_ref.shape[2:])
        g = jnp.where(idx == first, -jnp.inf, g)


def _select_sample(q4, kmean):
    n_seq, n_blocks, heads, hd = kmean.shape
    assert n_blocks >= MOBA_TOPK
    return pl.pallas_call(
        _select_kernel,
        grid=(1,),
        in_specs=[pl.BlockSpec((n_seq, heads, hd), lambda b: (0, 0, 0)),
                  pl.BlockSpec((n_seq, n_blocks, heads, hd), lambda b: (0, 0, 0, 0))],
        out_specs=pl.BlockSpec((n_seq, MOBA_TOPK, heads, LANES), lambda b: (0, 0, 0, 0)),
        out_shape=jax.ShapeDtypeStruct((n_seq, MOBA_TOPK, heads, LANES), jnp.int32),
        compiler_params=pltpu.CompilerParams(dimension_semantics=("arbitrary",), vmem_limit_bytes=VMEM_LIMIT),
        name="select_sample",
    )(q4, kmean)


def _moba_sample_kernel(sel_ref, pt_ref, q_ref, kn_ref, vn_ref, sga_ref, ck_ref, cv_ref, o_ref, kbuf, vbuf, sems,
                        *, n_seq, n_pages, heads):
    b = pl.program_id(0)
    n_sel = MOBA_TOPK * PAGES_PER_BLOCK
    scale = HEAD_DIM ** -0.5

    def slab_copies(seq, slot):
        copies = []
        for h in range(heads):
            for t in range(MOBA_TOPK):
                blk = sel_ref[(seq * MOBA_TOPK + t) * heads + h]
                for r in range(PAGES_PER_BLOCK):
                    page = pt_ref[seq * n_pages + blk * PAGES_PER_BLOCK + r]
                    dst = t * PAGES_PER_BLOCK + r
                    copies.append(pltpu.make_async_copy(
                        ck_ref.at[page, :, h, :], kbuf.at[slot, h, dst], sems.at[slot, 0, h]))
                    copies.append(pltpu.make_async_copy(
                        cv_ref.at[page, :, h, :], vbuf.at[slot, h, dst], sems.at[slot, 1, h]))
        return copies

    slot = b % 2

    def start_all(copies):
        for n, c in enumerate(copies):
            c.start(priority=n % 2)

    @pl.when(b == 0)
    def _():
        start_all(slab_copies(b, slot))

    @pl.when(b + 1 < n_seq)
    def _():
        start_all(slab_copies(b + 1, 1 - slot))

    for c in slab_copies(b, slot):
        c.wait()

    for h in range(heads):
        hs = slice(h, h + 1)
        qh = q_ref[0, hs, :]
        kh = kbuf[slot, h].reshape(n_sel * PAGE_SIZE, HEAD_DIM).astype(BF16)
        vh = vbuf[slot, h].reshape(n_sel * PAGE_SIZE, HEAD_DIM).astype(BF16)
        s = _dot_nt(qh.astype(BF16), kh) * scale
        s_new = jnp.sum(qh * kn_ref[0, hs, :], axis=1, keepdims=True) * scale
        m = jnp.maximum(jnp.max(s, axis=1, keepdims=True), s_new)
        p = jnp.exp(s - m)
        p_new = jnp.exp(s_new - m)
        l = jnp.sum(p, axis=1, keepdims=True) + p_new
        acc = jnp.dot(p.astype(BF16), vh, preferred_element_type=F32) + p_new * vn_ref[0, hs, :]
        o_ref[0, hs, :] = acc / l * sga_ref[0, hs, :]


def _moba_sample(sel_flat, page_table_flat, q4, kn4, vn4, sga4, cache_k, cache_v, n_pages):
    n_seq, heads, hd = q4.shape
    n_sel = MOBA_TOPK * PAGES_PER_BLOCK
    vec = pl.BlockSpec((1, heads, hd), lambda b, sel, pt: (b, 0, 0))
    hbm = pl.BlockSpec(memory_space=pl.ANY)
    return pl.pallas_call(
        functools.partial(_moba_sample_kernel, n_seq=n_seq, n_pages=n_pages, heads=heads),
        grid_spec=pltpu.PrefetchScalarGridSpec(
            num_scalar_prefetch=2,
            grid=(n_seq,),
            in_specs=[vec, vec, vec, vec, hbm, hbm],
            out_specs=vec,
            scratch_shapes=[pltpu.VMEM((2, heads, n_sel, PAGE_SIZE, hd), F32),
                            pltpu.VMEM((2, heads, n_sel, PAGE_SIZE, hd), F32),
                            pltpu.SemaphoreType.DMA((2, 2, heads))],
        ),
        out_shape=jax.ShapeDtypeStruct(q4.shape, F32),
        compiler_params=pltpu.CompilerParams(dimension_semantics=("arbitrary",), vmem_limit_bytes=VMEM_LIMIT),
        name="moba_sample",
    )(sel_flat, page_table_flat, q4, kn4, vn4, sga4, cache_k, cache_v)


def kernel(x_prompt, x_sample, p_prompt, p_sample, cache_k, cache_v, state_conv, page_table, w_in, b_in, w_dw,
           b_dw, g_cn, b_cn, w_pw, b_pw, w_out, b_out, g_ln, b_ln, w_pe, w_pg, b_pg):
    depth = w_in.shape[0]
    assert depth == 1
    batch, seq, d_model = x_prompt.shape
    n_seq, dec_seq, _ = x_sample.shape
    assert dec_seq == 1
    n_phys, page, heads, head_dim = cache_k.shape[1:]
    assert head_dim == HEAD_DIM and page == PAGE_SIZE
    attn_w = heads * head_dim
    conv_w = w_pw.shape[1]
    assert conv_w == attn_w and w_in.shape[2] == 7 * attn_w
    n_pages = page_table.shape[1]
    hist = CONV_K - 1
    alpha = (2 * depth) ** 0.25
    n = batch * seq

    row2 = lambda a: a.reshape(1, -1)
    w_in_b = w_in[0].astype(BF16)
    w_pw_b = w_pw[0].astype(BF16)
    w_out_b = w_out[0].astype(BF16)
    w_pg_b = w_pg[0].astype(BF16)
    w_pe_b = w_pe[0].astype(BF16)
    b_in2, b_dw2, g_cn2, b_cn2, b_pw2 = row2(b_in[0]), row2(b_dw[0]), row2(g_cn[0]), row2(b_cn[0]), row2(b_pw[0])
    b_out2, g_ln2, b_ln2, b_pg2 = row2(b_out[0]), row2(g_ln[0]), row2(b_ln[0]), row2(b_pg[0])
    conv_w_args = (w_dw[0], b_dw2, g_cn2, b_cn2, w_pw_b, b_pw2)
    out_w_args = (w_out_b, b_out2, g_ln2, b_ln2, w_pg_b, b_pg2, w_pe_b, alpha)

    xp = x_prompt.reshape(n, d_model)
    xs = x_sample.reshape(n_seq, d_model)
    (q, k, v, sga, u, sgc), (qs, ks, vs, sgas, us, sgcs) = _in_proj_all(xp, xs, w_in_b, b_in2, tm=1024)
    pt_flat = page_table.reshape(-1)
    ck = cache_k.reshape(n_phys, page, heads, head_dim)
    cv = cache_v.reshape(n_phys, page, heads, head_dim)
    total_pages = n_seq * n_pages
    out_pages = (n // MOBA_BLOCK) * OUT_PROJ_PAGES_PER_STEP
    conv_pages = (n // MOBA_BLOCK) * CONV_PAGES_PER_STEP
    attn_pages = total_pages - out_pages - conv_pages
    attn, kmean_a = _moba_prompt(q, k, v, sga, batch, seq, heads, pt_flat, ck, attn_pages)
    conv, kmean_c = _conv_prompt(u, sgc, *conv_w_args, seq=seq, tile=MOBA_BLOCK, page_table_flat=pt_flat, cache=ck,
                                 first_page=attn_pages, n_stream_pages=conv_pages)
    n_blocks = seq // MOBA_BLOCK

    def paired_rows(r, pt):
        blk = r % n_blocks
        mirror = n_blocks - 1 - blk
        return (r // n_blocks, jnp.minimum(blk, mirror), jnp.where(blk > mirror, 1, 0), 0, 0)

    y_prompt, kmean_o = _out_proj(xp, attn, pl.BlockSpec((1, 1, 1, MOBA_BLOCK, attn_w), paired_rows), conv,
                                  p_prompt[0].reshape(n, -1), *out_w_args, tm=MOBA_BLOCK, page_table_flat=pt_flat,
                                  cache=ck, first_page=total_pages - out_pages, n_stream_pages=out_pages)
    kmean = jnp.concatenate([kmean_a, kmean_c, kmean_o]).reshape(n_seq, n_pages // PAGES_PER_BLOCK, heads, head_dim)
    conv_prompt_new = u.reshape(batch, seq, conv_w)[:, seq - hist:, :]

    conv_s, conv_sample_new = _conv_sample(state_conv[0], us, sgcs, *conv_w_args)
    as4 = lambda a: a.reshape(n_seq, heads, head_dim)
    sel = _select_sample(as4(qs), kmean)[:, :, :, 0]
    attn_s = _moba_sample(sel.reshape(-1), pt_flat, as4(qs), as4(ks), as4(vs), as4(sgas), ck, cv, n_pages)
    (y_sample,) = _out_proj(xs, attn_s.reshape(n_seq, attn_w), pl.BlockSpec((n_seq, attn_w), lambda r, pt: (r, 0)),
                            conv_s, p_sample[0].reshape(n_seq, -1), *out_w_args, tm=n_seq, page_table_flat=pt_flat,
                            cache=ck, first_page=0, n_stream_pages=0)

    kv_shape = (depth, batch, seq, heads, head_dim)
    kvs_shape = (depth, n_seq, dec_seq, heads, head_dim)
    return (y_prompt.reshape(batch, seq, d_model), y_sample.reshape(n_seq, dec_seq, d_model),
            k.reshape(kv_shape), v.reshape(kv_shape), conv_prompt_new.reshape(depth, batch, hist, conv_w),
            ks.reshape(kvs_shape), vs.reshape(kvs_shape), conv_sample_new.reshape(depth, n_seq, hist, conv_w))
```
